```python
import math
import jax
import jax.numpy as jnp
from jax import lax
import numpy as np

D_MODEL = 2048
BATCH = 4
SEQ = 2048
DEPTH = 2
DEC_BATCH = 8
DEC_SEQ = 1
PAST_LEN = 16384
PAGE_SIZE = 128

N_EVEN = (DEPTH + 1) // 2
N_ODD = DEPTH // 2
EPS = 1e-6
H_A = 8
DK_A = D_MODEL // (4 * H_A)
DV_A = 2 * DK_A
Q_BLOCK = 128
H_B = 4
DK_B = D_MODEL // (2 * H_B)
DV_B = D_MODEL // (2 * H_B)
RET_CHUNK = 128
ROPE_BASE = 10000.0
EVEN_IN_WIDTHS = (H_A * 2 * DK_A, H_A * 2 * DK_A, H_A * DV_A,
                  H_B * DK_B, H_B * DK_B, H_B * DV_B, H_B * DV_B)
EVEN_IN = sum(EVEN_IN_WIDTHS)
EVEN_SPLITS = tuple(int(s) for s in np.cumsum(EVEN_IN_WIDTHS)[:-1])
D_MIX_EVEN = H_A * DV_A + H_B * DV_B
LRU_WIDTH = D_MODEL
N_LRU_BLOCKS = 8
LRU_BW = LRU_WIDTH // N_LRU_BLOCKS
CONV_W = 4
LRU_C = 8.0
D_FF = 256 * ((8 * D_MODEL // 3 + 255) // 256)
N_EXPERTS = 8
TOP_K = 2
D_FF_EXPERT = 7 * D_MODEL // 2
MOE_BLOCK = 128

kernel_name = 'hybrid_diffattn_retention_rglru_moe_step'


def rmsnorm(x, g):
    xf = x.astype(jnp.float32)
    return xf * lax.rsqrt(jnp.mean(xf * xf, axis=-1, keepdims=True) + EPS) * g


def head_groupnorm(o, g):
    mu = jnp.mean(o, axis=-1, keepdims=True)
    var = jnp.mean(jnp.square(o - mu), axis=-1, keepdims=True)
    return (o - mu) * lax.rsqrt(var + EPS) * g


def adaln(c, w, b):
    m = jax.nn.silu(c.astype(jnp.float32)) @ w + b
    return jnp.split(m[:, None, :], 6, axis=-1)


def modulate(x, g, shift, scale):
    return rmsnorm(x, g) * (1.0 + scale) + shift


def rope(x, pos):
    half = x.shape[-1] // 2
    freqs = ROPE_BASE ** (-jnp.arange(half, dtype=jnp.float32) / half)
    ang = pos.astype(jnp.float32)[:, None] * freqs[None, :]
    cos = jnp.cos(ang)[None, :, None, :]
    sin = jnp.sin(ang)[None, :, None, :]
    x1, x2 = x[..., :half], x[..., half:]
    return jnp.concatenate([x1 * cos - x2 * sin, x1 * sin + x2 * cos], axis=-1)


def diff_attn_prompt(q, k, v, lam):
    B, L = q.shape[:2]
    nb = L // Q_BLOCK
    sc = DK_A ** -0.5
    qb = q.reshape(B, nb, Q_BLOCK, H_A, 2, DK_A).swapaxes(0, 1)
    kpos = jnp.arange(L)

    def block(args):
        qi, start = args
        s = jnp.einsum('bqhcd,bkhcd->bhcqk', qi, k).astype(jnp.float32) * sc
        qpos = start + jnp.arange(Q_BLOCK)
        s = jnp.where(kpos[None, :] <= qpos[:, None], s, -jnp.inf)
        p = jax.nn.softmax(s, axis=-1)
        attn = p[:, :, 0] - lam * p[:, :, 1]
        return jnp.einsum('bhqk,bkhv->bqhv', attn, v)

    o = lax.map(block, (qb, jnp.arange(nb) * Q_BLOCK))
    return o.swapaxes(0, 1).reshape(B, L, H_A, DV_A)


def diff_attn_sample(q, k, v, k_past, v_past, lam):
    P = k_past.shape[1]
    L = q.shape[1]
    sc = DK_A ** -0.5
    s_past = jnp.einsum('bqhcd,bkhcd->bhcqk', q, k_past).astype(jnp.float32) * sc
    s_new = jnp.einsum('bqhcd,bkhcd->bhcqk', q, k).astype(jnp.float32) * sc
    causal = jnp.arange(L)[None, :] <= jnp.arange(L)[:, None]
    s_new = jnp.where(causal, s_new, -jnp.inf)
    p = jax.nn.softmax(jnp.concatenate([s_past, s_new], axis=-1), axis=-1)
    attn = p[:, :, 0] - lam * p[:, :, 1]
    return (jnp.einsum('bhqk,bkhv->bqhv', attn[..., :P], v_past)
            + jnp.einsum('bhqk,bkhv->bqhv', attn[..., P:], v))


def retention_chunk(q, k, v, s0, log_gamma):
    L = q.shape[1]
    idx = jnp.arange(L)
    d = idx[:, None] - idx[None, :]
    dpos = jnp.where(d >= 0, d, 0).astype(jnp.float32)
    decay = jnp.where(d >= 0, jnp.exp(dpos[None] * log_gamma[:, None, None]), 0.0)
    scores = jnp.einsum('blhk,bmhk->bhlm', q, k) * decay[None]
    intra = jnp.einsum('bhlm,bmhv->blhv', scores, v)
    in_dec = jnp.exp((idx + 1).astype(jnp.float32)[:, None] * log_gamma[None, :])
    inter = jnp.einsum('blhk,bhkv->blhv', q, s0) * in_dec[None, :, :, None]
    k_dec = jnp.exp((L - 1 - idx).astype(jnp.float32)[:, None] * log_gamma[None, :])
    s_new = (jnp.exp(L * log_gamma)[None, :, None, None] * s0
             + jnp.einsum('blhk,blhv->bhkv', k * k_dec[None, :, :, None], v))
    return intra + inter, s_new


def retention(q, k, v, s0, log_gamma):
    B, L = q.shape[:2]
    C = RET_CHUNK if L % RET_CHUNK == 0 else L
    nc = L // C

    def to_chunks(t):
        return t.reshape(B, nc, C, *t.shape[2:]).swapaxes(0, 1)

    def step(s, qkv):
        o, s = retention_chunk(qkv[0], qkv[1], qkv[2], s, log_gamma)
        return s, o

    s_final, o = lax.scan(step, s0.astype(jnp.float32), (to_chunks(q), to_chunks(k), to_chunks(v)))
    return o.swapaxes(0, 1).reshape(B, L, H_B, DV_B), s_final


def even_mixer(h, pos, w_in, w_out, lam_qk, subln_a, gn_b, lam_init, log_gamma, kv_past, s0):
    B, L, _ = h.shape
    z = h @ w_in
    qa, ka, va, qb, kb, vb, gb = jnp.split(z, EVEN_SPLITS, axis=-1)
    qa = qa.reshape(B, L, H_A, 2, DK_A)
    ka = ka.reshape(B, L, H_A, 2, DK_A)
    va = va.reshape(B, L, H_A, DV_A)
    lam = (jnp.exp(jnp.sum(lam_qk[0] * lam_qk[1])) - jnp.exp(jnp.sum(lam_qk[2] * lam_qk[3]))
           + lam_init).astype(jnp.float32)
    if kv_past is None:
        oa = diff_attn_prompt(qa, ka, va, lam)
    else:
        oa = diff_attn_sample(qa, ka, va, kv_past[0], kv_past[1], lam)
    oa = rmsnorm(oa, subln_a.reshape(H_A, DV_A)) * (1.0 - lam_init)
    qr = rope(qb.reshape(B, L, H_B, DK_B).astype(jnp.float32), pos) * DK_B ** -0.5
    kr = rope(kb.reshape(B, L, H_B, DK_B).astype(jnp.float32), pos)
    ob, s_new = retention(qr, kr, vb.reshape(B, L, H_B, DV_B).astype(jnp.float32), s0, log_gamma)
    ob = head_groupnorm(ob, gn_b.reshape(H_B, DV_B)) * jax.nn.silu(gb.astype(jnp.float32)).reshape(B, L, H_B, DV_B)
    o = jnp.concatenate([oa.reshape(B, L, -1), ob.reshape(B, L, -1)], axis=-1)
    return o @ w_out, ka, va, s_new


def lru_scan(a, b, h0):
    b = b.at[:, 0].add(a[:, 0] * h0)

    def comb(x, y):
        return x[0] * y[0], y[0] * x[1] + y[1]

    _, hs = lax.associative_scan(comb, (a, b), axis=1)
    return hs


def rglru_mixer(h, w_in, w_out, conv_w, conv_b, w_ga, b_ga, w_gx, b_gx, lru_lambda, conv_buf, h0):
    B, L, _ = h.shape
    z = h @ w_in
    gate_br, x_br = jnp.split(z, 2, axis=-1)
    xp = jnp.concatenate([conv_buf.astype(x_br.dtype), x_br], axis=1)
    xc = conv_b + sum(conv_w[j] * xp[:, j:j + L] for j in range(CONV_W))
    new_buf = xp[:, L:]
    xc = xc.astype(jnp.float32)
    xb = xc.reshape(B, L, N_LRU_BLOCKS, LRU_BW)
    r = jax.nn.sigmoid(jnp.einsum('blnk,nkj->blnj', xb, w_ga).reshape(B, L, LRU_WIDTH) + b_ga)
    gi = jax.nn.sigmoid(jnp.einsum('blnk,nkj->blnj', xb, w_gx).reshape(B, L, LRU_WIDTH) + b_gx)
    log_a = -LRU_C * r * jax.nn.softplus(-lru_lambda)
    a = jnp.exp(log_a)
    mult = jnp.sqrt(jnp.maximum(-jnp.expm1(2.0 * log_a), 0.0))
    hs = lru_scan(a, mult * gi * xc, h0.astype(jnp.float32))
    y = jax.nn.gelu(gate_br.astype(jnp.float32)) * hs
    return y @ w_out, new_buf, hs[:, -1]


def swiglu(h, w1, w3, w2):
    return (jax.nn.silu(h @ w1) * (h @ w3)) @ w2


def moe_swiglu(h, w_router, w1, w3, w2):
    B, L, D = h.shape
    T = B * L
    x = h.reshape(T, D)
    logits = (x @ w_router).astype(jnp.float32)
    top_val, top_idx = lax.top_k(logits, TOP_K)
    gates = jax.nn.softmax(top_val, axis=-1)
    n_slots = T * TOP_K
    e_flat = top_idx.reshape(-1)
    tok_flat = jnp.arange(n_slots, dtype=jnp.int32) // TOP_K
    order = jnp.argsort(e_flat)
    e_sorted = e_flat[order]
    tok_sorted = tok_flat[order]
    counts = jnp.bincount(e_flat, length=N_EXPERTS)
    padded = (counts + MOE_BLOCK - 1) // MOE_BLOCK * MOE_BLOCK
    starts = jnp.cumsum(counts) - counts
    pends = jnp.cumsum(padded)
    pstarts = pends - padded
    dest = pstarts[e_sorted] + (jnp.arange(n_slots) - starts[e_sorted])
    n_blocks = -(-n_slots // MOE_BLOCK) + N_EXPERTS
    rows = jnp.zeros((n_blocks * MOE_BLOCK, D), x.dtype).at[dest].set(x[tok_sorted])
    blk_e = jnp.minimum(jnp.searchsorted(pends, jnp.arange(n_blocks) * MOE_BLOCK, side='right'), N_EXPERTS - 1)

    def expert_block(args):
        xb, e = args
        return (jax.nn.silu(xb @ w1[e]) * (xb @ w3[e])) @ w2[e]

    out = lax.map(expert_block, (rows.reshape(n_blocks, MOE_BLOCK, D), blk_e)).reshape(-1, D)
    y_slots = out[dest] * gates.reshape(-1)[order][:, None]
    y = jnp.zeros((T, D), y_slots.dtype).at[tok_sorted].add(y_slots)
    return y.reshape(B, L, D)


def setup_inputs(seed: int = 0) -> dict:
    key = jax.random.key(seed)
    ks = iter(jax.random.split(key, 64))
    f32 = jnp.float32

    def nrm(shape, scale):
        return jax.random.normal(next(ks), shape, f32) * scale

    def gain(shape):
        return 1.0 + nrm(shape, 0.02)

    n_pages = PAST_LEN // PAGE_SIZE
    n_used = DEC_BATCH * n_pages
    n_phys = n_used + max(1, n_used // 4)
    page_table = jax.random.permutation(next(ks), n_phys)[:n_used].reshape(DEC_BATCH, n_pages).astype(jnp.int32)
    a0 = jax.random.uniform(next(ks), (N_ODD, LRU_WIDTH), f32, 0.9, 0.999)
    lru_lambda = jnp.log(a0) - jnp.log1p(-a0)
    return {
        'x_prompt': nrm((BATCH, SEQ, D_MODEL), 1.0),
        'x_sample': nrm((DEC_BATCH, DEC_SEQ, D_MODEL), 1.0),
        'c_prompt': nrm((BATCH, D_MODEL), 1.0),
        'c_sample': nrm((DEC_BATCH, D_MODEL), 1.0),
        'cache_k': nrm((N_EVEN, n_phys, PAGE_SIZE, H_A, 2, DK_A), 1.0),
        'cache_v': nrm((N_EVEN, n_phys, PAGE_SIZE, H_A, DV_A), 1.0),
        'state_ret': nrm((N_EVEN, DEC_BATCH, H_B, DK_B, DV_B), 0.3),
        'state_conv': nrm((N_ODD, DEC_BATCH, CONV_W - 1, LRU_WIDTH), 1.0),
        'state_lru': nrm((N_ODD, DEC_BATCH, LRU_WIDTH), 0.5),
        'page_table': page_table,
        'ada_w_even': nrm((N_EVEN, D_MODEL, 6 * D_MODEL), 0.5 * D_MODEL ** -0.5),
        'ada_b_even': nrm((N_EVEN, 6 * D_MODEL), 0.02),
        'norm_mix_even': gain((N_EVEN, D_MODEL)),
        'norm_ffn_even': gain((N_EVEN, D_MODEL)),
        'w_in_even': nrm((N_EVEN, D_MODEL, EVEN_IN), D_MODEL ** -0.5),
        'lambda_qk': nrm((N_EVEN, 4, DK_A), 0.1),
        'subln_a': gain((N_EVEN, H_A * DV_A)),
        'gn_b': gain((N_EVEN, H_B * DV_B)),
        'w_out_even': nrm((N_EVEN, D_MIX_EVEN, D_MODEL), D_MIX_EVEN ** -0.5),
        'ffn_w1': nrm((N_EVEN, D_MODEL, D_FF), D_MODEL ** -0.5),
        'ffn_w3': nrm((N_EVEN, D_MODEL, D_FF), D_MODEL ** -0.5),
        'ffn_w2': nrm((N_EVEN, D_FF, D_MODEL), D_FF ** -0.5),
        'ada_w_odd': nrm((N_ODD, D_MODEL, 6 * D_MODEL), 0.5 * D_MODEL ** -0.5),
        'ada_b_odd': nrm((N_ODD, 6 * D_MODEL), 0.02),
        'norm_mix_odd': gain((N_ODD, D_MODEL)),
        'norm_ffn_odd': gain((N_ODD, D_MODEL)),
        'w_in_odd': nrm((N_ODD, D_MODEL, 2 * LRU_WIDTH), D_MODEL ** -0.5),
        'conv_w': nrm((N_ODD, CONV_W, LRU_WIDTH), CONV_W ** -0.5),
        'conv_b': nrm((N_ODD, LRU_WIDTH), 0.02),
        'w_gate_a': nrm((N_ODD, N_LRU_BLOCKS, LRU_BW, LRU_BW), LRU_BW ** -0.5),
        'b_gate_a': nrm((N_ODD, LRU_WIDTH), 0.02),
        'w_gate_x': nrm((N_ODD, N_LRU_BLOCKS, LRU_BW, LRU_BW), LRU_BW ** -0.5),
        'b_gate_x': nrm((N_ODD, LRU_WIDTH), 0.02),
        'lru_lambda': lru_lambda,
        'w_out_odd': nrm((N_ODD, LRU_WIDTH, D_MODEL), LRU_WIDTH ** -0.5),
        'router': nrm((N_ODD, D_MODEL, N_EXPERTS), D_MODEL ** -0.5),
        'moe_w1': nrm((N_ODD, N_EXPERTS, D_MODEL, D_FF_EXPERT), D_MODEL ** -0.5),
        'moe_w3': nrm((N_ODD, N_EXPERTS, D_MODEL, D_FF_EXPERT), D_MODEL ** -0.5),
        'moe_w2': nrm((N_ODD, N_EXPERTS, D_FF_EXPERT, D_MODEL), D_FF_EXPERT ** -0.5),
        'norm_final': gain((D_MODEL,)),
    }


def reference(x_prompt, x_sample, c_prompt, c_sample, cache_k, cache_v, state_ret, state_conv, state_lru,
              page_table, ada_w_even, ada_b_even, norm_mix_even, norm_ffn_even, w_in_even, lambda_qk, subln_a,
              gn_b, w_out_even, ffn_w1, ffn_w3, ffn_w2, ada_w_odd, ada_b_odd, norm_mix_odd, norm_ffn_odd,
              w_in_odd, conv_w, conv_b, w_gate_a, b_gate_a, w_gate_x, b_gate_x, lru_lambda, w_out_odd, router,
              moe_w1, moe_w3, moe_w2, norm_final):
    B_p, L_p = x_prompt.shape[:2]
    B_s, L_s = x_sample.shape[:2]
    pos_p = jnp.arange(L_p)
    pos_s = PAST_LEN + jnp.arange(L_s)
    log_gamma = jnp.log1p(-jnp.exp2(-5.0 - jnp.arange(H_B, dtype=jnp.float32)))

    def even_layer(x, c, pos, kv_past, s0, i, lam_init):
        sh1, sc1, g1, sh2, sc2, g2 = adaln(c, ada_w_even[i], ada_b_even[i])
        h = modulate(x, norm_mix_even[i], sh1, sc1)
        o, k_new, v_new, s_new = even_mixer(h, pos, w_in_even[i], w_out_even[i], lambda_qk[i], subln_a[i],
                                            gn_b[i], lam_init, log_gamma, kv_past, s0)
        x = x + g1 * o
        h = modulate(x, norm_ffn_even[i], sh2, sc2)
        x = x + g2 * swiglu(h, ffn_w1[i], ffn_w3[i], ffn_w2[i])
        return x, k_new, v_new, s_new

    def odd_layer(x, c, conv_buf, h0, i):
        sh1, sc1, g1, sh2, sc2, g2 = adaln(c, ada_w_odd[i], ada_b_odd[i])
        h = modulate(x, norm_mix_odd[i], sh1, sc1)
        o, buf_new, h_new = rglru_mixer(h, w_in_odd[i], w_out_odd[i], conv_w[i], conv_b[i], w_gate_a[i],
                                        b_gate_a[i], w_gate_x[i], b_gate_x[i], lru_lambda[i], conv_buf, h0)
        x = x + g1 * o
        h = modulate(x, norm_ffn_odd[i], sh2, sc2)
        x = x + g2 * moe_swiglu(h, router[i], moe_w1[i], moe_w3[i], moe_w2[i])
        return x, buf_new, h_new

    xp, xs = x_prompt, x_sample
    kp_l, vp_l, rp_l, cp_l, lp_l = [], [], [], [], []
    ks_l, vs_l, rs_l, cs_l, ls_l = [], [], [], [], []
    for l in range(DEPTH):
        i = l // 2
        if l % 2 == 0:
            lam_init = 0.8 - 0.6 * math.exp(-0.3 * l)
            s0_p = jnp.zeros((B_p, H_B, DK_B, DV_B), jnp.float32)
            xp, k_new, v_new, s_new = even_layer(xp, c_prompt, pos_p, None, s0_p, i, lam_init)
            kp_l.append(k_new); vp_l.append(v_new); rp_l.append(s_new)
            k_past = cache_k[i, page_table].reshape(B_s, -1, H_A, 2, DK_A)
            v_past = cache_v[i, page_table].reshape(B_s, -1, H_A, DV_A)
            xs, k_new, v_new, s_new = even_layer(xs, c_sample, pos_s, (k_past, v_past), state_ret[i], i, lam_init)
            ks_l.append(k_new); vs_l.append(v_new); rs_l.append(s_new)
        else:
            buf0 = jnp.zeros((B_p, CONV_W - 1, LRU_WIDTH), xp.dtype)
            h0 = jnp.zeros((B_p, LRU_WIDTH), jnp.float32)
            xp, buf_new, h_new = odd_layer(xp, c_prompt, buf0, h0, i)
            cp_l.append(buf_new); lp_l.append(h_new)
            xs, buf_new, h_new = odd_layer(xs, c_sample, state_conv[i], state_lru[i], i)
            cs_l.append(buf_new); ls_l.append(h_new)

    y_prompt = rmsnorm(xp, norm_final).astype(x_prompt.dtype)
    y_sample = rmsnorm(xs, norm_final).astype(x_sample.dtype)
    return (y_prompt, y_sample,
            jnp.stack(kp_l), jnp.stack(vp_l), jnp.stack(rp_l), jnp.stack(cp_l), jnp.stack(lp_l),
            jnp.stack(ks_l), jnp.stack(vs_l), jnp.stack(rs_l), jnp.stack(cs_l), jnp.stack(ls_l))
```

```python
import functools
import math

import jax
import jax.numpy as jnp
from jax import lax
from jax.experimental import pallas as pl
from jax.experimental.pallas import tpu as pltpu

F32 = jnp.float32
BF16 = jnp.bfloat16

EPS = 1e-6
ROPE_BASE = 10000.0
LRU_C = 8.0
TOP_K = 2
NEG_BIG = -1e30

V7X_LANES = 128
V7X_SUBLANES = 8
V7X_VMEM_LIMIT_BYTES = 56 * 1024 * 1024


def _cparams(n_axes):
    return pltpu.CompilerParams(dimension_semantics=("arbitrary",) * n_axes,
                                vmem_limit_bytes=V7X_VMEM_LIMIT_BYTES)


def _tile(n, pref, align):
    if n <= pref:
        return n
    t = (pref // align) * align
    while t > align and n % t:
        t -= align
    assert n % t == 0, (n, pref, align)
    return t


def _iota(shape, dim):
    return lax.broadcasted_iota(jnp.int32, shape, dim)


def _adaln_kernel(c_ref, w_ref, b_ref, o_ref):
    c = c_ref[...]
    s = (c * jax.nn.sigmoid(c)).astype(BF16)
    o_ref[...] = jnp.dot(s, w_ref[...].astype(BF16), preferred_element_type=F32) + b_ref[...]


def _adaln(c_all, w, b, li):
    R, D = c_all.shape
    N = w.shape[-1]
    tn = _tile(N, 1024, V7X_LANES)
    return pl.pallas_call(
        _adaln_kernel,
        grid=(N // tn,),
        in_specs=[pl.BlockSpec((R, D), lambda j: (0, 0)),
                  pl.BlockSpec((None, D, tn), lambda j: (li, 0, j)),
                  pl.BlockSpec((None, 1, tn), lambda j: (li, 0, j))],
        out_specs=pl.BlockSpec((R, tn), lambda j: (0, j)),
        out_shape=jax.ShapeDtypeStruct((R, N), F32),
        compiler_params=_cparams(1),
        name="adaln",
    )(c_all, w, b)


def _modulated(x_ref, g_ref, sh_ref, sc_ref):
    x = x_ref[...]
    ms = jnp.mean(x * x, axis=-1, keepdims=True)
    return x * lax.rsqrt(ms + EPS) * g_ref[...] * (1.0 + sc_ref[...]) + sh_ref[...]


def _modulate_kernel(x_ref, g_ref, sh_ref, sc_ref, o_ref):
    o_ref[...] = _modulated(x_ref, g_ref, sh_ref, sc_ref).astype(o_ref.dtype)


def _modulate_route_kernel(x_ref, g_ref, sh_ref, sc_ref, wr_ref, o_ref, r_ref, *, n_experts):
    h = _modulated(x_ref, g_ref, sh_ref, sc_ref)
    o_ref[...] = h.astype(o_ref.dtype)
    logits = jnp.dot(h, wr_ref[...], preferred_element_type=F32, precision=lax.Precision.HIGHEST)
    lane = _iota(logits.shape, 1)
    logits = jnp.where(lane < n_experts, logits, NEG_BIG)
    m1 = jnp.max(logits, axis=-1, keepdims=True)
    i1 = jnp.min(jnp.where(logits == m1, lane, V7X_LANES), axis=-1, keepdims=True)
    rest = jnp.where(lane == i1, NEG_BIG, logits)
    m2 = jnp.max(rest, axis=-1, keepdims=True)
    i2 = jnp.min(jnp.where(rest == m2, lane, V7X_LANES), axis=-1, keepdims=True)
    e2 = jnp.exp(m2 - m1)
    g1 = 1.0 / (1.0 + e2)
    g2 = e2 / (1.0 + e2)
    r_ref[...] = jnp.where(lane == 0, i1.astype(F32),
                           jnp.where(lane == 1, i2.astype(F32),
                                     jnp.where(lane == 2, g1, jnp.where(lane == 3, g2, 0.0))))


def _mod_specs(B, L, D, R, tm):
    xmap = lambda b, i: (b, i, 0)
    return [pl.BlockSpec((None, tm, D), xmap),
            pl.BlockSpec((1, D), lambda b, i: (0, 0)),
            pl.BlockSpec((None, R, D), lambda b, i: (b, 0, 0)),
            pl.BlockSpec((None, R, D), lambda b, i: (b, 0, 0))]


def _modulate(x, g, shift, scale):
    B, L, D = x.shape
    R = shift.shape[1]
    tm = L if R == L else _tile(L, 512, V7X_SUBLANES)
    return pl.pallas_call(
        _modulate_kernel,
        grid=(B, L // tm),
        in_specs=_mod_specs(B, L, D, R, tm),
        out_specs=pl.BlockSpec((None, tm, D), lambda b, i: (b, i, 0)),
        out_shape=jax.ShapeDtypeStruct((B, L, D), BF16),
        compiler_params=_cparams(2),
        name="modulate",
    )(x, g, shift, scale)


def _modulate_route(x, g, shift, scale, w_router_padded, n_experts):
    B, L, D = x.shape
    R = shift.shape[1]
    tm = L if R == L else _tile(L, 512, V7X_SUBLANES)
    return pl.pallas_call(
        functools.partial(_modulate_route_kernel, n_experts=n_experts),
        grid=(B, L // tm),
        in_specs=_mod_specs(B, L, D, R, tm) + [pl.BlockSpec((D, V7X_LANES), lambda b, i: (0, 0))],
        out_specs=[pl.BlockSpec((None, tm, D), lambda b, i: (b, i, 0)),
                   pl.BlockSpec((None, tm, V7X_LANES), lambda b, i: (b, i, 0))],
        out_shape=[jax.ShapeDtypeStruct((B, L, D), BF16),
                   jax.ShapeDtypeStruct((B, L, V7X_LANES), F32)],
        compiler_params=_cparams(2),
        name="modulate_route",
    )(x, g, shift, scale, w_router_padded)


def _cast_weights_once(first, w_refs, wb_refs):
    @pl.when(first)
    def _():
        for w_ref, wb_ref in zip(w_refs, wb_refs):
            wb_ref[...] = w_ref[...].astype(BF16)


def _mm_plain_kernel(a_ref, w_ref, o_ref, wb_ref):
    first = jnp.logical_and(pl.program_id(1) == 0, pl.program_id(2) == 0)
    _cast_weights_once(first, [w_ref], [wb_ref])
    o_ref[...] = jnp.dot(a_ref[...], wb_ref[...], preferred_element_type=F32).astype(o_ref.dtype)


def _mm_swiglu_kernel(a_ref, w1_ref, w3_ref, o_ref, w1b_ref, w3b_ref):
    first = jnp.logical_and(pl.program_id(1) == 0, pl.program_id(2) == 0)
    _cast_weights_once(first, [w1_ref, w3_ref], [w1b_ref, w3b_ref])
    a = a_ref[...]
    u = jnp.dot(a, w1b_ref[...], preferred_element_type=F32)
    v = jnp.dot(a, w3b_ref[...], preferred_element_type=F32)
    o_ref[...] = (u * jax.nn.sigmoid(u) * v).astype(o_ref.dtype)


def _mm_residual_kernel(a_ref, w_ref, x_ref, g_ref, o_ref, wb_ref):
    first = jnp.logical_and(pl.program_id(1) == 0, pl.program_id(2) == 0)
    _cast_weights_once(first, [w_ref], [wb_ref])
    y = jnp.dot(a_ref[...], wb_ref[...], preferred_element_type=F32)
    o_ref[...] = x_ref[...] + g_ref[...] * y


def _mm_tiles(L, K, N, n_w):
    budget = 28 * 1024 * 1024
    tn = _tile(N, 512, V7X_LANES)
    while tn > V7X_LANES and n_w * K * tn * (2 * 4 + 2) > budget:
        tn = _tile(N, tn // 2, V7X_LANES)
    tm = _tile(L, 1024 if K <= 2048 else 512, V7X_SUBLANES)
    return tm, tn


def _mm_plain(a, w, li, out_dtype):
    B, L, K = a.shape
    N = w.shape[-1]
    tm, tn = _mm_tiles(L, K, N, 1)
    return pl.pallas_call(
        _mm_plain_kernel,
        grid=(N // tn, B, L // tm),
        in_specs=[pl.BlockSpec((None, tm, K), lambda j, b, i: (b, i, 0)),
                  pl.BlockSpec((None, K, tn), lambda j, b, i: (li, 0, j))],
        out_specs=pl.BlockSpec((None, tm, tn), lambda j, b, i: (b, i, j)),
        out_shape=jax.ShapeDtypeStruct((B, L, N), out_dtype),
        scratch_shapes=[pltpu.VMEM((K, tn), BF16)],
        compiler_params=_cparams(3),
        name="mm_plain",
    )(a, w)


def _mm_swiglu(a, w1, w3, li):
    B, L, K = a.shape
    N = w1.shape[-1]
    tm, tn = _mm_tiles(L, K, N, 2)
    wspec = pl.BlockSpec((None, K, tn), lambda j, b, i: (li, 0, j))
    return pl.pallas_call(
        _mm_swiglu_kernel,
        grid=(N // tn, B, L // tm),
        in_specs=[pl.BlockSpec((None, tm, K), lambda j, b, i: (b, i, 0)), wspec, wspec],
        out_specs=pl.BlockSpec((None, tm, tn), lambda j, b, i: (b, i, j)),
        out_shape=jax.ShapeDtypeStruct((B, L, N), BF16),
        scratch_shapes=[pltpu.VMEM((K, tn), BF16), pltpu.VMEM((K, tn), BF16)],
        compiler_params=_cparams(3),
        name="mm_swiglu",
    )(a, w1, w3)


def _mm_residual(a, w, li, x, gate):
    B, L, K = a.shape
    N = w.shape[-1]
    R = gate.shape[1]
    tm, tn = _mm_tiles(L, K, N, 1)
    if R == L:
        tm = L
    return pl.pallas_call(
        _mm_residual_kernel,
        grid=(N // tn, B, L // tm),
        in_specs=[pl.BlockSpec((None, tm, K), lambda j, b, i: (b, i, 0)),
                  pl.BlockSpec((None, K, tn), lambda j, b, i: (li, 0, j)),
                  pl.BlockSpec((None, tm, tn), lambda j, b, i: (b, i, j)),
                  pl.BlockSpec((None, R, tn), lambda j, b, i: (b, 0, j))],
        out_specs=pl.BlockSpec((None, tm, tn), lambda j, b, i: (b, i, j)),
        out_shape=jax.ShapeDtypeStruct((B, L, N), F32),
        scratch_shapes=[pltpu.VMEM((K, tn), BF16)],
        compiler_params=_cparams(3),
        name="mm_residual",
    )(a, w, x, gate)


def _diff_lambda(lq_ref, lam_init):
    lq = lq_ref[...]
    a = jnp.sum(lq[0:1] * lq[1:2], axis=-1, keepdims=True)
    b = jnp.sum(lq[2:3] * lq[3:4], axis=-1, keepdims=True)
    return jnp.exp(a) - jnp.exp(b) + lam_init


def _attn_prompt_kernel(q_ref, k_ref, v_ref, lq_ref, sub_ref, o_ref, *, tq, dk, lam_init):
    qi = pl.program_id(2)
    sc = dk ** -0.5
    q = q_ref[...] * sc
    lane = _iota(q.shape, 1)
    qq = jnp.concatenate([jnp.where(lane < dk, q, 0.0), jnp.where(lane >= dk, q, 0.0)], axis=0).astype(BF16)

    def block(j, carry, masked):
        m, l, acc = carry
        start = pl.multiple_of(j * tq, tq)
        kb = k_ref[pl.ds(start, tq), :].astype(BF16)
        vb = v_ref[pl.ds(start, tq), :].astype(BF16)
        s = lax.dot_general(qq, kb, (((1,), (1,)), ((), ())), preferred_element_type=F32)
        if masked:
            row = _iota(s.shape, 0)
            row = jnp.where(row >= tq, row - tq, row)
            s = jnp.where(_iota(s.shape, 1) <= row, s, NEG_BIG)
        m_new = jnp.maximum(m, jnp.max(s, axis=-1, keepdims=True))
        alpha = jnp.exp(m - m_new)
        p = jnp.exp(s - m_new)
        l = alpha * l + jnp.sum(p, axis=-1, keepdims=True)
        acc = alpha * acc + jnp.dot(p.astype(BF16), vb, preferred_element_type=F32)
        return m_new, l, acc

    init = (jnp.full((2 * tq, 1), NEG_BIG, F32), jnp.zeros((2 * tq, 1), F32),
            jnp.zeros((2 * tq, 2 * dk), F32))
    carry = lax.fori_loop(0, qi, lambda j, c: block(j, c, False), init)
    m, l, acc = block(qi, carry, True)
    on = acc / l
    lam = _diff_lambda(lq_ref, lam_init)
    o = on[:tq] - lam * on[tq:]
    ms = jnp.mean(o * o, axis=-1, keepdims=True)
    o_ref[...] = (o * lax.rsqrt(ms + EPS) * sub_ref[...] * (1.0 - lam_init)).astype(o_ref.dtype)


def _attn_prompt(z, lambda_qk, subln, li, n_heads, d_model, lam_init):
    B, L, _ = z.shape
    hw = 2 * lambda_qk.shape[-1]
    tq = _tile(L, 256, V7X_SUBLANES)
    H = n_heads
    return pl.pallas_call(
        functools.partial(_attn_prompt_kernel, tq=tq, dk=hw // 2, lam_init=lam_init),
        grid=(B, H, L // tq),
        in_specs=[pl.BlockSpec((None, tq, hw), lambda b, h, i: (b, i, h)),
                  pl.BlockSpec((None, L, hw), lambda b, h, i: (b, 0, H + h)),
                  pl.BlockSpec((None, L, hw), lambda b, h, i: (b, 0, 2 * H + h)),
                  pl.BlockSpec((None,) + lambda_qk.shape[1:], lambda b, h, i: (li, 0, 0)),
                  pl.BlockSpec((None, 1, hw), lambda b, h, i: (li, 0, h))],
        out_specs=pl.BlockSpec((None, tq, hw), lambda b, h, i: (b, i, h)),
        out_shape=jax.ShapeDtypeStruct((B, L, d_model), BF16),
        compiler_params=_cparams(3),
        name="attn_prompt",
    )(z, z, z, lambda_qk, subln)


def _attn_sample_kernel(pt_ref, q_ref, kn_ref, vn_ref, lq_ref, sub_ref, *rest, n_pages, n_heads, dk, lam_init):
    k_refs = rest[:n_pages]
    v_refs = rest[n_pages:2 * n_pages]
    o_ref, m_ref, l_ref, acc_ref = rest[2 * n_pages:]
    s_id = pl.program_id(1)
    H = n_heads
    hw = 2 * dk
    W = H * hw
    sc = dk ** -0.5

    @pl.when(s_id == 0)
    def _():
        m_ref[...] = jnp.full(m_ref.shape, NEG_BIG, F32)
        l_ref[...] = jnp.zeros(l_ref.shape, F32)
        acc_ref[...] = jnp.zeros(acc_ref.shape, F32)

    row = _iota((2 * H, W), 0)
    lane = _iota((2 * H, W), 1)
    r_head = jnp.where(row >= H, row - H, row)
    r_comp = jnp.where(row >= H, 1, 0)
    l_head = lane // hw
    l_comp = (lane // dk) % 2
    qsel = jnp.logical_and(l_head == r_head, l_comp == r_comp)
    qf = jnp.where(qsel, q_ref[...] * sc, 0.0)
    qb = qf.astype(BF16)

    s_parts = [lax.dot_general(qb, k_refs[r][...].astype(BF16), (((1,), (1,)), ((), ())),
                               preferred_element_type=F32) for r in range(n_pages)]
    s = jnp.concatenate(s_parts, axis=-1) if n_pages > 1 else s_parts[0]
    m = m_ref[...]
    m_new = jnp.maximum(m, jnp.max(s, axis=-1, keepdims=True))
    alpha = jnp.exp(m - m_new)
    p = jnp.exp(s - m_new)
    l_ref[...] = alpha * l_ref[...] + jnp.sum(p, axis=-1, keepdims=True)
    pb = p.astype(BF16)
    page = s_parts[0].shape[-1]
    pv = jnp.dot(pb[:, :page], v_refs[0][...].astype(BF16), preferred_element_type=F32)
    for r in range(1, n_pages):
        pv = pv + jnp.dot(pb[:, r * page:(r + 1) * page], v_refs[r][...].astype(BF16),
                          preferred_element_type=F32)
    acc_ref[...] = alpha * acc_ref[...] + pv
    m_ref[...] = m_new

    @pl.when(s_id == pl.num_programs(1) - 1)
    def _():
        s_new = jnp.sum(qf * kn_ref[...], axis=-1, keepdims=True)
        m_old = m_ref[...]
        m_fin = jnp.maximum(m_old, s_new)
        a_fin = jnp.exp(m_old - m_fin)
        p_new = jnp.exp(s_new - m_fin)
        l_fin = a_fin * l_ref[...] + p_new
        acc = a_fin * acc_ref[...] + p_new * vn_ref[...]
        on = acc / l_fin
        lam = _diff_lambda(lq_ref, lam_init)
        o = on[:H] - lam * on[H:]
        own = (_iota((H, W), 1) // hw) == _iota((H, W), 0)
        om = jnp.where(own, o, 0.0)
        ms = jnp.sum(om * om, axis=-1, keepdims=True) / hw
        onrm = om * lax.rsqrt(ms + EPS)
        o_ref[...] = (jnp.sum(onrm, axis=0, keepdims=True) * sub_ref[...] * (1.0 - lam_init)).astype(o_ref.dtype)


def _attn_sample(z, cache_k, cache_v, page_table, lambda_qk, subln, li, n_heads, lam_init):
    Bs = z.shape[0]
    n_phys_total, page, W = cache_k.shape
    n_pg = page_table.shape[1]
    P = 4 if n_pg % 4 == 0 else (2 if n_pg % 2 == 0 else 1)
    n_phys = n_phys_total // lambda_qk.shape[0]
    base = li * n_phys
    H = n_heads
    dk = lambda_qk.shape[-1]

    def page_spec(r):
        return pl.BlockSpec((None, page, W), lambda b, s, pt: (base + pt[b, s * P + r], 0, 0))

    zrow = lambda c: pl.BlockSpec((None, 1, W), lambda b, s, pt: (b, 0, c))
    grid_spec = pltpu.PrefetchScalarGridSpec(
        num_scalar_prefetch=1,
        grid=(Bs, n_pg // P),
        in_specs=[zrow(0), zrow(1), zrow(2),
                  pl.BlockSpec((None,) + lambda_qk.shape[1:], lambda b, s, pt: (li, 0, 0)),
                  pl.BlockSpec((None, 1, W), lambda b, s, pt: (li, 0, 0))]
                 + [page_spec(r) for r in range(P)] + [page_spec(r) for r in range(P)],
        out_specs=pl.BlockSpec((None, 1, W), lambda b, s, pt: (b, 0, 0)),
        scratch_shapes=[pltpu.VMEM((2 * H, 1), F32), pltpu.VMEM((2 * H, 1), F32),
                        pltpu.VMEM((2 * H, W), F32)],
    )
    return pl.pallas_call(
        functools.partial(_attn_sample_kernel, n_pages=P, n_heads=H, dk=dk, lam_init=lam_init),
        grid_spec=grid_spec,
        out_shape=jax.ShapeDtypeStruct((Bs, 1, W), BF16),
        compiler_params=_cparams(2),
        name="attn_sample",
    )(page_table, z, z, z, lambda_qk, subln, *([cache_k] * P), *([cache_v] * P))


def _rope(x, cos, sin):
    half = x.shape[-1] // 2
    x1, x2 = x[:, :half], x[:, half:]
    return jnp.concatenate([x1 * cos - x2 * sin, x1 * sin + x2 * cos], axis=-1)


def _groupnorm_gate(o, gn, g):
    mu = jnp.mean(o, axis=-1, keepdims=True)
    d = o - mu
    var = jnp.mean(d * d, axis=-1, keepdims=True)
    return d * lax.rsqrt(var + EPS) * gn * (g * jax.nn.sigmoid(g))


def _ret_prompt_kernel(o_in_ref, q_ref, k_ref, v_ref, g_ref, cos_ref, sin_ref, lg_ref, gn_ref,
                       o_ref, sfin_ref, s_ref, *, chunk, dk):
    del o_in_ref
    c = pl.program_id(2)

    @pl.when(c == 0)
    def _():
        s_ref[...] = jnp.zeros(s_ref.shape, F32)

    cos, sin = cos_ref[...], sin_ref[...]
    qr = _rope(q_ref[...], cos, sin) * dk ** -0.5
    kr = _rope(k_ref[...], cos, sin)
    vb = v_ref[...].astype(BF16)
    lg = lg_ref[...][:, :1]
    C = chunk
    d = _iota((C, C), 0) - _iota((C, C), 1)
    decay = jnp.where(d >= 0, jnp.exp(jnp.maximum(d, 0).astype(F32) * lg), 0.0)
    qrb = qr.astype(BF16)
    scores = lax.dot_general(qrb, kr.astype(BF16), (((1,), (1,)), ((), ())), preferred_element_type=F32) * decay
    intra = jnp.dot(scores.astype(BF16), vb, preferred_element_type=F32)
    t = _iota((C, 1), 0).astype(F32)
    s_old = s_ref[...]
    inter = jnp.dot(qrb, s_old.astype(BF16), preferred_element_type=F32) * jnp.exp((t + 1.0) * lg)
    kd = kr * jnp.exp((C - 1.0 - t) * lg)
    s_new = jnp.exp(C * lg) * s_old + jnp.dot(kd.T.astype(BF16), vb, preferred_element_type=F32)
    s_ref[...] = s_new
    sfin_ref[...] = s_new
    o_ref[...] = _groupnorm_gate(intra + inter, gn_ref[...], g_ref[...]).astype(o_ref.dtype)


def _ret_prompt(o_buf, z, cos_tab, sin_tab, log_gamma, gn_b, li, n_heads_a, n_heads_b):
    B, L, _ = z.shape
    dk = 2 * cos_tab.shape[-1]
    HB = n_heads_b
    off = 3 * n_heads_a * V7X_LANES // dk
    o_off = n_heads_a * V7X_LANES // dk
    C = _tile(L, 256, V7X_SUBLANES)
    zb = lambda g: pl.BlockSpec((None, C, dk), lambda b, h, c: (b, c, off + g * HB + h))
    return pl.pallas_call(
        functools.partial(_ret_prompt_kernel, chunk=C, dk=dk),
        grid=(B, HB, L // C),
        in_specs=[pl.BlockSpec(memory_space=pl.ANY), zb(0), zb(1), zb(2), zb(3),
                  pl.BlockSpec((C, dk // 2), lambda b, h, c: (c, 0)),
                  pl.BlockSpec((C, dk // 2), lambda b, h, c: (c, 0)),
                  pl.BlockSpec((None, 1, V7X_LANES), lambda b, h, c: (h, 0, 0)),
                  pl.BlockSpec((None, 1, dk), lambda b, h, c: (li, 0, h))],
        out_specs=[pl.BlockSpec((None, C, dk), lambda b, h, c: (b, c, o_off + h)),
                   pl.BlockSpec((None, None, dk, dk), lambda b, h, c: (b, h, 0, 0))],
        out_shape=[jax.ShapeDtypeStruct(o_buf.shape, o_buf.dtype),
                   jax.ShapeDtypeStruct((B, HB, dk, dk), F32)],
        scratch_shapes=[pltpu.VMEM((dk, dk), F32)],
        input_output_aliases={0: 0},
        compiler_params=_cparams(3),
        name="ret_prompt",
    )(o_buf, z, z, z, z, cos_tab, sin_tab, log_gamma, gn_b)


def _ret_sample_kernel(o_in_ref, q_ref, k_ref, v_ref, g_ref, cos_ref, sin_ref, lg_ref, gn_ref, s0_ref,
                       o_ref, snew_ref, *, dk):
    del o_in_ref
    cos, sin = cos_ref[...], sin_ref[...]
    qr = _rope(q_ref[...], cos, sin) * dk ** -0.5
    kr = _rope(k_ref[...], cos, sin)
    v = v_ref[...]
    gamma = jnp.exp(lg_ref[...][:, :1])
    eye = _iota((dk, dk), 0) == _iota((dk, dk), 1)
    kcol = jnp.sum(jnp.where(eye, kr, 0.0), axis=-1, keepdims=True)
    qcol = jnp.sum(jnp.where(eye, qr, 0.0), axis=-1, keepdims=True)
    s0 = s0_ref[...]
    snew_ref[...] = gamma * s0 + kcol * v
    inter = gamma * jnp.sum(qcol * s0, axis=0, keepdims=True)
    intra = jnp.sum(qr * kr, axis=-1, keepdims=True) * v
    o_ref[...] = _groupnorm_gate(intra + inter, gn_ref[...], g_ref[...]).astype(o_ref.dtype)


def _ret_sample(o_buf, z, cos_row, sin_row, log_gamma, gn_b, state, li, n_heads_a, n_heads_b):
    Bs = z.shape[0]
    dk = 2 * cos_row.shape[-1]
    HB = n_heads_b
    off = 3 * n_heads_a * V7X_LANES // dk
    o_off = n_heads_a * V7X_LANES // dk
    zb = lambda g: pl.BlockSpec((None, 1, dk), lambda b, h: (b, 0, off + g * HB + h))
    return pl.pallas_call(
        functools.partial(_ret_sample_kernel, dk=dk),
        grid=(Bs, HB),
        in_specs=[pl.BlockSpec(memory_space=pl.ANY), zb(0), zb(1), zb(2), zb(3),
                  pl.BlockSpec((1, dk // 2), lambda b, h: (0, 0)),
                  pl.BlockSpec((1, dk // 2), lambda b, h: (0, 0)),
                  pl.BlockSpec((None, 1, V7X_LANES), lambda b, h: (h, 0, 0)),
                  pl.BlockSpec((None, 1, dk), lambda b, h: (li, 0, h)),
                  pl.BlockSpec((None, None, dk, dk), lambda b, h: (li * Bs + b, h, 0, 0))],
        out_specs=[pl.BlockSpec((None, 1, dk), lambda b, h: (b, 0, o_off + h)),
                   pl.BlockSpec((None, None, dk, dk), lambda b, h: (b, h, 0, 0))],
        out_shape=[jax.ShapeDtypeStruct(o_buf.shape, o_buf.dtype),
                   jax.ShapeDtypeStruct((Bs, HB, dk, dk), F32)],
        input_output_aliases={0: 0},
        compiler_params=_cparams(2),
        name="ret_sample",
    )(o_buf, z, z, z, z, cos_row, sin_row, log_gamma, gn_b, state)


def _softplus(z):
    return jnp.maximum(z, 0.0) + jnp.log1p(jnp.exp(-jnp.abs(z)))


def _lru_coeffs(xc, wa_ref, wx_ref, ba_ref, bx_ref, lam_ref):
    xcb = xc.astype(BF16)
    r = jax.nn.sigmoid(jnp.dot(xcb, wa_ref[...].astype(BF16), preferred_element_type=F32) + ba_ref[...])
    gi = jax.nn.sigmoid(jnp.dot(xcb, wx_ref[...].astype(BF16), preferred_element_type=F32) + bx_ref[...])
    log_a = -LRU_C * r * _softplus(-lam_ref[...])
    a = jnp.exp(log_a)
    t = jnp.tanh(log_a)
    mult = jnp.sqrt(jnp.maximum(-2.0 * t / (1.0 - t), 0.0))
    return a, mult * gi * xc


def _lru_prompt_kernel(gate_ref, x_ref, cw_ref, cb_ref, wa_ref, wx_ref, ba_ref, bx_ref, lam_ref,
                       tail0_ref, h0_ref, y_ref, hl_ref, tail_ref, hc_ref, *, tt, conv_w):
    t_id = pl.program_id(2)

    @pl.when(t_id == 0)
    def _():
        tail_ref[...] = tail0_ref[...]
        hc_ref[...] = h0_ref[...]

    x = x_ref[...]
    tl = tail_ref[...]
    row8 = _iota(tl.shape, 0)
    xc = cb_ref[...] + cw_ref[conv_w - 1:conv_w, :] * x
    for s in range(1, conv_w):
        xs = pltpu.roll(x, s, 0)
        head = jnp.where(row8 >= s, xs[:V7X_SUBLANES], pltpu.roll(tl, s, 0))
        xs = jnp.concatenate([head, xs[V7X_SUBLANES:]], axis=0) if tt > V7X_SUBLANES else head
        xc = xc + cw_ref[conv_w - 1 - s:conv_w - s, :] * xs
    tail_ref[...] = x[tt - V7X_SUBLANES:, :]

    a, b = _lru_coeffs(xc, wa_ref, wx_ref, ba_ref, bx_ref, lam_ref)
    row = _iota(a.shape, 0)
    s = 1
    while s < tt:
        keep = row >= s
        a_s = jnp.where(keep, pltpu.roll(a, s, 0), 1.0)
        b_s = jnp.where(keep, pltpu.roll(b, s, 0), 0.0)
        b = b + a * b_s
        a = a * a_s
        s *= 2
    h = b + a * hc_ref[...]
    hc_ref[...] = h[tt - 1:tt, :]
    hl_ref[...] = h[tt - 1:tt, :]
    y_ref[...] = (jax.nn.gelu(gate_ref[...], approximate=True) * h).astype(y_ref.dtype)


def _lru_prompt(z, conv_w, conv_b, w_ga, w_gx, b_ga, b_gx, lam, tail0, h0, li):
    B, L, W2 = z.shape
    W = W2 // 2
    bw = w_ga.shape[-1]
    NB = W // bw
    CW = conv_w.shape[1]
    tt = _tile(L, 256, V7X_SUBLANES)
    vec = lambda: pl.BlockSpec((None, 1, bw), lambda b, n, t: (li, 0, n))
    gw = lambda: pl.BlockSpec((None, bw, bw), lambda b, n, t: (li * NB + n, 0, 0))
    return pl.pallas_call(
        functools.partial(_lru_prompt_kernel, tt=tt, conv_w=CW),
        grid=(B, NB, L // tt),
        in_specs=[pl.BlockSpec((None, tt, bw), lambda b, n, t: (b, t, n)),
                  pl.BlockSpec((None, tt, bw), lambda b, n, t: (b, t, NB + n)),
                  pl.BlockSpec((None, CW, bw), lambda b, n, t: (li, 0, n)),
                  vec(), gw(), gw(), vec(), vec(), vec(),
                  pl.BlockSpec((None, V7X_SUBLANES, bw), lambda b, n, t: (b, 0, n)),
                  pl.BlockSpec((None, 1, bw), lambda b, n, t: (b, 0, n))],
        out_specs=[pl.BlockSpec((None, tt, bw), lambda b, n, t: (b, t, n)),
                   pl.BlockSpec((None, 1, bw), lambda b, n, t: (b, 0, n))],
        out_shape=[jax.ShapeDtypeStruct((B, L, W), BF16), jax.ShapeDtypeStruct((B, 1, W), F32)],
        scratch_shapes=[pltpu.VMEM((V7X_SUBLANES, bw), F32), pltpu.VMEM((1, bw), F32)],
        compiler_params=_cparams(3),
        name="lru_prompt",
    )(z, z, conv_w, conv_b, w_ga, w_gx, b_ga, b_gx, lam, tail0, h0)


def _lru_sample_kernel(gate_ref, x_ref, cw_ref, cb_ref, wa_ref, wx_ref, ba_ref, bx_ref, lam_ref,
                       buf_ref, h0_ref, y_ref, h_ref, *, conv_w):
    x = x_ref[...]
    xc = cb_ref[...] + cw_ref[conv_w - 1:conv_w, :] * x
    for j in range(conv_w - 1):
        xc = xc + cw_ref[j:j + 1, :] * buf_ref[j]
    a, b = _lru_coeffs(xc, wa_ref, wx_ref, ba_ref, bx_ref, lam_ref)
    h = a * h0_ref[...] + b
    h_ref[...] = h
    y_ref[...] = (jax.nn.gelu(gate_ref[...], approximate=True) * h).astype(y_ref.dtype)


def _lru_sample(z, conv_w, conv_b, w_ga, w_gx, b_ga, b_gx, lam, buf, h0, li):
    Bs, W2 = z.shape
    W = W2 // 2
    bw = w_ga.shape[-1]
    NB = W // bw
    CW = conv_w.shape[1]
    vec = lambda: pl.BlockSpec((None, 1, bw), lambda n: (li, 0, n))
    gw = lambda: pl.BlockSpec((None, bw, bw), lambda n: (li * NB + n, 0, 0))
    return pl.pallas_call(
        functools.partial(_lru_sample_kernel, conv_w=CW),
        grid=(NB,),
        in_specs=[pl.BlockSpec((Bs, bw), lambda n: (0, n)),
                  pl.BlockSpec((Bs, bw), lambda n: (0, NB + n)),
                  pl.BlockSpec((None, CW, bw), lambda n: (li, 0, n)),
                  vec(), gw(), gw(), vec(), vec(), vec(),
                  pl.BlockSpec((CW - 1, Bs, bw), lambda n: (0, 0, n)),
                  pl.BlockSpec((Bs, bw), lambda n: (0, n))],
        out_specs=[pl.BlockSpec((Bs, bw), lambda n: (0, n)),
                   pl.BlockSpec((Bs, bw), lambda n: (0, n))],
        out_shape=[jax.ShapeDtypeStruct((Bs, W), BF16), jax.ShapeDtypeStruct((Bs, W), F32)],
        compiler_params=_cparams(1),
        name="lru_sample",
    )(z, z, conv_w, conv_b, w_ga, w_gx, b_ga, b_gx, lam, buf, h0)


def _moe_up_kernel(be_ref, bf_ref, nu_ref, x_ref, w1_ref, w3_ref, o_ref, w1b_ref, w3b_ref):
    rb = pl.program_id(1)
    _cast_weights_once(bf_ref[rb] == 1, [w1_ref, w3_ref], [w1b_ref, w3b_ref])

    @pl.when(rb < nu_ref[0])
    def _():
        x = x_ref[...]
        u = jnp.dot(x, w1b_ref[...], preferred_element_type=F32)
        v = jnp.dot(x, w3b_ref[...], preferred_element_type=F32)
        o_ref[...] = (u * jax.nn.sigmoid(u) * v).astype(o_ref.dtype)

    @pl.when(rb >= nu_ref[0])
    def _():
        o_ref[...] = jnp.zeros(o_ref.shape, o_ref.dtype)


def _moe_down_kernel(be_ref, bf_ref, nu_ref, a_ref, w_ref, g_ref, o_ref, wb_ref):
    rb = pl.program_id(1)
    _cast_weights_once(bf_ref[rb] == 1, [w_ref], [wb_ref])

    @pl.when(rb < nu_ref[0])
    def _():
        o_ref[...] = g_ref[...] * jnp.dot(a_ref[...], wb_ref[...], preferred_element_type=F32)

    @pl.when(rb >= nu_ref[0])
    def _():
        o_ref[...] = jnp.zeros(o_ref.shape, o_ref.dtype)


MOE_ROW_BLOCK = 512


def _moe_up(xs, w1, w3, blk_e, blk_first, n_used, li, n_experts):
    n_rows, D = xs.shape
    F = w1.shape[-1]
    tm = MOE_ROW_BLOCK
    tn = _tile(F, 512, V7X_LANES)
    wspec = pl.BlockSpec((None, D, tn), lambda j, rb, be, bf, nu: (li * n_experts + be[rb], 0, j))
    grid_spec = pltpu.PrefetchScalarGridSpec(
        num_scalar_prefetch=3,
        grid=(F // tn, n_rows // tm),
        in_specs=[pl.BlockSpec((tm, D), lambda j, rb, be, bf, nu: (rb, 0)), wspec, wspec],
        out_specs=pl.BlockSpec((tm, tn), lambda j, rb, be, bf, nu: (rb, j)),
        scratch_shapes=[pltpu.VMEM((D, tn), BF16), pltpu.VMEM((D, tn), BF16)],
    )
    return pl.pallas_call(
        _moe_up_kernel,
        grid_spec=grid_spec,
        out_shape=jax.ShapeDtypeStruct((n_rows, F), BF16),
        compiler_params=_cparams(2),
        name="moe_up",
    )(blk_e, blk_first, n_used, xs, w1, w3)


def _moe_down(hh, w2, gate_rows, blk_e, blk_first, n_used, li, n_experts):
    n_rows, F = hh.shape
    D = w2.shape[-1]
    tm = MOE_ROW_BLOCK
    tn = _tile(D, 256, V7X_LANES)
    grid_spec = pltpu.PrefetchScalarGridSpec(
        num_scalar_prefetch=3,
        grid=(D // tn, n_rows // tm),
        in_specs=[pl.BlockSpec((tm, F), lambda j, rb, be, bf, nu: (rb, 0)),
                  pl.BlockSpec((None, F, tn), lambda j, rb, be, bf, nu: (li * n_experts + be[rb], 0, j)),
                  pl.BlockSpec((tm, 1), lambda j, rb, be, bf, nu: (rb, 0))],
        out_specs=pl.BlockSpec((tm, tn), lambda j, rb, be, bf, nu: (rb, j)),
        scratch_shapes=[pltpu.VMEM((F, tn), BF16)],
    )
    return pl.pallas_call(
        _moe_down_kernel,
        grid_spec=grid_spec,
        out_shape=jax.ShapeDtypeStruct((n_rows, D), F32),
        compiler_params=_cparams(2),
        name="moe_down",
    )(blk_e, blk_first, n_used, hh, w2, gate_rows)


def _residual_kernel(x_ref, y_ref, g_ref, o_ref):
    o_ref[...] = x_ref[...] + g_ref[...] * y_ref[...]


def _residual_norm_kernel(x_ref, y_ref, g_ref, nf_ref, o_ref):
    x = x_ref[...] + g_ref[...] * y_ref[...]
    ms = jnp.mean(x * x, axis=-1, keepdims=True)
    o_ref[...] = x * lax.rsqrt(ms + EPS) * nf_ref[...]


def _norm_kernel(x_ref, nf_ref, o_ref):
    x = x_ref[...]
    ms = jnp.mean(x * x, axis=-1, keepdims=True)
    o_ref[...] = x * lax.rsqrt(ms + EPS) * nf_ref[...]


def _residual(x, y, gate, norm_g):
    B, L, D = x.shape
    R = gate.shape[1]
    tm = L if R == L else _tile(L, 512, V7X_SUBLANES)
    blk = pl.BlockSpec((None, tm, D), lambda b, i: (b, i, 0))
    in_specs = [blk, blk, pl.BlockSpec((None, R, D), lambda b, i: (b, 0, 0))]
    args = [x, y, gate]
    kern = _residual_kernel
    if norm_g is not None:
        in_specs.append(pl.BlockSpec((1, D), lambda b, i: (0, 0)))
        args.append(norm_g)
        kern = _residual_norm_kernel
    return pl.pallas_call(
        kern, grid=(B, L // tm), in_specs=in_specs, out_specs=blk,
        out_shape=jax.ShapeDtypeStruct((B, L, D), F32),
        compiler_params=_cparams(2), name="residual",
    )(*args)


def _final_norm(x, norm_g):
    B, L, D = x.shape
    tm = _tile(L, 512, V7X_SUBLANES)
    blk = pl.BlockSpec((None, tm, D), lambda b, i: (b, i, 0))
    return pl.pallas_call(
        _norm_kernel, grid=(B, L // tm),
        in_specs=[blk, pl.BlockSpec((1, D), lambda b, i: (0, 0))], out_specs=blk,
        out_shape=jax.ShapeDtypeStruct((B, L, D), F32),
        compiler_params=_cparams(2), name="final_norm",
    )(x, norm_g)


def _split_mods(m, n_p, n_s):
    D = m.shape[-1] // 6
    mp = [m[:n_p, k * D:(k + 1) * D][:, None, :] for k in range(6)]
    ms = [m[n_p:n_p + n_s, k * D:(k + 1) * D][None, :, :] for k in range(6)]
    return mp, ms


def _stack_cond(c_prompt, c_sample):
    c_all = jnp.concatenate([c_prompt, c_sample], axis=0)
    pad = (-c_all.shape[0]) % V7X_SUBLANES
    if pad:
        c_all = jnp.concatenate([c_all, jnp.zeros((pad, c_all.shape[1]), c_all.dtype)], axis=0)
    return c_all


def _moe(h_tok, route, w1, w3, w2, li, n_experts):
    T, D = h_tok.shape
    E = n_experts
    tm = MOE_ROW_BLOCK
    e_flat = route[:, :TOP_K].astype(jnp.int32).reshape(-1)
    g_flat = route[:, TOP_K:2 * TOP_K].reshape(-1)
    n_slots = T * TOP_K
    onehot = (e_flat[:, None] == jnp.arange(E, dtype=jnp.int32)[None, :]).astype(jnp.int32)
    csum = jnp.cumsum(onehot, axis=0)
    counts = csum[-1]
    rank = jnp.sum((csum - onehot) * onehot, axis=-1)
    padded = (counts + tm - 1) // tm * tm
    pends = jnp.cumsum(padded)
    pstarts = pends - padded
    dest = pstarts[e_flat] + rank
    n_blocks = -(-n_slots // tm) + E
    n_rows = n_blocks * tm
    tok_flat = jnp.arange(n_slots, dtype=jnp.int32) // TOP_K
    src = jnp.zeros((n_rows,), jnp.int32).at[dest].set(tok_flat)
    gate_rows = jnp.zeros((n_rows,), F32).at[dest].set(g_flat)[:, None]
    blk_start = jnp.arange(n_blocks, dtype=jnp.int32) * tm
    blk_e = jnp.minimum(jnp.searchsorted(pends, blk_start, side='right'), E - 1).astype(jnp.int32)
    blk_first = jnp.concatenate([jnp.ones((1,), jnp.int32), (blk_e[1:] != blk_e[:-1]).astype(jnp.int32)])
    n_used = (pends[-1:] // tm).astype(jnp.int32)
    xs = jnp.take(h_tok, src, axis=0)
    hh = _moe_up(xs, w1, w3, blk_e, blk_first, n_used, li, E)
    ys = _moe_down(hh, w2, gate_rows, blk_e, blk_first, n_used, li, E)
    d2 = dest.reshape(T, TOP_K)
    y = jnp.take(ys, d2[:, 0], axis=0)
    for k in range(1, TOP_K):
        y = y + jnp.take(ys, d2[:, k], axis=0)
    return y


def kernel(x_prompt, x_sample, c_prompt, c_sample, cache_k, cache_v, state_ret, state_conv, state_lru, page_table, ada_w_even, ada_b_even, norm_mix_even, norm_ffn_even, w_in_even, lambda_qk, subln_a, gn_b, w_out_even, ffn_w1, ffn_w3, ffn_w2, ada_w_odd, ada_b_odd, norm_mix_odd, norm_ffn_odd, w_in_odd, conv_w, conv_b, w_gate_a, b_gate_a, w_gate_x, b_gate_x, lru_lambda, w_out_odd, router, moe_w1, moe_w3, moe_w2, norm_final):
    Bp, Lp, D = x_prompt.shape
    Bs, Ls, _ = x_sample.shape
    assert Ls == 1, "the sample group carries one new token per sequence"
    n_even, n_odd = w_in_even.shape[0], w_in_odd.shape[0]
    depth = n_even + n_odd
    _, n_phys, page, HA, _, dka = cache_k.shape
    HB, dkb = state_ret.shape[2], state_ret.shape[3]
    past_len = page_table.shape[1] * page
    E = router.shape[-1]
    CW = conv_w.shape[1]
    W = conv_w.shape[-1]

    half = dkb // 2
    freqs = ROPE_BASE ** (-jnp.arange(half, dtype=F32) / half)
    ang_p = jnp.arange(Lp).astype(F32)[:, None] * freqs[None, :]
    ang_s = (past_len + jnp.arange(Ls)).astype(F32)[:, None] * freqs[None, :]
    cos_p, sin_p, cos_s, sin_s = jnp.cos(ang_p), jnp.sin(ang_p), jnp.cos(ang_s), jnp.sin(ang_s)
    log_gamma = jnp.log1p(-jnp.exp2(-5.0 - jnp.arange(HB, dtype=F32)))
    log_gamma = jnp.broadcast_to(log_gamma[:, None, None], (HB, 1, V7X_LANES))

    c_all = _stack_cond(c_prompt, c_sample)
    row3 = lambda a: a.reshape(a.shape[0], 1, a.shape[-1])
    cache_k2 = cache_k.reshape(n_even * n_phys, page, HA * 2 * dka)
    cache_v2 = cache_v.reshape(n_even * n_phys, page, HA * 2 * dka)
    state_ret2 = state_ret.reshape(n_even * Bs, HB, dkb, dkb)
    w_ga2 = w_gate_a.reshape(-1, w_gate_a.shape[-2], w_gate_a.shape[-1])
    w_gx2 = w_gate_x.reshape(-1, w_gate_x.shape[-2], w_gate_x.shape[-1])
    moe_w1f = moe_w1.reshape(-1, moe_w1.shape[-2], moe_w1.shape[-1])
    moe_w3f = moe_w3.reshape(-1, moe_w3.shape[-2], moe_w3.shape[-1])
    moe_w2f = moe_w2.reshape(-1, moe_w2.shape[-2], moe_w2.shape[-1])
    router_pad = jnp.concatenate(
        [router, jnp.zeros(router.shape[:-1] + (V7X_LANES - E,), router.dtype)], axis=-1)
    nf = norm_final.reshape(1, D)

    xp = x_prompt
    xs = x_sample.reshape(1, Bs, D)
    kp_l, vp_l, rp_l, cp_l, lp_l = [], [], [], [], []
    ks_l, vs_l, rs_l, cs_l, ls_l = [], [], [], [], []
    normed = False
    for l in range(depth):
        i = l // 2
        last = l == depth - 1
        if l % 2 == 0:
            lam_init = 0.8 - 0.6 * math.exp(-0.3 * l)
            m = _adaln(c_all, ada_w_even, row3(ada_b_even), i)
            (sh1, sc1, g1, sh2, sc2, g2), (sh1s, sc1s, g1s, sh2s, sc2s, g2s) = _split_mods(m, Bp, Bs)
            gmix, gffn = norm_mix_even[i][None, :], norm_ffn_even[i][None, :]
            wa = HA * 2 * dka
            h = _modulate(xp, gmix, sh1, sc1)
            z = _mm_plain(h, w_in_even, i, F32)
            kp_l.append(z[:, :, wa:2 * wa].reshape(Bp, Lp, HA, 2, dka))
            vp_l.append(z[:, :, 2 * wa:3 * wa].reshape(Bp, Lp, HA, 2 * dka))
            o = _attn_prompt(z, lambda_qk, row3(subln_a), i, HA, D, lam_init)
            o, s_fin = _ret_prompt(o, z, cos_p, sin_p, log_gamma, row3(gn_b), i, HA, HB)
            rp_l.append(s_fin)
            xp = _mm_residual(o, w_out_even, i, xp, g1)
            h = _modulate(xp, gffn, sh2, sc2)
            hh = _mm_swiglu(h, ffn_w1, ffn_w3, i)
            xp = _mm_residual(hh, ffn_w2, i, xp, g2)
            h = _modulate(xs, gmix, sh1s, sc1s)
            zs = _mm_plain(h, w_in_even, i, F32)
            zs3 = zs.reshape(Bs, 1, zs.shape[-1])
            ks_l.append(zs3[:, :, wa:2 * wa].reshape(Bs, Ls, HA, 2, dka))
            vs_l.append(zs3[:, :, 2 * wa:3 * wa].reshape(Bs, Ls, HA, 2 * dka))
            oa = _attn_sample(zs3, cache_k2, cache_v2, page_table, lambda_qk, row3(subln_a), i, HA, lam_init)
            o_s = jnp.concatenate([oa, jnp.zeros((Bs, 1, D - oa.shape[-1]), oa.dtype)], axis=-1)
            o_s, s_new = _ret_sample(o_s, zs3, cos_s, sin_s, log_gamma, row3(gn_b), state_ret2, i, HA, HB)
            rs_l.append(s_new)
            xs = _mm_residual(o_s.reshape(1, Bs, D), w_out_even, i, xs, g1s)
            h = _modulate(xs, gffn, sh2s, sc2s)
            hh = _mm_swiglu(h, ffn_w1, ffn_w3, i)
            xs = _mm_residual(hh, ffn_w2, i, xs, g2s)
        else:
            m = _adaln(c_all, ada_w_odd, row3(ada_b_odd), i)
            (sh1, sc1, g1, sh2, sc2, g2), (sh1s, sc1s, g1s, sh2s, sc2s, g2s) = _split_mods(m, Bp, Bs)
            gmix, gffn = norm_mix_odd[i][None, :], norm_ffn_odd[i][None, :]
            lru_args = (conv_w, row3(conv_b), w_ga2, w_gx2, row3(b_gate_a), row3(b_gate_x), row3(lru_lambda))
            h = _modulate(xp, gmix, sh1, sc1)
            z = _mm_plain(h, w_in_odd, i, F32)
            tail0 = jnp.zeros((Bp, V7X_SUBLANES, W), F32)
            h0 = jnp.zeros((Bp, 1, W), F32)
            y, h_last = _lru_prompt(z, *lru_args, tail0, h0, i)
            cp_l.append(z[:, Lp - (CW - 1):, W:])
            lp_l.append(h_last.reshape(Bp, W))
            xp = _mm_residual(y, w_out_odd, i, xp, g1)
            h = _modulate(xs, gmix, sh1s, sc1s)
            zs = _mm_plain(h, w_in_odd, i, F32).reshape(Bs, 2 * W)
            buf = state_conv[i]
            ys, h_new = _lru_sample(zs, *lru_args, jnp.swapaxes(buf, 0, 1), state_lru[i], i)
            cs_l.append(jnp.concatenate([buf[:, 1:], zs[:, None, W:]], axis=1))
            ls_l.append(h_new)
            xs = _mm_residual(ys.reshape(1, Bs, W), w_out_odd, i, xs, g1s)
            hp, route_p = _modulate_route(xp, gffn, sh2, sc2, router_pad[i], E)
            hs, route_s = _modulate_route(xs, gffn, sh2s, sc2s, router_pad[i], E)
            Tp = Bp * Lp
            h_tok = jnp.concatenate([hp.reshape(Tp, D), hs.reshape(Bs, D)], axis=0)
            route = jnp.concatenate([route_p.reshape(Tp, V7X_LANES), route_s.reshape(Bs, V7X_LANES)], axis=0)
            y_tok = _moe(h_tok, route, moe_w1f, moe_w3f, moe_w2f, i, E)
            xp = _residual(xp, y_tok[:Tp].reshape(Bp, Lp, D), g2, nf if last else None)
            xs = _residual(xs, y_tok[Tp:].reshape(1, Bs, D), g2s, nf if last else None)
            normed = last
    if not normed:
        xp = _final_norm(xp, nf)
        xs = _final_norm(xs, nf)
    y_prompt = xp.astype(x_prompt.dtype)
    y_sample = xs.reshape(Bs, Ls, D).astype(x_sample.dtype)
    return (y_prompt, y_sample,
            jnp.stack(kp_l), jnp.stack(vp_l), jnp.stack(rp_l), jnp.stack(cp_l), jnp.stack(lp_l),
            jnp.stack(ks_l), jnp.stack(vs_l), jnp.stack(rs_l), jnp.stack(cs_l), jnp.stack(ls_l))
```

```python
import functools
import math

import jax
import jax.numpy as jnp
from jax import lax
from jax.experimental import pallas as pl
from jax.experimental.pallas import tpu as pltpu

F32 = jnp.float32
BF16 = jnp.bfloat16

EPS = 1e-6
ROPE_BASE = 10000.0
LRU_C = 8.0
TOP_K = 2
NEG_BIG = -1e30

V7X_LANES = 128
V7X_SUBLANES = 8
V7X_VMEM_LIMIT_BYTES = 56 * 1024 * 1024


def _cparams(n_axes):
    return pltpu.CompilerParams(dimension_semantics=("arbitrary",) * n_axes,
                                vmem_limit_bytes=V7X_VMEM_LIMIT_BYTES)


def _tile(n, pref, align):
    if n <= pref:
        return n
    t = (pref // align) * align
    while t > align and n % t:
        t -= align
    assert n % t == 0, (n, pref, align)
    return t


def _iota(shape, dim):
    return lax.broadcasted_iota(jnp.int32, shape, dim)


def _adaln_kernel(c_ref, w_ref, b_ref, o_ref):
    c = c_ref[...]
    s = (c * jax.nn.sigmoid(c)).astype(BF16)
    o_ref[...] = jnp.dot(s, w_ref[...].astype(BF16), preferred_element_type=F32) + b_ref[...]


def _adaln(c_all, w, b, li):
    R, D = c_all.shape
    N = w.shape[-1]
    tn = _tile(N, 1024, V7X_LANES)
    return pl.pallas_call(
        _adaln_kernel,
        grid=(N // tn,),
        in_specs=[pl.BlockSpec((R, D), lambda j: (0, 0)),
                  pl.BlockSpec((None, D, tn), lambda j: (li, 0, j)),
                  pl.BlockSpec((None, 1, tn), lambda j: (li, 0, j))],
        out_specs=pl.BlockSpec((R, tn), lambda j: (0, j)),
        out_shape=jax.ShapeDtypeStruct((R, N), F32),
        compiler_params=_cparams(1),
        name="adaln",
    )(c_all, w, b)


def _modulated(x_ref, g_ref, sh_ref, sc_ref):
    x = x_ref[...]
    ms = jnp.mean(x * x, axis=-1, keepdims=True)
    return x * lax.rsqrt(ms + EPS) * g_ref[...] * (1.0 + sc_ref[...]) + sh_ref[...]


def _modulate_kernel(x_ref, g_ref, sh_ref, sc_ref, o_ref):
    o_ref[...] = _modulated(x_ref, g_ref, sh_ref, sc_ref).astype(o_ref.dtype)


def _modulate_route_kernel(x_ref, g_ref, sh_ref, sc_ref, wr_ref, *rest, n_experts):
    o_ref, r_ref = rest[-2:]
    h = _modulated(x_ref, g_ref, sh_ref, sc_ref)
    o_ref[...] = h
    logits = jnp.dot(h, wr_ref[...], preferred_element_type=F32, precision=lax.Precision.HIGHEST)
    lane = _iota(logits.shape, 1)
    logits = jnp.where(lane < n_experts, logits, NEG_BIG)
    m1 = jnp.max(logits, axis=-1, keepdims=True)
    i1 = jnp.min(jnp.where(logits == m1, lane, V7X_LANES), axis=-1, keepdims=True)
    rest = jnp.where(lane == i1, NEG_BIG, logits)
    m2 = jnp.max(rest, axis=-1, keepdims=True)
    i2 = jnp.min(jnp.where(rest == m2, lane, V7X_LANES), axis=-1, keepdims=True)
    e2 = jnp.exp(m2 - m1)
    g1 = 1.0 / (1.0 + e2)
    g2 = e2 / (1.0 + e2)
    r_ref[...] = jnp.where(lane == 0, i1.astype(F32),
                           jnp.where(lane == 1, i2.astype(F32),
                                     jnp.where(lane == 2, g1, jnp.where(lane == 3, g2, 0.0))))


def _mod_specs(B, L, D, R, tm):
    xmap = lambda b, i: (b, i, 0)
    return [pl.BlockSpec((None, tm, D), xmap),
            pl.BlockSpec((1, D), lambda b, i: (0, 0)),
            pl.BlockSpec((None, R, D), lambda b, i: (b, 0, 0)),
            pl.BlockSpec((None, R, D), lambda b, i: (b, 0, 0))]


def _modulate(x, g, shift, scale):
    B, L, D = x.shape
    R = shift.shape[1]
    tm = L if R == L else _tile(L, 512, V7X_SUBLANES)
    return pl.pallas_call(
        _modulate_kernel,
        grid=(B, L // tm),
        in_specs=_mod_specs(B, L, D, R, tm),
        out_specs=pl.BlockSpec((None, tm, D), lambda b, i: (b, i, 0)),
        out_shape=jax.ShapeDtypeStruct((B, L, D), BF16),
        compiler_params=_cparams(2),
        name="modulate",
    )(x, g, shift, scale)


def _modulate_route(x, g, shift, scale, w_router_padded, n_experts, n_tok, row0, shared=None):
    B, L, D = x.shape
    R = shift.shape[1]
    tm = L if R == L else _tile(L, 512, V7X_SUBLANES)
    nb = L // tm
    assert row0 % tm == 0
    blk0 = row0 // tm
    omap = lambda b, i: (blk0 + b * nb + i, 0)
    in_specs = _mod_specs(B, L, D, R, tm) + [pl.BlockSpec((D, V7X_LANES), lambda b, i: (0, 0))]
    args = [x, g, shift, scale, w_router_padded]
    aliases = {}
    if shared is not None:
        in_specs += [pl.BlockSpec(memory_space=pl.ANY), pl.BlockSpec(memory_space=pl.ANY)]
        args += list(shared)
        aliases = {5: 0, 6: 1}
    return pl.pallas_call(
        functools.partial(_modulate_route_kernel, n_experts=n_experts),
        grid=(B, nb),
        in_specs=in_specs,
        out_specs=[pl.BlockSpec((tm, D), omap), pl.BlockSpec((tm, V7X_LANES), omap)],
        out_shape=[jax.ShapeDtypeStruct((n_tok, D), F32),
                   jax.ShapeDtypeStruct((n_tok, V7X_LANES), F32)],
        input_output_aliases=aliases,
        compiler_params=_cparams(2),
        name="modulate_route",
    )(*args)


def _cast_weights_once(first, w_refs, wb_refs):
    @pl.when(first)
    def _():
        for w_ref, wb_ref in zip(w_refs, wb_refs):
            wb_ref[...] = w_ref[...].astype(BF16)


def _mm_plain_kernel(a_ref, w_ref, o_ref, wb_ref):
    first = jnp.logical_and(pl.program_id(1) == 0, pl.program_id(2) == 0)
    _cast_weights_once(first, [w_ref], [wb_ref])
    o_ref[...] = jnp.dot(a_ref[...], wb_ref[...], preferred_element_type=F32).astype(o_ref.dtype)


def _mm_swiglu_kernel(a_ref, w1_ref, w3_ref, o_ref, w1b_ref, w3b_ref):
    first = jnp.logical_and(pl.program_id(1) == 0, pl.program_id(2) == 0)
    _cast_weights_once(first, [w1_ref, w3_ref], [w1b_ref, w3b_ref])
    a = a_ref[...]
    u = jnp.dot(a, w1b_ref[...], preferred_element_type=F32)
    v = jnp.dot(a, w3b_ref[...], preferred_element_type=F32)
    o_ref[...] = (u * jax.nn.sigmoid(u) * v).astype(o_ref.dtype)


def _mm_residual_kernel(a_ref, w_ref, x_ref, g_ref, o_ref, wb_ref):
    first = jnp.logical_and(pl.program_id(1) == 0, pl.program_id(2) == 0)
    _cast_weights_once(first, [w_ref], [wb_ref])
    y = jnp.dot(a_ref[...], wb_ref[...], preferred_element_type=F32)
    o_ref[...] = x_ref[...] + g_ref[...] * y


def _mm_tiles(L, K, N, n_w):
    budget = 28 * 1024 * 1024
    tn = _tile(N, 512, V7X_LANES)
    while tn > V7X_LANES and n_w * K * tn * (2 * 4 + 2) > budget:
        tn = _tile(N, tn // 2, V7X_LANES)
    tm = _tile(L, 1024 if K <= 2048 else 512, V7X_SUBLANES)
    return tm, tn


def _mm_plain(a, w, li, out_dtype):
    B, L, K = a.shape
    N = w.shape[-1]
    tm, tn = _mm_tiles(L, K, N, 1)
    return pl.pallas_call(
        _mm_plain_kernel,
        grid=(N // tn, B, L // tm),
        in_specs=[pl.BlockSpec((None, tm, K), lambda j, b, i: (b, i, 0)),
                  pl.BlockSpec((None, K, tn), lambda j, b, i: (li, 0, j))],
        out_specs=pl.BlockSpec((None, tm, tn), lambda j, b, i: (b, i, j)),
        out_shape=jax.ShapeDtypeStruct((B, L, N), out_dtype),
        scratch_shapes=[pltpu.VMEM((K, tn), BF16)],
        compiler_params=_cparams(3),
        name="mm_plain",
    )(a, w)


def _mm_swiglu(a, w1, w3, li):
    B, L, K = a.shape
    N = w1.shape[-1]
    tm, tn = _mm_tiles(L, K, N, 2)
    wspec = pl.BlockSpec((None, K, tn), lambda j, b, i: (li, 0, j))
    return pl.pallas_call(
        _mm_swiglu_kernel,
        grid=(N // tn, B, L // tm),
        in_specs=[pl.BlockSpec((None, tm, K), lambda j, b, i: (b, i, 0)), wspec, wspec],
        out_specs=pl.BlockSpec((None, tm, tn), lambda j, b, i: (b, i, j)),
        out_shape=jax.ShapeDtypeStruct((B, L, N), BF16),
        scratch_shapes=[pltpu.VMEM((K, tn), BF16), pltpu.VMEM((K, tn), BF16)],
        compiler_params=_cparams(3),
        name="mm_swiglu",
    )(a, w1, w3)


def _mm_residual(a, w, li, x, gate):
    B, L, K = a.shape
    N = w.shape[-1]
    R = gate.shape[1]
    tm, tn = _mm_tiles(L, K, N, 1)
    if R == L:
        tm = L
    return pl.pallas_call(
        _mm_residual_kernel,
        grid=(N // tn, B, L // tm),
        in_specs=[pl.BlockSpec((None, tm, K), lambda j, b, i: (b, i, 0)),
                  pl.BlockSpec((None, K, tn), lambda j, b, i: (li, 0, j)),
                  pl.BlockSpec((None, tm, tn), lambda j, b, i: (b, i, j)),
                  pl.BlockSpec((None, R, tn), lambda j, b, i: (b, 0, j))],
        out_specs=pl.BlockSpec((None, tm, tn), lambda j, b, i: (b, i, j)),
        out_shape=jax.ShapeDtypeStruct((B, L, N), F32),
        scratch_shapes=[pltpu.VMEM((K, tn), BF16)],
        compiler_params=_cparams(3),
        name="mm_residual",
    )(a, w, x, gate)


def _diff_lambda(lq_ref, lam_init):
    lq = lq_ref[...]
    a = jnp.sum(lq[0:1] * lq[1:2], axis=-1, keepdims=True)
    b = jnp.sum(lq[2:3] * lq[3:4], axis=-1, keepdims=True)
    return jnp.exp(a) - jnp.exp(b) + lam_init


def _attn_prompt_kernel(q_ref, k_ref, v_ref, lq_ref, sub_ref, o_ref, *, tq, dk, lam_init):
    qi = pl.program_id(2)
    sc = dk ** -0.5
    q = q_ref[...] * sc
    lane = _iota(q.shape, 1)
    qq = jnp.concatenate([jnp.where(lane < dk, q, 0.0), jnp.where(lane >= dk, q, 0.0)], axis=0).astype(BF16)

    def block(j, carry, masked):
        m, l, acc = carry
        start = pl.multiple_of(j * tq, tq)
        kb = k_ref[pl.ds(start, tq), :].astype(BF16)
        vb = v_ref[pl.ds(start, tq), :].astype(BF16)
        s = lax.dot_general(qq, kb, (((1,), (1,)), ((), ())), preferred_element_type=F32)
        if masked:
            row = _iota(s.shape, 0)
            row = jnp.where(row >= tq, row - tq, row)
            s = jnp.where(_iota(s.shape, 1) <= row, s, NEG_BIG)
        m_new = jnp.maximum(m, jnp.max(s, axis=-1, keepdims=True))
        alpha = jnp.exp(m - m_new)
        p = jnp.exp(s - m_new)
        l = alpha * l + jnp.sum(p, axis=-1, keepdims=True)
        acc = alpha * acc + jnp.dot(p.astype(BF16), vb, preferred_element_type=F32)
        return m_new, l, acc

    init = (jnp.full((2 * tq, 1), NEG_BIG, F32), jnp.zeros((2 * tq, 1), F32),
            jnp.zeros((2 * tq, 2 * dk), F32))
    carry = lax.fori_loop(0, qi, lambda j, c: block(j, c, False), init)
    m, l, acc = block(qi, carry, True)
    on = acc / l
    lam = _diff_lambda(lq_ref, lam_init)
    o = on[:tq] - lam * on[tq:]
    ms = jnp.mean(o * o, axis=-1, keepdims=True)
    o_ref[...] = (o * lax.rsqrt(ms + EPS) * sub_ref[...] * (1.0 - lam_init)).astype(o_ref.dtype)


def _attn_prompt(z, lambda_qk, subln, li, n_heads, d_model, lam_init):
    B, L, _ = z.shape
    hw = 2 * lambda_qk.shape[-1]
    tq = _tile(L, 512, V7X_SUBLANES)
    H = n_heads
    return pl.pallas_call(
        functools.partial(_attn_prompt_kernel, tq=tq, dk=hw // 2, lam_init=lam_init),
        grid=(B, H, L // tq),
        in_specs=[pl.BlockSpec((None, tq, hw), lambda b, h, i: (b, i, h)),
                  pl.BlockSpec((None, L, hw), lambda b, h, i: (b, 0, H + h)),
                  pl.BlockSpec((None, L, hw), lambda b, h, i: (b, 0, 2 * H + h)),
                  pl.BlockSpec((None,) + lambda_qk.shape[1:], lambda b, h, i: (li, 0, 0)),
                  pl.BlockSpec((None, 1, hw), lambda b, h, i: (li, 0, h))],
        out_specs=pl.BlockSpec((None, tq, hw), lambda b, h, i: (b, i, h)),
        out_shape=jax.ShapeDtypeStruct((B, L, d_model), BF16),
        compiler_params=_cparams(3),
        name="attn_prompt",
    )(z, z, z, lambda_qk, subln)


def _attn_sample_kernel(pt_ref, q_ref, kn_ref, vn_ref, lq_ref, sub_ref, *rest, n_pages, n_heads, dk, lam_init):
    kt_refs = rest[:n_pages]
    v_refs = rest[n_pages:2 * n_pages]
    o_ref, m_ref, l_ref, acc_ref = rest[2 * n_pages:]
    s_id = pl.program_id(1)
    H = n_heads
    hw = 2 * dk
    W = H * hw
    page = kt_refs[0].shape[-1]
    sc = dk ** -0.5

    @pl.when(s_id == 0)
    def _():
        m_ref[...] = jnp.full(m_ref.shape, NEG_BIG, F32)
        l_ref[...] = jnp.zeros(l_ref.shape, F32)
        acc_ref[...] = jnp.zeros(acc_ref.shape, F32)

    row = _iota((2 * H, W), 0)
    lane = _iota((2 * H, W), 1)
    r_head = jnp.where(row >= H, row - H, row)
    r_comp = jnp.where(row >= H, 1, 0)
    qsel = jnp.logical_and(lane // hw == r_head, (lane // dk) % 2 == r_comp)
    qf = jnp.where(qsel, q_ref[...] * sc, 0.0)
    qb = qf.astype(BF16)

    s_parts = [jnp.dot(qb, kt_refs[r][...].astype(BF16), preferred_element_type=F32) for r in range(n_pages)]
    s = jnp.concatenate(s_parts, axis=-1) if n_pages > 1 else s_parts[0]
    m = m_ref[...]
    m_new = jnp.maximum(m, jnp.max(s, axis=-1, keepdims=True))
    alpha = jnp.exp(m - m_new)
    p = jnp.exp(s - m_new)
    l_ref[...] = alpha * l_ref[...] + jnp.sum(p, axis=-1, keepdims=True)
    m_ref[...] = m_new
    p_head = r_head[:, :page]
    pv = jnp.zeros((2 * H, hw), F32)
    for r in range(n_pages):
        pr = p[:, r * page:(r + 1) * page]
        pexp = jnp.concatenate([jnp.where(p_head == h, pr, 0.0) for h in range(H)], axis=1).astype(BF16)
        vh = jnp.concatenate([v_refs[r][pl.ds(h, page, stride=H), :] for h in range(H)], axis=0).astype(BF16)
        pv = pv + jnp.dot(pexp, vh, preferred_element_type=F32)
    acc_ref[...] = alpha * acc_ref[...] + pv

    @pl.when(s_id == pl.num_programs(1) - 1)
    def _():
        s_new = jnp.sum(qf * kn_ref[...], axis=-1, keepdims=True)
        m_old = m_ref[...]
        m_fin = jnp.maximum(m_old, s_new)
        a_fin = jnp.exp(m_old - m_fin)
        p_new = jnp.exp(s_new - m_fin)
        l_fin = a_fin * l_ref[...] + p_new
        vn = vn_ref[...]
        acc = a_fin * acc_ref[...] + p_new * jnp.concatenate([vn, vn], axis=0)
        on = acc / l_fin
        lam = _diff_lambda(lq_ref, lam_init)
        o = on[:H] - lam * on[H:]
        ms = jnp.mean(o * o, axis=-1, keepdims=True)
        o_ref[...] = (o * lax.rsqrt(ms + EPS) * sub_ref[...] * (1.0 - lam_init)).astype(o_ref.dtype)


def _attn_sample(z, v_new, cache_kt, cache_v, page_table, lambda_qk, subln, li, n_heads, lam_init):
    Bs = z.shape[0]
    n_phys_total, W, page = cache_kt.shape
    n_pg = page_table.shape[1]
    P = next(c for c in (8, 4, 2, 1) if n_pg % c == 0)
    n_phys = n_phys_total // lambda_qk.shape[0]
    base = li * n_phys
    H = n_heads
    dk = lambda_qk.shape[-1]
    hw = 2 * dk

    def page_spec(shape, r):
        return pl.BlockSpec((None,) + shape, lambda b, s, pt: (base + pt[b, s * P + r], 0, 0))

    zrow = lambda c: pl.BlockSpec((None, 1, W), lambda b, s, pt: (b, 0, c))
    grid_spec = pltpu.PrefetchScalarGridSpec(
        num_scalar_prefetch=1,
        grid=(Bs, n_pg // P),
        in_specs=[zrow(0), zrow(1),
                  pl.BlockSpec((None, H, hw), lambda b, s, pt: (b, 0, 0)),
                  pl.BlockSpec((None,) + lambda_qk.shape[1:], lambda b, s, pt: (li, 0, 0)),
                  pl.BlockSpec((None, H, hw), lambda b, s, pt: (li, 0, 0))]
                 + [page_spec((W, page), r) for r in range(P)]
                 + [page_spec((page * H, hw), r) for r in range(P)],
        out_specs=pl.BlockSpec((None, H, hw), lambda b, s, pt: (b, 0, 0)),
        scratch_shapes=[pltpu.VMEM((2 * H, 1), F32), pltpu.VMEM((2 * H, 1), F32),
                        pltpu.VMEM((2 * H, hw), F32)],
    )
    return pl.pallas_call(
        functools.partial(_attn_sample_kernel, n_pages=P, n_heads=H, dk=dk, lam_init=lam_init),
        grid_spec=grid_spec,
        out_shape=jax.ShapeDtypeStruct((Bs, H, hw), BF16),
        compiler_params=_cparams(2),
        name="attn_sample",
    )(page_table, z, z, v_new, lambda_qk, subln, *([cache_kt] * P), *([cache_v] * P))


def _rope(x, cos, sin):
    half = x.shape[-1] // 2
    x1, x2 = x[:, :half], x[:, half:]
    return jnp.concatenate([x1 * cos - x2 * sin, x1 * sin + x2 * cos], axis=-1)


def _groupnorm_gate(o, gn, g):
    mu = jnp.mean(o, axis=-1, keepdims=True)
    d = o - mu
    var = jnp.mean(d * d, axis=-1, keepdims=True)
    return d * lax.rsqrt(var + EPS) * gn * (g * jax.nn.sigmoid(g))


def _ret_prompt_kernel(o_in_ref, q_ref, k_ref, v_ref, g_ref, cos_ref, sin_ref, lg_ref, gn_ref,
                       o_ref, sfin_ref, s_ref, *, chunk, dk):
    del o_in_ref
    c = pl.program_id(2)

    @pl.when(c == 0)
    def _():
        s_ref[...] = jnp.zeros(s_ref.shape, F32)

    cos, sin = cos_ref[...], sin_ref[...]
    qr = _rope(q_ref[...], cos, sin) * dk ** -0.5
    kr = _rope(k_ref[...], cos, sin)
    vb = v_ref[...].astype(BF16)
    lg = lg_ref[...][:, :1]
    C = chunk
    d = _iota((C, C), 0) - _iota((C, C), 1)
    decay = jnp.where(d >= 0, jnp.exp(jnp.maximum(d, 0).astype(F32) * lg), 0.0)
    qrb = qr.astype(BF16)
    scores = lax.dot_general(qrb, kr.astype(BF16), (((1,), (1,)), ((), ())), preferred_element_type=F32) * decay
    intra = jnp.dot(scores.astype(BF16), vb, preferred_element_type=F32)
    t = _iota((C, 1), 0).astype(F32)
    s_old = s_ref[...]
    inter = jnp.dot(qrb, s_old.astype(BF16), preferred_element_type=F32) * jnp.exp((t + 1.0) * lg)
    kd = kr * jnp.exp((C - 1.0 - t) * lg)
    s_new = jnp.exp(C * lg) * s_old + jnp.dot(kd.T.astype(BF16), vb, preferred_element_type=F32)
    s_ref[...] = s_new
    sfin_ref[...] = s_new
    o_ref[...] = _groupnorm_gate(intra + inter, gn_ref[...], g_ref[...]).astype(o_ref.dtype)


def _ret_prompt(o_buf, z, cos_tab, sin_tab, log_gamma, gn_b, li, n_heads_a, n_heads_b):
    B, L, _ = z.shape
    dk = 2 * cos_tab.shape[-1]
    HB = n_heads_b
    off = 3 * n_heads_a * V7X_LANES // dk
    o_off = n_heads_a * V7X_LANES // dk
    C = _tile(L, 256, V7X_SUBLANES)
    zb = lambda g: pl.BlockSpec((None, C, dk), lambda b, h, c: (b, c, off + g * HB + h))
    return pl.pallas_call(
        functools.partial(_ret_prompt_kernel, chunk=C, dk=dk),
        grid=(B, HB, L // C),
        in_specs=[pl.BlockSpec(memory_space=pl.ANY), zb(0), zb(1), zb(2), zb(3),
                  pl.BlockSpec((C, dk // 2), lambda b, h, c: (c, 0)),
                  pl.BlockSpec((C, dk // 2), lambda b, h, c: (c, 0)),
                  pl.BlockSpec((None, 1, V7X_LANES), lambda b, h, c: (h, 0, 0)),
                  pl.BlockSpec((None, 1, dk), lambda b, h, c: (li, 0, h))],
        out_specs=[pl.BlockSpec((None, C, dk), lambda b, h, c: (b, c, o_off + h)),
                   pl.BlockSpec((None, None, dk, dk), lambda b, h, c: (b, h, 0, 0))],
        out_shape=[jax.ShapeDtypeStruct(o_buf.shape, o_buf.dtype),
                   jax.ShapeDtypeStruct((B, HB, dk, dk), F32)],
        scratch_shapes=[pltpu.VMEM((dk, dk), F32)],
        input_output_aliases={0: 0},
        compiler_params=_cparams(3),
        name="ret_prompt",
    )(o_buf, z, z, z, z, cos_tab, sin_tab, log_gamma, gn_b)


def _ret_sample_kernel(o_in_ref, q_ref, k_ref, v_ref, g_ref, cos_ref, sin_ref, lg_ref, gn_ref, s0_ref,
                       o_ref, snew_ref, *, dk):
    del o_in_ref
    cos, sin = cos_ref[...], sin_ref[...]
    qr = _rope(q_ref[...], cos, sin) * dk ** -0.5
    kr = _rope(k_ref[...], cos, sin)
    v = v_ref[...]
    gamma = jnp.exp(lg_ref[...][:, :1])
    eye = _iota((dk, dk), 0) == _iota((dk, dk), 1)
    kcol = jnp.sum(jnp.where(eye, kr, 0.0), axis=-1, keepdims=True)
    qcol = jnp.sum(jnp.where(eye, qr, 0.0), axis=-1, keepdims=True)
    s0 = s0_ref[...]
    snew_ref[...] = gamma * s0 + kcol * v
    inter = gamma * jnp.sum(qcol * s0, axis=0, keepdims=True)
    intra = jnp.sum(qr * kr, axis=-1, keepdims=True) * v
    o_ref[...] = _groupnorm_gate(intra + inter, gn_ref[...], g_ref[...]).astype(o_ref.dtype)


def _ret_sample(o_buf, z, cos_row, sin_row, log_gamma, gn_b, state, li, n_heads_a, n_heads_b):
    Bs = z.shape[0]
    dk = 2 * cos_row.shape[-1]
    HB = n_heads_b
    off = 3 * n_heads_a * V7X_LANES // dk
    o_off = n_heads_a * V7X_LANES // dk
    zb = lambda g: pl.BlockSpec((None, 1, dk), lambda b, h: (b, 0, off + g * HB + h))
    return pl.pallas_call(
        functools.partial(_ret_sample_kernel, dk=dk),
        grid=(Bs, HB),
        in_specs=[pl.BlockSpec(memory_space=pl.ANY), zb(0), zb(1), zb(2), zb(3),
                  pl.BlockSpec((1, dk // 2), lambda b, h: (0, 0)),
                  pl.BlockSpec((1, dk // 2), lambda b, h: (0, 0)),
                  pl.BlockSpec((None, 1, V7X_LANES), lambda b, h: (h, 0, 0)),
                  pl.BlockSpec((None, 1, dk), lambda b, h: (li, 0, h)),
                  pl.BlockSpec((None, None, dk, dk), lambda b, h: (li * Bs + b, h, 0, 0))],
        out_specs=[pl.BlockSpec((None, 1, dk), lambda b, h: (b, 0, o_off + h)),
                   pl.BlockSpec((None, None, dk, dk), lambda b, h: (b, h, 0, 0))],
        out_shape=[jax.ShapeDtypeStruct(o_buf.shape, o_buf.dtype),
                   jax.ShapeDtypeStruct((Bs, HB, dk, dk), F32)],
        input_output_aliases={0: 0},
        compiler_params=_cparams(2),
        name="ret_sample",
    )(o_buf, z, z, z, z, cos_row, sin_row, log_gamma, gn_b, state)


def _softplus(z):
    return jnp.maximum(z, 0.0) + jnp.log1p(jnp.exp(-jnp.abs(z)))


def _lru_coeffs(xc, wa_ref, wx_ref, ba_ref, bx_ref, lam_ref):
    xcb = xc.astype(BF16)
    r = jax.nn.sigmoid(jnp.dot(xcb, wa_ref[...].astype(BF16), preferred_element_type=F32) + ba_ref[...])
    gi = jax.nn.sigmoid(jnp.dot(xcb, wx_ref[...].astype(BF16), preferred_element_type=F32) + bx_ref[...])
    log_a = -LRU_C * r * _softplus(-lam_ref[...])
    a = jnp.exp(log_a)
    t = jnp.tanh(log_a)
    mult = jnp.sqrt(jnp.maximum(-2.0 * t / (1.0 - t), 0.0))
    return a, mult * gi * xc


def _lru_prompt_kernel(gate_ref, x_ref, cw_ref, cb_ref, wa_ref, wx_ref, ba_ref, bx_ref, lam_ref,
                       tail0_ref, h0_ref, y_ref, hl_ref, tail_ref, hc_ref, *, tt, conv_w):
    t_id = pl.program_id(2)

    @pl.when(t_id == 0)
    def _():
        tail_ref[...] = tail0_ref[...]
        hc_ref[...] = h0_ref[...]

    x = x_ref[...]
    tl = tail_ref[...]
    row8 = _iota(tl.shape, 0)
    xc = cb_ref[...] + cw_ref[conv_w - 1:conv_w, :] * x
    for s in range(1, conv_w):
        xs = pltpu.roll(x, s, 0)
        head = jnp.where(row8 >= s, xs[:V7X_SUBLANES], pltpu.roll(tl, s, 0))
        xs = jnp.concatenate([head, xs[V7X_SUBLANES:]], axis=0) if tt > V7X_SUBLANES else head
        xc = xc + cw_ref[conv_w - 1 - s:conv_w - s, :] * xs
    tail_ref[...] = x[tt - V7X_SUBLANES:, :]

    a, b = _lru_coeffs(xc, wa_ref, wx_ref, ba_ref, bx_ref, lam_ref)
    row = _iota(a.shape, 0)
    s = 1
    while s < tt:
        if s % V7X_SUBLANES == 0:
            a_s = jnp.concatenate([jnp.ones((s, a.shape[1]), F32), a[:tt - s]], axis=0)
            b_s = jnp.concatenate([jnp.zeros((s, a.shape[1]), F32), b[:tt - s]], axis=0)
        else:
            keep = row >= s
            a_s = jnp.where(keep, pltpu.roll(a, s, 0), 1.0)
            b_s = jnp.where(keep, pltpu.roll(b, s, 0), 0.0)
        b = b + a * b_s
        a = a * a_s
        s *= 2
    h = b + a * hc_ref[...]
    hc_ref[...] = h[tt - 1:tt, :]
    hl_ref[...] = h[tt - 1:tt, :]
    y_ref[...] = (jax.nn.gelu(gate_ref[...], approximate=True) * h).astype(y_ref.dtype)


def _lru_prompt(z, conv_w, conv_b, w_ga, w_gx, b_ga, b_gx, lam, tail0, h0, li):
    B, L, W2 = z.shape
    W = W2 // 2
    bw = w_ga.shape[-1]
    NB = W // bw
    CW = conv_w.shape[1]
    tt = _tile(L, 256, V7X_SUBLANES)
    vec = lambda: pl.BlockSpec((None, 1, bw), lambda b, n, t: (li, 0, n))
    gw = lambda: pl.BlockSpec((None, bw, bw), lambda b, n, t: (li * NB + n, 0, 0))
    return pl.pallas_call(
        functools.partial(_lru_prompt_kernel, tt=tt, conv_w=CW),
        grid=(B, NB, L // tt),
        in_specs=[pl.BlockSpec((None, tt, bw), lambda b, n, t: (b, t, n)),
                  pl.BlockSpec((None, tt, bw), lambda b, n, t: (b, t, NB + n)),
                  pl.BlockSpec((None, CW, bw), lambda b, n, t: (li, 0, n)),
                  vec(), gw(), gw(), vec(), vec(), vec(),
                  pl.BlockSpec((None, V7X_SUBLANES, bw), lambda b, n, t: (b, 0, n)),
                  pl.BlockSpec((None, 1, bw), lambda b, n, t: (b, 0, n))],
        out_specs=[pl.BlockSpec((None, tt, bw), lambda b, n, t: (b, t, n)),
                   pl.BlockSpec((None, 1, bw), lambda b, n, t: (b, 0, n))],
        out_shape=[jax.ShapeDtypeStruct((B, L, W), BF16), jax.ShapeDtypeStruct((B, 1, W), F32)],
        scratch_shapes=[pltpu.VMEM((V7X_SUBLANES, bw), F32), pltpu.VMEM((1, bw), F32)],
        compiler_params=_cparams(3),
        name="lru_prompt",
    )(z, z, conv_w, conv_b, w_ga, w_gx, b_ga, b_gx, lam, tail0, h0)


def _lru_sample_kernel(gate_ref, x_ref, cw_ref, cb_ref, wa_ref, wx_ref, ba_ref, bx_ref, lam_ref,
                       buf_ref, h0_ref, y_ref, h_ref, *, conv_w):
    x = x_ref[...]
    xc = cb_ref[...] + cw_ref[conv_w - 1:conv_w, :] * x
    for j in range(conv_w - 1):
        xc = xc + cw_ref[j:j + 1, :] * buf_ref[j]
    a, b = _lru_coeffs(xc, wa_ref, wx_ref, ba_ref, bx_ref, lam_ref)
    h = a * h0_ref[...] + b
    h_ref[...] = h
    y_ref[...] = (jax.nn.gelu(gate_ref[...], approximate=True) * h).astype(y_ref.dtype)


def _lru_sample(z, conv_w, conv_b, w_ga, w_gx, b_ga, b_gx, lam, buf, h0, li):
    Bs, W2 = z.shape
    W = W2 // 2
    bw = w_ga.shape[-1]
    NB = W // bw
    CW = conv_w.shape[1]
    vec = lambda: pl.BlockSpec((None, 1, bw), lambda n: (li, 0, n))
    gw = lambda: pl.BlockSpec((None, bw, bw), lambda n: (li * NB + n, 0, 0))
    return pl.pallas_call(
        functools.partial(_lru_sample_kernel, conv_w=CW),
        grid=(NB,),
        in_specs=[pl.BlockSpec((Bs, bw), lambda n: (0, n)),
                  pl.BlockSpec((Bs, bw), lambda n: (0, NB + n)),
                  pl.BlockSpec((None, CW, bw), lambda n: (li, 0, n)),
                  vec(), gw(), gw(), vec(), vec(), vec(),
                  pl.BlockSpec((CW - 1, Bs, bw), lambda n: (0, 0, n)),
                  pl.BlockSpec((Bs, bw), lambda n: (0, n))],
        out_specs=[pl.BlockSpec((Bs, bw), lambda n: (0, n)),
                   pl.BlockSpec((Bs, bw), lambda n: (0, n))],
        out_shape=[jax.ShapeDtypeStruct((Bs, W), BF16), jax.ShapeDtypeStruct((Bs, W), F32)],
        compiler_params=_cparams(1),
        name="lru_sample",
    )(z, z, conv_w, conv_b, w_ga, w_gx, b_ga, b_gx, lam, buf, h0)


MOE_ROW_BLOCK = 512
MOE_SUB_BLOCK = 128


def _valid_rows_only(n_valid, o_ref, compute):
    tm = o_ref.shape[0]

    @pl.when(n_valid == tm)
    def _():
        o_ref[...] = compute(slice(None))

    for s in range(tm // MOE_SUB_BLOCK):
        rows = slice(s * MOE_SUB_BLOCK, (s + 1) * MOE_SUB_BLOCK)

        @pl.when(jnp.logical_and(n_valid < tm, n_valid > s * MOE_SUB_BLOCK))
        def _(rows=rows):
            o_ref[rows, :] = compute(rows)

        @pl.when(n_valid <= s * MOE_SUB_BLOCK)
        def _(rows=rows):
            o_ref[rows, :] = jnp.zeros((MOE_SUB_BLOCK, o_ref.shape[1]), o_ref.dtype)


def _moe_up_kernel(be_ref, bf_ref, bv_ref, x_ref, w1_ref, w3_ref, o_ref, w1b_ref, w3b_ref):
    rb = pl.program_id(1)
    _cast_weights_once(bf_ref[rb] == 1, [w1_ref, w3_ref], [w1b_ref, w3b_ref])

    def compute(rows):
        x = x_ref[rows, :]
        u = jnp.dot(x, w1b_ref[...], preferred_element_type=F32)
        v = jnp.dot(x, w3b_ref[...], preferred_element_type=F32)
        return (u * jax.nn.sigmoid(u) * v).astype(o_ref.dtype)

    _valid_rows_only(bv_ref[rb], o_ref, compute)


def _moe_down_kernel(be_ref, bf_ref, bv_ref, a_ref, w_ref, o_ref, wb_ref):
    rb = pl.program_id(1)
    _cast_weights_once(bf_ref[rb] == 1, [w_ref], [wb_ref])

    def compute(rows):
        return jnp.dot(a_ref[rows, :], wb_ref[...], preferred_element_type=F32)

    _valid_rows_only(bv_ref[rb], o_ref, compute)


def _moe_up(xs, w1, w3, blk_e, blk_first, blk_valid, li, n_experts):
    n_rows, D = xs.shape
    F = w1.shape[-1]
    tm = MOE_ROW_BLOCK
    tn = _tile(F, 512, V7X_LANES)
    wspec = pl.BlockSpec((None, D, tn), lambda j, rb, be, bf, nu: (li * n_experts + be[rb], 0, j))
    grid_spec = pltpu.PrefetchScalarGridSpec(
        num_scalar_prefetch=3,
        grid=(F // tn, n_rows // tm),
        in_specs=[pl.BlockSpec((tm, D), lambda j, rb, be, bf, nu: (rb, 0)), wspec, wspec],
        out_specs=pl.BlockSpec((tm, tn), lambda j, rb, be, bf, nu: (rb, j)),
        scratch_shapes=[pltpu.VMEM((D, tn), BF16), pltpu.VMEM((D, tn), BF16)],
    )
    return pl.pallas_call(
        _moe_up_kernel,
        grid_spec=grid_spec,
        out_shape=jax.ShapeDtypeStruct((n_rows, F), BF16),
        compiler_params=_cparams(2),
        name="moe_up",
    )(blk_e, blk_first, blk_valid, xs, w1, w3)


def _moe_down(hh, w2, blk_e, blk_first, blk_valid, li, n_experts):
    n_rows, F = hh.shape
    D = w2.shape[-1]
    tm = MOE_ROW_BLOCK
    tn = _tile(D, 512, V7X_LANES)
    grid_spec = pltpu.PrefetchScalarGridSpec(
        num_scalar_prefetch=3,
        grid=(D // tn, n_rows // tm),
        in_specs=[pl.BlockSpec((tm, F), lambda j, rb, be, bf, nu: (rb, 0)),
                  pl.BlockSpec((None, F, tn), lambda j, rb, be, bf, nu: (li * n_experts + be[rb], 0, j))],
        out_specs=pl.BlockSpec((tm, tn), lambda j, rb, be, bf, nu: (rb, j)),
        scratch_shapes=[pltpu.VMEM((F, tn), BF16)],
    )
    return pl.pallas_call(
        _moe_down_kernel,
        grid_spec=grid_spec,
        out_shape=jax.ShapeDtypeStruct((n_rows, D), F32),
        compiler_params=_cparams(2),
        name="moe_down",
    )(blk_e, blk_first, blk_valid, hh, w2)


def _dispatch_kernel(src_ref, bv_ref, tab_ref, o_ref, buf_ref, *, tm):
    rb = pl.program_id(1)

    @pl.when(bv_ref[rb] > 0)
    def _():
        base = rb * tm

        def body(r, carry):
            buf_ref[pl.ds(r, 1), :] = tab_ref[pl.ds(src_ref[base + r], 1), :]
            return carry

        lax.fori_loop(0, tm, body, 0, unroll=8)
        o_ref[...] = buf_ref[...].astype(o_ref.dtype)

    @pl.when(bv_ref[rb] == 0)
    def _():
        o_ref[...] = jnp.zeros(o_ref.shape, o_ref.dtype)


def _dispatch(table, src, blk_valid):
    n_tok, D = table.shape
    n_rows = src.shape[0]
    tm = MOE_ROW_BLOCK
    tc = D
    while tc > V7X_LANES and n_tok * tc * 4 > 36 * 1024 * 1024:
        tc = _tile(D, tc // 2, V7X_LANES)
    grid_spec = pltpu.PrefetchScalarGridSpec(
        num_scalar_prefetch=2,
        grid=(D // tc, n_rows // tm),
        in_specs=[pl.BlockSpec((n_tok, tc), lambda j, rb, src, nu: (0, j), pipeline_mode=pl.Buffered(1))],
        out_specs=pl.BlockSpec((tm, tc), lambda j, rb, src, nu: (rb, j)),
        scratch_shapes=[pltpu.VMEM((tm, tc), F32)],
    )
    return pl.pallas_call(
        functools.partial(_dispatch_kernel, tm=tm),
        grid_spec=grid_spec,
        out_shape=jax.ShapeDtypeStruct((n_rows, D), BF16),
        compiler_params=_cparams(2),
        name="moe_dispatch",
    )(src, blk_valid, table)


def _combine_kernel(d_ref, tab_ref, route_ref, o_ref, a_ref, b_ref, *, tm):
    base = pl.program_id(1) * (tm * TOP_K)

    def body(t, carry):
        a_ref[pl.ds(t, 1), :] = tab_ref[pl.ds(d_ref[base + TOP_K * t], 1), :]
        b_ref[pl.ds(t, 1), :] = tab_ref[pl.ds(d_ref[base + TOP_K * t + 1], 1), :]
        return carry

    lax.fori_loop(0, tm, body, 0, unroll=8)
    rt = route_ref[...]
    o_ref[...] = rt[:, TOP_K:TOP_K + 1] * a_ref[...] + rt[:, TOP_K + 1:TOP_K + 2] * b_ref[...]


def _combine(ys, dest, route):
    assert TOP_K == 2
    n_rows, D = ys.shape
    n_tok = route.shape[0]
    tm = _tile(n_tok, 512, V7X_SUBLANES)
    tc = _tile(D, 512, V7X_LANES)
    while tc > V7X_LANES and n_rows * tc * 4 > 44 * 1024 * 1024:
        tc = _tile(D, tc // 2, V7X_LANES)
    grid_spec = pltpu.PrefetchScalarGridSpec(
        num_scalar_prefetch=1,
        grid=(D // tc, n_tok // tm),
        in_specs=[pl.BlockSpec((n_rows, tc), lambda j, i, d: (0, j), pipeline_mode=pl.Buffered(1)),
                  pl.BlockSpec((tm, V7X_LANES), lambda j, i, d: (i, 0))],
        out_specs=pl.BlockSpec((tm, tc), lambda j, i, d: (i, j)),
        scratch_shapes=[pltpu.VMEM((tm, tc), F32), pltpu.VMEM((tm, tc), F32)],
    )
    return pl.pallas_call(
        functools.partial(_combine_kernel, tm=tm),
        grid_spec=grid_spec,
        out_shape=jax.ShapeDtypeStruct((n_tok, D), F32),
        compiler_params=_cparams(2),
        name="moe_combine",
    )(dest, ys, route)


def _residual_kernel(x_ref, y_ref, g_ref, o_ref):
    o_ref[...] = x_ref[...] + g_ref[...] * y_ref[...]


def _residual_norm_kernel(x_ref, y_ref, g_ref, nf_ref, o_ref):
    x = x_ref[...] + g_ref[...] * y_ref[...]
    ms = jnp.mean(x * x, axis=-1, keepdims=True)
    o_ref[...] = x * lax.rsqrt(ms + EPS) * nf_ref[...]


def _norm_kernel(x_ref, nf_ref, o_ref):
    x = x_ref[...]
    ms = jnp.mean(x * x, axis=-1, keepdims=True)
    o_ref[...] = x * lax.rsqrt(ms + EPS) * nf_ref[...]


def _residual(x, y_tok, row0, gate, norm_g):
    B, L, D = x.shape
    R = gate.shape[1]
    tm = L if R == L else _tile(L, 512, V7X_SUBLANES)
    nb = L // tm
    assert row0 % tm == 0
    blk0 = row0 // tm
    blk = pl.BlockSpec((None, tm, D), lambda b, i: (b, i, 0))
    in_specs = [blk, pl.BlockSpec((tm, D), lambda b, i: (blk0 + b * nb + i, 0)),
                pl.BlockSpec((None, R, D), lambda b, i: (b, 0, 0))]
    args = [x, y_tok, gate]
    kern = _residual_kernel
    if norm_g is not None:
        in_specs.append(pl.BlockSpec((1, D), lambda b, i: (0, 0)))
        args.append(norm_g)
        kern = _residual_norm_kernel
    return pl.pallas_call(
        kern, grid=(B, L // tm), in_specs=in_specs, out_specs=blk,
        out_shape=jax.ShapeDtypeStruct((B, L, D), F32),
        compiler_params=_cparams(2), name="residual",
    )(*args)


def _final_norm(x, norm_g):
    B, L, D = x.shape
    tm = _tile(L, 512, V7X_SUBLANES)
    blk = pl.BlockSpec((None, tm, D), lambda b, i: (b, i, 0))
    return pl.pallas_call(
        _norm_kernel, grid=(B, L // tm),
        in_specs=[blk, pl.BlockSpec((1, D), lambda b, i: (0, 0))], out_specs=blk,
        out_shape=jax.ShapeDtypeStruct((B, L, D), F32),
        compiler_params=_cparams(2), name="final_norm",
    )(x, norm_g)


def _split_mods(m, n_p, n_s):
    D = m.shape[-1] // 6
    mp = [m[:n_p, k * D:(k + 1) * D][:, None, :] for k in range(6)]
    ms = [m[n_p:n_p + n_s, k * D:(k + 1) * D][None, :, :] for k in range(6)]
    return mp, ms


def _stack_cond(c_prompt, c_sample):
    c_all = jnp.concatenate([c_prompt, c_sample], axis=0)
    pad = (-c_all.shape[0]) % V7X_SUBLANES
    if pad:
        c_all = jnp.concatenate([c_all, jnp.zeros((pad, c_all.shape[1]), c_all.dtype)], axis=0)
    return c_all


def _moe(table, route, w1, w3, w2, li, n_experts):
    T = table.shape[0]
    E = n_experts
    tm = MOE_ROW_BLOCK
    e_flat = route[:, :TOP_K].astype(jnp.int32).reshape(-1)
    n_slots = T * TOP_K
    onehot = (e_flat[:, None] == jnp.arange(E, dtype=jnp.int32)[None, :]).astype(jnp.int32)
    csum = jnp.cumsum(onehot, axis=0)
    counts = csum[-1]
    rank = jnp.sum((csum - onehot) * onehot, axis=-1)
    padded = (counts + tm - 1) // tm * tm
    pends = jnp.cumsum(padded)
    pstarts = pends - padded
    dest = (jnp.sum(onehot * pstarts[None, :], axis=-1) + rank).astype(jnp.int32)
    n_blocks = -(-n_slots // tm) + E
    n_rows = n_blocks * tm
    tok_flat = jnp.arange(n_slots, dtype=jnp.int32) // TOP_K
    src = jnp.zeros((n_rows,), jnp.int32).at[dest].set(tok_flat)
    blk_start = jnp.arange(n_blocks, dtype=jnp.int32) * tm
    blk_e = jnp.minimum(jnp.sum((blk_start[:, None] >= pends[None, :]).astype(jnp.int32), axis=-1), E - 1)
    blk_first = jnp.concatenate([jnp.ones((1,), jnp.int32), (blk_e[1:] != blk_e[:-1]).astype(jnp.int32)])
    seg_end = jnp.sum((blk_e[:, None] == jnp.arange(E, dtype=jnp.int32)[None, :]) * (pstarts + counts)[None, :], axis=-1)
    blk_valid = jnp.clip(seg_end - blk_start, 0, tm).astype(jnp.int32)
    xs = _dispatch(table, src, blk_valid)
    hh = _moe_up(xs, w1, w3, blk_e, blk_first, blk_valid, li, E)
    ys = _moe_down(hh, w2, blk_e, blk_first, blk_valid, li, E)
    return _combine(ys, dest, route)


def kernel(x_prompt, x_sample, c_prompt, c_sample, cache_k, cache_v, state_ret, state_conv, state_lru, page_table, ada_w_even, ada_b_even, norm_mix_even, norm_ffn_even, w_in_even, lambda_qk, subln_a, gn_b, w_out_even, ffn_w1, ffn_w3, ffn_w2, ada_w_odd, ada_b_odd, norm_mix_odd, norm_ffn_odd, w_in_odd, conv_w, conv_b, w_gate_a, b_gate_a, w_gate_x, b_gate_x, lru_lambda, w_out_odd, router, moe_w1, moe_w3, moe_w2, norm_final):
    Bp, Lp, D = x_prompt.shape
    Bs, Ls, _ = x_sample.shape
    assert Ls == 1, "the sample group carries one new token per sequence"
    n_even, n_odd = w_in_even.shape[0], w_in_odd.shape[0]
    depth = n_even + n_odd
    _, n_phys, page, HA, _, dka = cache_k.shape
    HB, dkb = state_ret.shape[2], state_ret.shape[3]
    past_len = page_table.shape[1] * page
    E = router.shape[-1]
    CW = conv_w.shape[1]
    W = conv_w.shape[-1]

    half = dkb // 2
    freqs = ROPE_BASE ** (-jnp.arange(half, dtype=F32) / half)
    ang_p = jnp.arange(Lp).astype(F32)[:, None] * freqs[None, :]
    ang_s = (past_len + jnp.arange(Ls)).astype(F32)[:, None] * freqs[None, :]
    cos_p, sin_p, cos_s, sin_s = jnp.cos(ang_p), jnp.sin(ang_p), jnp.cos(ang_s), jnp.sin(ang_s)
    log_gamma = jnp.log1p(-jnp.exp2(-5.0 - jnp.arange(HB, dtype=F32)))
    log_gamma = jnp.broadcast_to(log_gamma[:, None, None], (HB, 1, V7X_LANES))

    c_all = _stack_cond(c_prompt, c_sample)
    row3 = lambda a: a.reshape(a.shape[0], 1, a.shape[-1])
    cache_kt = jnp.transpose(cache_k, (0, 1, 3, 4, 5, 2)).reshape(n_even * n_phys, HA * 2 * dka, page)
    cache_v2 = cache_v.reshape(n_even * n_phys, page * HA, 2 * dka)
    subln3 = subln_a.reshape(n_even, HA, 2 * dka)
    state_ret2 = state_ret.reshape(n_even * Bs, HB, dkb, dkb)
    w_ga2 = w_gate_a.reshape(-1, w_gate_a.shape[-2], w_gate_a.shape[-1])
    w_gx2 = w_gate_x.reshape(-1, w_gate_x.shape[-2], w_gate_x.shape[-1])
    moe_w1f = moe_w1.reshape(-1, moe_w1.shape[-2], moe_w1.shape[-1])
    moe_w3f = moe_w3.reshape(-1, moe_w3.shape[-2], moe_w3.shape[-1])
    moe_w2f = moe_w2.reshape(-1, moe_w2.shape[-2], moe_w2.shape[-1])
    router_pad = jnp.concatenate(
        [router, jnp.zeros(router.shape[:-1] + (V7X_LANES - E,), router.dtype)], axis=-1)
    nf = norm_final.reshape(1, D)

    xp = x_prompt
    xs = x_sample.reshape(1, Bs, D)
    kp_l, vp_l, rp_l, cp_l, lp_l = [], [], [], [], []
    ks_l, vs_l, rs_l, cs_l, ls_l = [], [], [], [], []
    normed = False
    for l in range(depth):
        i = l // 2
        last = l == depth - 1
        if l % 2 == 0:
            lam_init = 0.8 - 0.6 * math.exp(-0.3 * l)
            m = _adaln(c_all, ada_w_even, row3(ada_b_even), i)
            (sh1, sc1, g1, sh2, sc2, g2), (sh1s, sc1s, g1s, sh2s, sc2s, g2s) = _split_mods(m, Bp, Bs)
            gmix, gffn = norm_mix_even[i][None, :], norm_ffn_even[i][None, :]
            wa = HA * 2 * dka
            h = _modulate(xp, gmix, sh1, sc1)
            z = _mm_plain(h, w_in_even, i, F32)
            kp_l.append(z[:, :, wa:2 * wa].reshape(Bp, Lp, HA, 2, dka))
            vp_l.append(z[:, :, 2 * wa:3 * wa].reshape(Bp, Lp, HA, 2 * dka))
            o = _attn_prompt(z, lambda_qk, row3(subln_a), i, HA, D, lam_init)
            o, s_fin = _ret_prompt(o, z, cos_p, sin_p, log_gamma, row3(gn_b), i, HA, HB)
            rp_l.append(s_fin)
            xp = _mm_residual(o, w_out_even, i, xp, g1)
            h = _modulate(xp, gffn, sh2, sc2)
            hh = _mm_swiglu(h, ffn_w1, ffn_w3, i)
            xp = _mm_residual(hh, ffn_w2, i, xp, g2)
            h = _modulate(xs, gmix, sh1s, sc1s)
            zs = _mm_plain(h, w_in_even, i, F32)
            zs3 = zs.reshape(Bs, 1, zs.shape[-1])
            ks_l.append(zs3[:, :, wa:2 * wa].reshape(Bs, Ls, HA, 2, dka))
            v_new = zs3[:, :, 2 * wa:3 * wa].reshape(Bs, HA, 2 * dka)
            vs_l.append(v_new.reshape(Bs, Ls, HA, 2 * dka))
            oa = _attn_sample(zs3, v_new, cache_kt, cache_v2, page_table, lambda_qk, subln3, i, HA, lam_init)
            o_s = jnp.concatenate([oa.reshape(Bs, 1, wa), jnp.zeros((Bs, 1, D - wa), oa.dtype)], axis=-1)
            o_s, s_new = _ret_sample(o_s, zs3, cos_s, sin_s, log_gamma, row3(gn_b), state_ret2, i, HA, HB)
            rs_l.append(s_new)
            xs = _mm_residual(o_s.reshape(1, Bs, D), w_out_even, i, xs, g1s)
            h = _modulate(xs, gffn, sh2s, sc2s)
            hh = _mm_swiglu(h, ffn_w1, ffn_w3, i)
            xs = _mm_residual(hh, ffn_w2, i, xs, g2s)
        else:
            m = _adaln(c_all, ada_w_odd, row3(ada_b_odd), i)
            (sh1, sc1, g1, sh2, sc2, g2), (sh1s, sc1s, g1s, sh2s, sc2s, g2s) = _split_mods(m, Bp, Bs)
            gmix, gffn = norm_mix_odd[i][None, :], norm_ffn_odd[i][None, :]
            lru_args = (conv_w, row3(conv_b), w_ga2, w_gx2, row3(b_gate_a), row3(b_gate_x), row3(lru_lambda))
            h = _modulate(xp, gmix, sh1, sc1)
            z = _mm_plain(h, w_in_odd, i, F32)
            tail0 = jnp.zeros((Bp, V7X_SUBLANES, W), F32)
            h0 = jnp.zeros((Bp, 1, W), F32)
            y, h_last = _lru_prompt(z, *lru_args, tail0, h0, i)
            cp_l.append(z[:, Lp - (CW - 1):, W:])
            lp_l.append(h_last.reshape(Bp, W))
            xp = _mm_residual(y, w_out_odd, i, xp, g1)
            h = _modulate(xs, gmix, sh1s, sc1s)
            zs = _mm_plain(h, w_in_odd, i, F32).reshape(Bs, 2 * W)
            buf = state_conv[i]
            ys, h_new = _lru_sample(zs, *lru_args, jnp.swapaxes(buf, 0, 1), state_lru[i], i)
            cs_l.append(jnp.concatenate([buf[:, 1:], zs[:, None, W:]], axis=1))
            ls_l.append(h_new)
            xs = _mm_residual(ys.reshape(1, Bs, W), w_out_odd, i, xs, g1s)
            Tp = Bp * Lp
            n_tok = Tp + Bs
            tabs = _modulate_route(xp, gffn, sh2, sc2, router_pad[i], E, n_tok, 0)
            table, route = _modulate_route(xs, gffn, sh2s, sc2s, router_pad[i], E, n_tok, Tp, shared=tabs)
            y_tok = _moe(table, route, moe_w1f, moe_w3f, moe_w2f, i, E)
            xp = _residual(xp, y_tok, 0, g2, nf if last else None)
            xs = _residual(xs, y_tok, Tp, g2s, nf if last else None)
            normed = last
    if not normed:
        xp = _final_norm(xp, nf)
        xs = _final_norm(xs, nf)
    y_prompt = xp.astype(x_prompt.dtype)
    y_sample = xs.reshape(Bs, Ls, D).astype(x_sample.dtype)
    return (y_prompt, y_sample,
            jnp.stack(kp_l), jnp.stack(vp_l), jnp.stack(rp_l), jnp.stack(cp_l), jnp.stack(lp_l),
            jnp.stack(ks_l), jnp.stack(vs_l), jnp.stack(rs_l), jnp.stack(cs_l), jnp.stack(ls_l))
```

```python
import functools
import math

import jax
import jax.numpy as jnp
from jax import lax
from jax.experimental import pallas as pl
from jax.experimental.pallas import tpu as pltpu

F32 = jnp.float32
BF16 = jnp.bfloat16

EPS = 1e-6
ROPE_BASE = 10000.0
LRU_C = 8.0
TOP_K = 2
NEG_BIG = -1e30

V7X_LANES = 128
V7X_SUBLANES = 8
V7X_VMEM_LIMIT_BYTES = 56 * 1024 * 1024


def _cparams(n_axes):
    return pltpu.CompilerParams(dimension_semantics=("arbitrary",) * n_axes,
                                vmem_limit_bytes=V7X_VMEM_LIMIT_BYTES)


def _tile(n, pref, align):
    if n <= pref:
        return n
    t = (pref // align) * align
    while t > align and n % t:
        t -= align
    assert n % t == 0, (n, pref, align)
    return t


def _iota(shape, dim):
    return lax.broadcasted_iota(jnp.int32, shape, dim)


def _adaln_kernel(c_ref, w_ref, b_ref, o_ref):
    c = c_ref[...]
    s = (c * jax.nn.sigmoid(c)).astype(BF16)
    o_ref[...] = jnp.dot(s, w_ref[...].astype(BF16), preferred_element_type=F32) + b_ref[...]


def _adaln(c_all, w, b, li):
    R, D = c_all.shape
    N = w.shape[-1]
    tn = _tile(N, 1024, V7X_LANES)
    return pl.pallas_call(
        _adaln_kernel,
        grid=(N // tn,),
        in_specs=[pl.BlockSpec((R, D), lambda j: (0, 0)),
                  pl.BlockSpec((None, D, tn), lambda j: (li, 0, j)),
                  pl.BlockSpec((None, 1, tn), lambda j: (li, 0, j))],
        out_specs=pl.BlockSpec((R, tn), lambda j: (0, j)),
        out_shape=jax.ShapeDtypeStruct((R, N), F32),
        compiler_params=_cparams(1),
        name="adaln",
    )(c_all, w, b)


def _modulated(x_ref, g_ref, sh_ref, sc_ref):
    x = x_ref[...]
    ms = jnp.mean(x * x, axis=-1, keepdims=True)
    return x * lax.rsqrt(ms + EPS) * g_ref[...] * (1.0 + sc_ref[...]) + sh_ref[...]


def _modulate_kernel(x_ref, g_ref, sh_ref, sc_ref, o_ref):
    o_ref[...] = _modulated(x_ref, g_ref, sh_ref, sc_ref).astype(o_ref.dtype)


def _top2_route(h, wr_ref, n_experts):
    w = wr_ref[...]
    h_hi, w_hi = h.astype(BF16), w.astype(BF16)
    h_lo = (h - h_hi.astype(F32)).astype(BF16)
    w_lo = (w - w_hi.astype(F32)).astype(BF16)
    logits = (jnp.dot(h_hi, w_hi, preferred_element_type=F32) + jnp.dot(h_hi, w_lo, preferred_element_type=F32)
              + jnp.dot(h_lo, w_hi, preferred_element_type=F32))
    lane = _iota(logits.shape, 1)
    logits = jnp.where(lane < n_experts, logits, NEG_BIG)
    m1 = jnp.max(logits, axis=-1, keepdims=True)
    i1 = jnp.min(jnp.where(logits == m1, lane, V7X_LANES), axis=-1, keepdims=True)
    rest = jnp.where(lane == i1, NEG_BIG, logits)
    m2 = jnp.max(rest, axis=-1, keepdims=True)
    i2 = jnp.min(jnp.where(rest == m2, lane, V7X_LANES), axis=-1, keepdims=True)
    e2 = jnp.exp(m2 - m1)
    g1 = 1.0 / (1.0 + e2)
    g2 = e2 / (1.0 + e2)
    return jnp.where(lane == 0, i1.astype(F32),
                     jnp.where(lane == 1, i2.astype(F32),
                               jnp.where(lane == 2, g1, jnp.where(lane == 3, g2, 0.0))))


def _modulate_route_kernel(x_ref, g_ref, sh_ref, sc_ref, xs_ref, shs_ref, scs_ref, wr_ref, o_ref, r_ref,
                           *, n_experts, n_prompt_steps):
    t = pl.program_id(0)

    @pl.when(t < n_prompt_steps)
    def _():
        h = _modulated(x_ref, g_ref, sh_ref, sc_ref)
        o_ref[...] = h
        r_ref[...] = _top2_route(h, wr_ref, n_experts)

    @pl.when(t == n_prompt_steps)
    def _():
        h = _modulated(xs_ref, g_ref, shs_ref, scs_ref)
        n = h.shape[0]
        o_ref[:n, :] = h
        r_ref[:n, :] = _top2_route(h, wr_ref, n_experts)


def _mod_specs(B, L, D, R, tm):
    xmap = lambda b, i: (b, i, 0)
    return [pl.BlockSpec((None, tm, D), xmap),
            pl.BlockSpec((1, D), lambda b, i: (0, 0)),
            pl.BlockSpec((None, R, D), lambda b, i: (b, 0, 0)),
            pl.BlockSpec((None, R, D), lambda b, i: (b, 0, 0))]


def _modulate(x, g, shift, scale):
    B, L, D = x.shape
    R = shift.shape[1]
    tm = L if R == L else _tile(L, 512, V7X_SUBLANES)
    return pl.pallas_call(
        _modulate_kernel,
        grid=(B, L // tm),
        in_specs=_mod_specs(B, L, D, R, tm),
        out_specs=pl.BlockSpec((None, tm, D), lambda b, i: (b, i, 0)),
        out_shape=jax.ShapeDtypeStruct((B, L, D), BF16),
        compiler_params=_cparams(2),
        name="modulate",
    )(x, g, shift, scale)


def _modulate_route(x, g, shift, scale, xs, shift_s, scale_s, w_router_padded, n_experts):
    B, L, D = x.shape
    Bs = xs.shape[0]
    tm = _tile(L, 512, V7X_SUBLANES)
    assert Bs <= tm
    nb = L // tm
    n_p = B * nb
    n_tok = B * L + Bs
    bmap = lambda t: jnp.minimum(t // nb, B - 1)
    imap = lambda t: jnp.where(t < n_p, t % nb, nb - 1)
    whole = lambda t: (0, 0)
    return pl.pallas_call(
        functools.partial(_modulate_route_kernel, n_experts=n_experts, n_prompt_steps=n_p),
        grid=(n_p + 1,),
        in_specs=[pl.BlockSpec((None, tm, D), lambda t: (bmap(t), imap(t), 0)),
                  pl.BlockSpec((1, D), whole),
                  pl.BlockSpec((None, 1, D), lambda t: (bmap(t), 0, 0)),
                  pl.BlockSpec((None, 1, D), lambda t: (bmap(t), 0, 0)),
                  pl.BlockSpec((Bs, D), whole), pl.BlockSpec((Bs, D), whole), pl.BlockSpec((Bs, D), whole),
                  pl.BlockSpec((D, V7X_LANES), whole)],
        out_specs=[pl.BlockSpec((tm, D), lambda t: (t, 0)), pl.BlockSpec((tm, V7X_LANES), lambda t: (t, 0))],
        out_shape=[jax.ShapeDtypeStruct((n_tok, D), F32),
                   jax.ShapeDtypeStruct((n_tok, V7X_LANES), F32)],
        compiler_params=_cparams(1),
        name="modulate_route",
    )(x, g, shift, scale, xs, shift_s, scale_s, w_router_padded)


def _cast_weights_once(first, w_refs, wb_refs):
    @pl.when(first)
    def _():
        for w_ref, wb_ref in zip(w_refs, wb_refs):
            wb_ref[...] = w_ref[...].astype(BF16)


def _mm_plain_kernel(a_ref, w_ref, o_ref, wb_ref):
    first = jnp.logical_and(pl.program_id(1) == 0, pl.program_id(2) == 0)
    _cast_weights_once(first, [w_ref], [wb_ref])
    o_ref[...] = jnp.dot(a_ref[...], wb_ref[...], preferred_element_type=F32).astype(o_ref.dtype)


def _mm_swiglu_kernel(a_ref, w1_ref, w3_ref, o_ref, w1b_ref, w3b_ref):
    first = jnp.logical_and(pl.program_id(1) == 0, pl.program_id(2) == 0)
    _cast_weights_once(first, [w1_ref, w3_ref], [w1b_ref, w3b_ref])
    a = a_ref[...]
    u = jnp.dot(a, w1b_ref[...], preferred_element_type=F32)
    v = jnp.dot(a, w3b_ref[...], preferred_element_type=F32)
    o_ref[...] = (u * jax.nn.sigmoid(u) * v).astype(o_ref.dtype)


def _mm_residual_kernel(a_ref, w_ref, x_ref, g_ref, o_ref, wb_ref):
    first = jnp.logical_and(pl.program_id(1) == 0, pl.program_id(2) == 0)
    _cast_weights_once(first, [w_ref], [wb_ref])
    y = jnp.dot(a_ref[...], wb_ref[...], preferred_element_type=F32)
    o_ref[...] = x_ref[...] + g_ref[...] * y


def _mm_residual2_kernel(a1_ref, a2_ref, w_ref, x_ref, g_ref, o_ref, wb_ref):
    first = jnp.logical_and(pl.program_id(1) == 0, pl.program_id(2) == 0)
    _cast_weights_once(first, [w_ref], [wb_ref])
    k1 = a1_ref.shape[-1]
    y = jnp.dot(a1_ref[...], wb_ref[:k1, :], preferred_element_type=F32)
    y = y + jnp.dot(a2_ref[...], wb_ref[k1:, :], preferred_element_type=F32)
    o_ref[...] = x_ref[...] + g_ref[...] * y


def _mm_tiles(L, K, N, n_w):
    budget = 28 * 1024 * 1024
    tn = _tile(N, 512, V7X_LANES)
    while tn > V7X_LANES and n_w * K * tn * (2 * 4 + 2) > budget:
        tn = _tile(N, tn // 2, V7X_LANES)
    tm = _tile(L, 1024 if K <= 2048 else 512, V7X_SUBLANES)
    return tm, tn


def _mm_plain(a, w, li, out_dtype):
    B, L, K = a.shape
    N = w.shape[-1]
    tm, tn = _mm_tiles(L, K, N, 1)
    return pl.pallas_call(
        _mm_plain_kernel,
        grid=(N // tn, B, L // tm),
        in_specs=[pl.BlockSpec((None, tm, K), lambda j, b, i: (b, i, 0)),
                  pl.BlockSpec((None, K, tn), lambda j, b, i: (li, 0, j))],
        out_specs=pl.BlockSpec((None, tm, tn), lambda j, b, i: (b, i, j)),
        out_shape=jax.ShapeDtypeStruct((B, L, N), out_dtype),
        scratch_shapes=[pltpu.VMEM((K, tn), BF16)],
        compiler_params=_cparams(3),
        name="mm_plain",
    )(a, w)


def _mm_swiglu(a, w1, w3, li):
    B, L, K = a.shape
    N = w1.shape[-1]
    tm, tn = _mm_tiles(L, K, N, 2)
    wspec = pl.BlockSpec((None, K, tn), lambda j, b, i: (li, 0, j))
    return pl.pallas_call(
        _mm_swiglu_kernel,
        grid=(N // tn, B, L // tm),
        in_specs=[pl.BlockSpec((None, tm, K), lambda j, b, i: (b, i, 0)), wspec, wspec],
        out_specs=pl.BlockSpec((None, tm, tn), lambda j, b, i: (b, i, j)),
        out_shape=jax.ShapeDtypeStruct((B, L, N), BF16),
        scratch_shapes=[pltpu.VMEM((K, tn), BF16), pltpu.VMEM((K, tn), BF16)],
        compiler_params=_cparams(3),
        name="mm_swiglu",
    )(a, w1, w3)


def _mm_residual(a, w, li, x, gate):
    B, L, K = a.shape
    N = w.shape[-1]
    R = gate.shape[1]
    tm, tn = _mm_tiles(L, K, N, 1)
    if R == L:
        tm = L
    return pl.pallas_call(
        _mm_residual_kernel,
        grid=(N // tn, B, L // tm),
        in_specs=[pl.BlockSpec((None, tm, K), lambda j, b, i: (b, i, 0)),
                  pl.BlockSpec((None, K, tn), lambda j, b, i: (li, 0, j)),
                  pl.BlockSpec((None, tm, tn), lambda j, b, i: (b, i, j)),
                  pl.BlockSpec((None, R, tn), lambda j, b, i: (b, 0, j))],
        out_specs=pl.BlockSpec((None, tm, tn), lambda j, b, i: (b, i, j)),
        out_shape=jax.ShapeDtypeStruct((B, L, N), F32),
        scratch_shapes=[pltpu.VMEM((K, tn), BF16)],
        compiler_params=_cparams(3),
        name="mm_residual",
    )(a, w, x, gate)


def _mm_residual2(a1, a2, w, li, x, gate):
    B, L, K1 = a1.shape
    K2 = a2.shape[-1]
    K = K1 + K2
    N = w.shape[-1]
    R = gate.shape[1]
    tm, tn = _mm_tiles(L, K, N, 1)
    if R == L:
        tm = L
    return pl.pallas_call(
        _mm_residual2_kernel,
        grid=(N // tn, B, L // tm),
        in_specs=[pl.BlockSpec((None, tm, K1), lambda j, b, i: (b, i, 0)),
                  pl.BlockSpec((None, tm, K2), lambda j, b, i: (b, i, 0)),
                  pl.BlockSpec((None, K, tn), lambda j, b, i: (li, 0, j)),
                  pl.BlockSpec((None, tm, tn), lambda j, b, i: (b, i, j)),
                  pl.BlockSpec((None, R, tn), lambda j, b, i: (b, 0, j))],
        out_specs=pl.BlockSpec((None, tm, tn), lambda j, b, i: (b, i, j)),
        out_shape=jax.ShapeDtypeStruct((B, L, N), F32),
        scratch_shapes=[pltpu.VMEM((K, tn), BF16)],
        compiler_params=_cparams(3),
        name="mm_residual2",
    )(a1, a2, w, x, gate)


def _diff_lambda(lq_ref, lam_init):
    lq = lq_ref[...]
    a = jnp.sum(lq[0:1] * lq[1:2], axis=-1, keepdims=True)
    b = jnp.sum(lq[2:3] * lq[3:4], axis=-1, keepdims=True)
    return jnp.exp(a) - jnp.exp(b) + lam_init


def _attn_prompt_kernel(q_ref, k_ref, v_ref, lq_ref, sub_ref, o_ref, *, tq, dk, lam_init):
    qi = pl.program_id(2)
    sc = dk ** -0.5
    q = q_ref[...] * sc
    lane = _iota(q.shape, 1)
    qq = jnp.concatenate([jnp.where(lane < dk, q, 0.0), jnp.where(lane >= dk, q, 0.0)], axis=0).astype(BF16)

    def block(j, carry, masked):
        m, l, acc = carry
        start = pl.multiple_of(j * tq, tq)
        kb = k_ref[pl.ds(start, tq), :].astype(BF16)
        vb = v_ref[pl.ds(start, tq), :].astype(BF16)
        s = lax.dot_general(qq, kb, (((1,), (1,)), ((), ())), preferred_element_type=F32)
        if masked:
            row = _iota(s.shape, 0)
            row = jnp.where(row >= tq, row - tq, row)
            s = jnp.where(_iota(s.shape, 1) <= row, s, NEG_BIG)
        m_new = jnp.maximum(m, jnp.max(s, axis=-1, keepdims=True))
        alpha = jnp.exp(m - m_new)
        p = jnp.exp(s - m_new)
        l = alpha * l + jnp.sum(p, axis=-1, keepdims=True)
        acc = alpha * acc + jnp.dot(p.astype(BF16), vb, preferred_element_type=F32)
        return m_new, l, acc

    init = (jnp.full((2 * tq, 1), NEG_BIG, F32), jnp.zeros((2 * tq, 1), F32),
            jnp.zeros((2 * tq, 2 * dk), F32))
    carry = lax.fori_loop(0, qi, lambda j, c: block(j, c, False), init)
    m, l, acc = block(qi, carry, True)
    on = acc / l
    lam = _diff_lambda(lq_ref, lam_init)
    o = on[:tq] - lam * on[tq:]
    ms = jnp.mean(o * o, axis=-1, keepdims=True)
    o_ref[...] = (o * lax.rsqrt(ms + EPS) * sub_ref[...] * (1.0 - lam_init)).astype(o_ref.dtype)


def _attn_prompt(z, lambda_qk, subln, li, n_heads, lam_init):
    B, L, _ = z.shape
    hw = 2 * lambda_qk.shape[-1]
    tq = _tile(L, 512, V7X_SUBLANES)
    H = n_heads
    return pl.pallas_call(
        functools.partial(_attn_prompt_kernel, tq=tq, dk=hw // 2, lam_init=lam_init),
        grid=(B, H, L // tq),
        in_specs=[pl.BlockSpec((None, tq, hw), lambda b, h, i: (b, i, h)),
                  pl.BlockSpec((None, L, hw), lambda b, h, i: (b, 0, H + h)),
                  pl.BlockSpec((None, L, hw), lambda b, h, i: (b, 0, 2 * H + h)),
                  pl.BlockSpec((None,) + lambda_qk.shape[1:], lambda b, h, i: (li, 0, 0)),
                  pl.BlockSpec((None, 1, hw), lambda b, h, i: (li, 0, h))],
        out_specs=pl.BlockSpec((None, tq, hw), lambda b, h, i: (b, i, h)),
        out_shape=jax.ShapeDtypeStruct((B, L, H * hw), BF16),
        compiler_params=_cparams(3),
        name="attn_prompt",
    )(z, z, z, lambda_qk, subln)


def _attn_sample_kernel(pt_ref, q_ref, kn_ref, vn_ref, lq_ref, sub_ref, *rest, n_pages, n_heads, dk, lam_init):
    kt_refs = rest[:n_pages]
    v_refs = rest[n_pages:2 * n_pages]
    o_ref, m_ref, l_ref, acc_ref = rest[2 * n_pages:]
    s_id = pl.program_id(1)
    H = n_heads
    hw = 2 * dk
    W = H * hw
    page = kt_refs[0].shape[-1]
    sc = dk ** -0.5

    @pl.when(s_id == 0)
    def _():
        m_ref[...] = jnp.full(m_ref.shape, NEG_BIG, F32)
        l_ref[...] = jnp.zeros(l_ref.shape, F32)
        acc_ref[...] = jnp.zeros(acc_ref.shape, F32)

    row = _iota((2 * H, W), 0)
    lane = _iota((2 * H, W), 1)
    r_head = jnp.where(row >= H, row - H, row)
    r_comp = jnp.where(row >= H, 1, 0)
    qsel = jnp.logical_and(lane // hw == r_head, (lane // dk) % 2 == r_comp)
    qf = jnp.where(qsel, q_ref[...] * sc, 0.0)
    qb = qf.astype(BF16)

    s_parts = [jnp.dot(qb, kt_refs[r][...].astype(BF16), preferred_element_type=F32) for r in range(n_pages)]
    s = jnp.concatenate(s_parts, axis=-1) if n_pages > 1 else s_parts[0]
    m = m_ref[...]
    m_new = jnp.maximum(m, jnp.max(s, axis=-1, keepdims=True))
    alpha = jnp.exp(m - m_new)
    p = jnp.exp(s - m_new)
    l_ref[...] = alpha * l_ref[...] + jnp.sum(p, axis=-1, keepdims=True)
    m_ref[...] = m_new
    p_head = r_head[:, :page]
    pv = jnp.zeros((2 * H, hw), F32)
    for r in range(n_pages):
        pr = p[:, r * page:(r + 1) * page]
        pexp = jnp.concatenate([jnp.where(p_head == h, pr, 0.0) for h in range(H)], axis=1).astype(BF16)
        vh = jnp.concatenate([v_refs[r][pl.ds(h, page, stride=H), :] for h in range(H)], axis=0).astype(BF16)
        pv = pv + jnp.dot(pexp, vh, preferred_element_type=F32)
    acc_ref[...] = alpha * acc_ref[...] + pv

    @pl.when(s_id == pl.num_programs(1) - 1)
    def _():
        s_new = jnp.sum(qf * kn_ref[...], axis=-1, keepdims=True)
        m_old = m_ref[...]
        m_fin = jnp.maximum(m_old, s_new)
        a_fin = jnp.exp(m_old - m_fin)
        p_new = jnp.exp(s_new - m_fin)
        l_fin = a_fin * l_ref[...] + p_new
        vn = vn_ref[...]
        acc = a_fin * acc_ref[...] + p_new * jnp.concatenate([vn, vn], axis=0)
        on = acc / l_fin
        lam = _diff_lambda(lq_ref, lam_init)
        o = on[:H] - lam * on[H:]
        ms = jnp.mean(o * o, axis=-1, keepdims=True)
        o_ref[...] = (o * lax.rsqrt(ms + EPS) * sub_ref[...] * (1.0 - lam_init)).astype(o_ref.dtype)


def _attn_sample(z, v_new, cache_kt, cache_v, page_table, lambda_qk, subln, li, n_heads, lam_init):
    Bs = z.shape[0]
    n_phys_total, W, page = cache_kt.shape
    n_pg = page_table.shape[1]
    P = next(c for c in (8, 4, 2, 1) if n_pg % c == 0)
    n_phys = n_phys_total // lambda_qk.shape[0]
    base = li * n_phys
    H = n_heads
    dk = lambda_qk.shape[-1]
    hw = 2 * dk

    def page_spec(shape, r):
        return pl.BlockSpec((None,) + shape, lambda b, s, pt: (base + pt[b, s * P + r], 0, 0))

    zrow = lambda c: pl.BlockSpec((None, 1, W), lambda b, s, pt: (b, 0, c))
    grid_spec = pltpu.PrefetchScalarGridSpec(
        num_scalar_prefetch=1,
        grid=(Bs, n_pg // P),
        in_specs=[zrow(0), zrow(1),
                  pl.BlockSpec((None, H, hw), lambda b, s, pt: (b, 0, 0)),
                  pl.BlockSpec((None,) + lambda_qk.shape[1:], lambda b, s, pt: (li, 0, 0)),
                  pl.BlockSpec((None, H, hw), lambda b, s, pt: (li, 0, 0))]
                 + [page_spec((W, page), r) for r in range(P)]
                 + [page_spec((page * H, hw), r) for r in range(P)],
        out_specs=pl.BlockSpec((None, H, hw), lambda b, s, pt: (b, 0, 0)),
        scratch_shapes=[pltpu.VMEM((2 * H, 1), F32), pltpu.VMEM((2 * H, 1), F32),
                        pltpu.VMEM((2 * H, hw), F32)],
    )
    return pl.pallas_call(
        functools.partial(_attn_sample_kernel, n_pages=P, n_heads=H, dk=dk, lam_init=lam_init),
        grid_spec=grid_spec,
        out_shape=jax.ShapeDtypeStruct((Bs, H, hw), BF16),
        compiler_params=_cparams(2),
        name="attn_sample",
    )(page_table, z, z, v_new, lambda_qk, subln, *([cache_kt] * P), *([cache_v] * P))


def _rope(x, cos, sin):
    half = x.shape[-1] // 2
    x1, x2 = x[:, :half], x[:, half:]
    return jnp.concatenate([x1 * cos - x2 * sin, x1 * sin + x2 * cos], axis=-1)


def _groupnorm_gate(o, gn, g):
    mu = jnp.mean(o, axis=-1, keepdims=True)
    d = o - mu
    var = jnp.mean(d * d, axis=-1, keepdims=True)
    return d * lax.rsqrt(var + EPS) * gn * (g * jax.nn.sigmoid(g))


def _ret_prompt_kernel(q_ref, k_ref, v_ref, g_ref, cos_ref, sin_ref, lg_ref, gn_ref,
                       o_ref, sfin_ref, s_ref, *, chunk, dk):
    c = pl.program_id(2)

    @pl.when(c == 0)
    def _():
        s_ref[...] = jnp.zeros(s_ref.shape, F32)

    cos, sin = cos_ref[...], sin_ref[...]
    qr = _rope(q_ref[...], cos, sin) * dk ** -0.5
    kr = _rope(k_ref[...], cos, sin)
    vb = v_ref[...].astype(BF16)
    lg = lg_ref[...][:, :1]
    C = chunk
    d = _iota((C, C), 0) - _iota((C, C), 1)
    decay = jnp.where(d >= 0, jnp.exp(jnp.maximum(d, 0).astype(F32) * lg), 0.0)
    qrb = qr.astype(BF16)
    scores = lax.dot_general(qrb, kr.astype(BF16), (((1,), (1,)), ((), ())), preferred_element_type=F32) * decay
    intra = jnp.dot(scores.astype(BF16), vb, preferred_element_type=F32)
    t = _iota((C, 1), 0).astype(F32)
    s_old = s_ref[...]
    inter = jnp.dot(qrb, s_old.astype(BF16), preferred_element_type=F32) * jnp.exp((t + 1.0) * lg)
    kd = kr * jnp.exp((C - 1.0 - t) * lg)
    s_new = jnp.exp(C * lg) * s_old + jnp.dot(kd.T.astype(BF16), vb, preferred_element_type=F32)
    s_ref[...] = s_new
    sfin_ref[...] = s_new
    o_ref[...] = _groupnorm_gate(intra + inter, gn_ref[...], g_ref[...]).astype(o_ref.dtype)


def _ret_prompt(z, cos_tab, sin_tab, log_gamma, gn_b, li, n_heads_a, n_heads_b):
    B, L, _ = z.shape
    dk = 2 * cos_tab.shape[-1]
    HB = n_heads_b
    assert (3 * n_heads_a * V7X_LANES) % dk == 0
    off = 3 * n_heads_a * V7X_LANES // dk
    C = _tile(L, 256, V7X_SUBLANES)
    zb = lambda g: pl.BlockSpec((None, C, dk), lambda b, h, c: (b, c, off + g * HB + h))
    return pl.pallas_call(
        functools.partial(_ret_prompt_kernel, chunk=C, dk=dk),
        grid=(B, HB, L // C),
        in_specs=[zb(0), zb(1), zb(2), zb(3),
                  pl.BlockSpec((C, dk // 2), lambda b, h, c: (c, 0)),
                  pl.BlockSpec((C, dk // 2), lambda b, h, c: (c, 0)),
                  pl.BlockSpec((None, 1, V7X_LANES), lambda b, h, c: (h, 0, 0)),
                  pl.BlockSpec((None, 1, dk), lambda b, h, c: (li, 0, h))],
        out_specs=[pl.BlockSpec((None, C, dk), lambda b, h, c: (b, c, h)),
                   pl.BlockSpec((None, None, dk, dk), lambda b, h, c: (b, h, 0, 0))],
        out_shape=[jax.ShapeDtypeStruct((B, L, HB * dk), BF16),
                   jax.ShapeDtypeStruct((B, HB, dk, dk), F32)],
        scratch_shapes=[pltpu.VMEM((dk, dk), F32)],
        compiler_params=_cparams(3),
        name="ret_prompt",
    )(z, z, z, z, cos_tab, sin_tab, log_gamma, gn_b)


def _ret_sample_kernel(q_ref, k_ref, v_ref, g_ref, cos_ref, sin_ref, lg_ref, gn_ref, s0_ref,
                       o_ref, snew_ref, *, dk):
    cos, sin = cos_ref[...], sin_ref[...]
    qr = _rope(q_ref[...], cos, sin) * dk ** -0.5
    kr = _rope(k_ref[...], cos, sin)
    v = v_ref[...]
    gamma = jnp.exp(lg_ref[...][:, :1])
    eye = _iota((dk, dk), 0) == _iota((dk, dk), 1)
    kcol = jnp.sum(jnp.where(eye, kr, 0.0), axis=-1, keepdims=True)
    qcol = jnp.sum(jnp.where(eye, qr, 0.0), axis=-1, keepdims=True)
    s0 = s0_ref[...]
    snew_ref[...] = gamma * s0 + kcol * v
    inter = gamma * jnp.sum(qcol * s0, axis=0, keepdims=True)
    intra = jnp.sum(qr * kr, axis=-1, keepdims=True) * v
    o_ref[...] = _groupnorm_gate(intra + inter, gn_ref[...], g_ref[...]).astype(o_ref.dtype)


def _ret_sample(z, cos_row, sin_row, log_gamma, gn_b, state, li, n_heads_a, n_heads_b):
    Bs = z.shape[0]
    dk = 2 * cos_row.shape[-1]
    HB = n_heads_b
    off = 3 * n_heads_a * V7X_LANES // dk
    zb = lambda g: pl.BlockSpec((None, 1, dk), lambda b, h: (b, 0, off + g * HB + h))
    return pl.pallas_call(
        functools.partial(_ret_sample_kernel, dk=dk),
        grid=(Bs, HB),
        in_specs=[zb(0), zb(1), zb(2), zb(3),
                  pl.BlockSpec((1, dk // 2), lambda b, h: (0, 0)),
                  pl.BlockSpec((1, dk // 2), lambda b, h: (0, 0)),
                  pl.BlockSpec((None, 1, V7X_LANES), lambda b, h: (h, 0, 0)),
                  pl.BlockSpec((None, 1, dk), lambda b, h: (li, 0, h)),
                  pl.BlockSpec((None, None, dk, dk), lambda b, h: (li * Bs + b, h, 0, 0))],
        out_specs=[pl.BlockSpec((None, 1, dk), lambda b, h: (b, 0, h)),
                   pl.BlockSpec((None, None, dk, dk), lambda b, h: (b, h, 0, 0))],
        out_shape=[jax.ShapeDtypeStruct((Bs, 1, HB * dk), BF16),
                   jax.ShapeDtypeStruct((Bs, HB, dk, dk), F32)],
        compiler_params=_cparams(2),
        name="ret_sample",
    )(z, z, z, z, cos_row, sin_row, log_gamma, gn_b, state)


def _softplus(z):
    return jnp.maximum(z, 0.0) + jnp.log1p(jnp.exp(-jnp.abs(z)))


def _lru_coeffs(xc, wa_ref, wx_ref, ba_ref, bx_ref, lam_ref):
    xcb = xc.astype(BF16)
    r = jax.nn.sigmoid(jnp.dot(xcb, wa_ref[...].astype(BF16), preferred_element_type=F32) + ba_ref[...])
    gi = jax.nn.sigmoid(jnp.dot(xcb, wx_ref[...].astype(BF16), preferred_element_type=F32) + bx_ref[...])
    log_a = -LRU_C * r * _softplus(-lam_ref[...])
    a = jnp.exp(log_a)
    t = jnp.tanh(log_a)
    mult = jnp.sqrt(jnp.maximum(-2.0 * t / (1.0 - t), 0.0))
    return a, mult * gi * xc


def _lru_prompt_kernel(gate_ref, x_ref, cw_ref, cb_ref, wa_ref, wx_ref, ba_ref, bx_ref, lam_ref,
                       tail0_ref, h0_ref, y_ref, hl_ref, tail_ref, hc_ref, *, tt, conv_w):
    t_id = pl.program_id(2)

    @pl.when(t_id == 0)
    def _():
        tail_ref[...] = tail0_ref[...]
        hc_ref[...] = h0_ref[...]

    x = x_ref[...]
    tl = tail_ref[...]
    row8 = _iota(tl.shape, 0)
    xc = cb_ref[...] + cw_ref[conv_w - 1:conv_w, :] * x
    for s in range(1, conv_w):
        xs = pltpu.roll(x, s, 0)
        head = jnp.where(row8 >= s, xs[:V7X_SUBLANES], pltpu.roll(tl, s, 0))
        xs = jnp.concatenate([head, xs[V7X_SUBLANES:]], axis=0) if tt > V7X_SUBLANES else head
        xc = xc + cw_ref[conv_w - 1 - s:conv_w - s, :] * xs
    tail_ref[...] = x[tt - V7X_SUBLANES:, :]

    a, b = _lru_coeffs(xc, wa_ref, wx_ref, ba_ref, bx_ref, lam_ref)
    row = _iota(a.shape, 0)
    s = 1
    while s < tt:
        if s % V7X_SUBLANES == 0:
            a_s = jnp.concatenate([jnp.ones((s, a.shape[1]), F32), a[:tt - s]], axis=0)
            b_s = jnp.concatenate([jnp.zeros((s, a.shape[1]), F32), b[:tt - s]], axis=0)
        else:
            keep = row >= s
            a_s = jnp.where(keep, pltpu.roll(a, s, 0), 1.0)
            b_s = jnp.where(keep, pltpu.roll(b, s, 0), 0.0)
        b = b + a * b_s
        a = a * a_s
        s *= 2
    h = b + a * hc_ref[...]
    hc_ref[...] = h[tt - 1:tt, :]
    hl_ref[...] = h[tt - 1:tt, :]
    y_ref[...] = (jax.nn.gelu(gate_ref[...], approximate=True) * h).astype(y_ref.dtype)


def _lru_prompt(z, conv_w, conv_b, w_ga, w_gx, b_ga, b_gx, lam, tail0, h0, li):
    B, L, W2 = z.shape
    W = W2 // 2
    bw = w_ga.shape[-1]
    NB = W // bw
    CW = conv_w.shape[1]
    tt = _tile(L, 256, V7X_SUBLANES)
    vec = lambda: pl.BlockSpec((None, 1, bw), lambda b, n, t: (li, 0, n))
    gw = lambda: pl.BlockSpec((None, bw, bw), lambda b, n, t: (li * NB + n, 0, 0))
    return pl.pallas_call(
        functools.partial(_lru_prompt_kernel, tt=tt, conv_w=CW),
        grid=(B, NB, L // tt),
        in_specs=[pl.BlockSpec((None, tt, bw), lambda b, n, t: (b, t, n)),
                  pl.BlockSpec((None, tt, bw), lambda b, n, t: (b, t, NB + n)),
                  pl.BlockSpec((None, CW, bw), lambda b, n, t: (li, 0, n)),
                  vec(), gw(), gw(), vec(), vec(), vec(),
                  pl.BlockSpec((None, V7X_SUBLANES, bw), lambda b, n, t: (b, 0, n)),
                  pl.BlockSpec((None, 1, bw), lambda b, n, t: (b, 0, n))],
        out_specs=[pl.BlockSpec((None, tt, bw), lambda b, n, t: (b, t, n)),
                   pl.BlockSpec((None, 1, bw), lambda b, n, t: (b, 0, n))],
        out_shape=[jax.ShapeDtypeStruct((B, L, W), BF16), jax.ShapeDtypeStruct((B, 1, W), F32)],
        scratch_shapes=[pltpu.VMEM((V7X_SUBLANES, bw), F32), pltpu.VMEM((1, bw), F32)],
        compiler_params=_cparams(3),
        name="lru_prompt",
    )(z, z, conv_w, conv_b, w_ga, w_gx, b_ga, b_gx, lam, tail0, h0)


def _lru_sample_kernel(gate_ref, x_ref, cw_ref, cb_ref, wa_ref, wx_ref, ba_ref, bx_ref, lam_ref,
                       buf_ref, h0_ref, y_ref, h_ref, *, conv_w):
    x = x_ref[...]
    xc = cb_ref[...] + cw_ref[conv_w - 1:conv_w, :] * x
    for j in range(conv_w - 1):
        xc = xc + cw_ref[j:j + 1, :] * buf_ref[j]
    a, b = _lru_coeffs(xc, wa_ref, wx_ref, ba_ref, bx_ref, lam_ref)
    h = a * h0_ref[...] + b
    h_ref[...] = h
    y_ref[...] = (jax.nn.gelu(gate_ref[...], approximate=True) * h).astype(y_ref.dtype)


def _lru_sample(z, conv_w, conv_b, w_ga, w_gx, b_ga, b_gx, lam, buf, h0, li):
    Bs, W2 = z.shape
    W = W2 // 2
    bw = w_ga.shape[-1]
    NB = W // bw
    CW = conv_w.shape[1]
    vec = lambda: pl.BlockSpec((None, 1, bw), lambda n: (li, 0, n))
    gw = lambda: pl.BlockSpec((None, bw, bw), lambda n: (li * NB + n, 0, 0))
    return pl.pallas_call(
        functools.partial(_lru_sample_kernel, conv_w=CW),
        grid=(NB,),
        in_specs=[pl.BlockSpec((Bs, bw), lambda n: (0, n)),
                  pl.BlockSpec((Bs, bw), lambda n: (0, NB + n)),
                  pl.BlockSpec((None, CW, bw), lambda n: (li, 0, n)),
                  vec(), gw(), gw(), vec(), vec(), vec(),
                  pl.BlockSpec((CW - 1, Bs, bw), lambda n: (0, 0, n)),
                  pl.BlockSpec((Bs, bw), lambda n: (0, n))],
        out_specs=[pl.BlockSpec((Bs, bw), lambda n: (0, n)),
                   pl.BlockSpec((Bs, bw), lambda n: (0, n))],
        out_shape=[jax.ShapeDtypeStruct((Bs, W), BF16), jax.ShapeDtypeStruct((Bs, W), F32)],
        compiler_params=_cparams(1),
        name="lru_sample",
    )(z, z, conv_w, conv_b, w_ga, w_gx, b_ga, b_gx, lam, buf, h0)


MOE_ROW_BLOCK = 512
MOE_SUB_BLOCK = 128


def _valid_rows_only(n_valid, o_ref, compute):
    tm = o_ref.shape[0]

    @pl.when(n_valid == tm)
    def _():
        o_ref[...] = compute(slice(None))

    for s in range(tm // MOE_SUB_BLOCK):
        rows = slice(s * MOE_SUB_BLOCK, (s + 1) * MOE_SUB_BLOCK)

        @pl.when(jnp.logical_and(n_valid < tm, n_valid > s * MOE_SUB_BLOCK))
        def _(rows=rows):
            o_ref[rows, :] = compute(rows)

        @pl.when(n_valid <= s * MOE_SUB_BLOCK)
        def _(rows=rows):
            o_ref[rows, :] = jnp.zeros((MOE_SUB_BLOCK, o_ref.shape[1]), o_ref.dtype)


def _moe_up_kernel(be_ref, bf_ref, bv_ref, x_ref, w1_ref, w3_ref, o_ref, w1b_ref, w3b_ref):
    rb = pl.program_id(1)
    _cast_weights_once(bf_ref[rb] == 1, [w1_ref, w3_ref], [w1b_ref, w3b_ref])

    def compute(rows):
        x = x_ref[rows, :]
        u = jnp.dot(x, w1b_ref[...], preferred_element_type=F32)
        v = jnp.dot(x, w3b_ref[...], preferred_element_type=F32)
        return (u * jax.nn.sigmoid(u) * v).astype(o_ref.dtype)

    _valid_rows_only(bv_ref[rb], o_ref, compute)


def _moe_down_kernel(be_ref, bf_ref, bv_ref, a_ref, w_ref, o_ref, wb_ref):
    rb = pl.program_id(1)
    _cast_weights_once(bf_ref[rb] == 1, [w_ref], [wb_ref])

    def compute(rows):
        return jnp.dot(a_ref[rows, :], wb_ref[...], preferred_element_type=F32)

    _valid_rows_only(bv_ref[rb], o_ref, compute)


def _moe_up(xs, w1, w3, blk_e, blk_first, blk_valid, li, n_experts):
    n_rows, D = xs.shape
    F = w1.shape[-1]
    tm = MOE_ROW_BLOCK
    tn = _tile(F, 1024, V7X_LANES)
    wspec = pl.BlockSpec((None, D, tn), lambda j, rb, be, bf, nu: (li * n_experts + be[rb], 0, j))
    grid_spec = pltpu.PrefetchScalarGridSpec(
        num_scalar_prefetch=3,
        grid=(F // tn, n_rows // tm),
        in_specs=[pl.BlockSpec((tm, D), lambda j, rb, be, bf, nu: (rb, 0)), wspec, wspec],
        out_specs=pl.BlockSpec((tm, tn), lambda j, rb, be, bf, nu: (rb, j)),
        scratch_shapes=[pltpu.VMEM((D, tn), BF16), pltpu.VMEM((D, tn), BF16)],
    )
    return pl.pallas_call(
        _moe_up_kernel,
        grid_spec=grid_spec,
        out_shape=jax.ShapeDtypeStruct((n_rows, F), BF16),
        compiler_params=_cparams(2),
        name="moe_up",
    )(blk_e, blk_first, blk_valid, xs, w1, w3)


def _moe_down(hh, w2, blk_e, blk_first, blk_valid, li, n_experts):
    n_rows, F = hh.shape
    D = w2.shape[-1]
    tm = MOE_ROW_BLOCK
    tn = _tile(D, 512, V7X_LANES)
    grid_spec = pltpu.PrefetchScalarGridSpec(
        num_scalar_prefetch=3,
        grid=(D // tn, n_rows // tm),
        in_specs=[pl.BlockSpec((tm, F), lambda j, rb, be, bf, nu: (rb, 0)),
                  pl.BlockSpec((None, F, tn), lambda j, rb, be, bf, nu: (li * n_experts + be[rb], 0, j))],
        out_specs=pl.BlockSpec((tm, tn), lambda j, rb, be, bf, nu: (rb, j)),
        scratch_shapes=[pltpu.VMEM((F, tn), BF16)],
    )
    return pl.pallas_call(
        _moe_down_kernel,
        grid_spec=grid_spec,
        out_shape=jax.ShapeDtypeStruct((n_rows, D), F32),
        compiler_params=_cparams(2),
        name="moe_down",
    )(blk_e, blk_first, blk_valid, hh, w2)


def _dispatch_kernel(src_ref, bv_ref, tab_ref, o_ref, buf_ref, *, tm):
    rb = pl.program_id(1)

    @pl.when(bv_ref[rb] > 0)
    def _():
        base = rb * tm

        def body(r, carry):
            buf_ref[pl.ds(r, 1), :] = tab_ref[pl.ds(src_ref[base + r], 1), :]
            return carry

        lax.fori_loop(0, tm, body, 0, unroll=8)
        o_ref[...] = buf_ref[...].astype(o_ref.dtype)

    @pl.when(bv_ref[rb] == 0)
    def _():
        o_ref[...] = jnp.zeros(o_ref.shape, o_ref.dtype)


def _dispatch(table, src, blk_valid):
    n_tok, D = table.shape
    n_rows = src.shape[0]
    tm = MOE_ROW_BLOCK
    tc = D
    while tc > V7X_LANES and n_tok * tc * 4 > 36 * 1024 * 1024:
        tc = _tile(D, tc // 2, V7X_LANES)
    grid_spec = pltpu.PrefetchScalarGridSpec(
        num_scalar_prefetch=2,
        grid=(D // tc, n_rows // tm),
        in_specs=[pl.BlockSpec((n_tok, tc), lambda j, rb, src, nu: (0, j), pipeline_mode=pl.Buffered(1))],
        out_specs=pl.BlockSpec((tm, tc), lambda j, rb, src, nu: (rb, j)),
        scratch_shapes=[pltpu.VMEM((tm, tc), F32)],
    )
    return pl.pallas_call(
        functools.partial(_dispatch_kernel, tm=tm),
        grid_spec=grid_spec,
        out_shape=jax.ShapeDtypeStruct((n_rows, D), BF16),
        compiler_params=_cparams(2),
        name="moe_dispatch",
    )(src, blk_valid, table)


def _combine_kernel(d_ref, tab_ref, route_ref, o_ref, a_ref, b_ref, *, tm):
    base = pl.program_id(1) * (tm * TOP_K)

    def body(t, carry):
        a_ref[pl.ds(t, 1), :] = tab_ref[pl.ds(d_ref[base + TOP_K * t], 1), :]
        b_ref[pl.ds(t, 1), :] = tab_ref[pl.ds(d_ref[base + TOP_K * t + 1], 1), :]
        return carry

    lax.fori_loop(0, tm, body, 0, unroll=8)
    rt = route_ref[...]
    o_ref[...] = rt[:, TOP_K:TOP_K + 1] * a_ref[...] + rt[:, TOP_K + 1:TOP_K + 2] * b_ref[...]


def _combine(ys, dest, route):
    assert TOP_K == 2
    n_rows, D = ys.shape
    n_tok = route.shape[0]
    tm = _tile(n_tok, 512, V7X_SUBLANES)
    tc = _tile(D, 512, V7X_LANES)
    while tc > V7X_LANES and n_rows * tc * 4 > 44 * 1024 * 1024:
        tc = _tile(D, tc // 2, V7X_LANES)
    grid_spec = pltpu.PrefetchScalarGridSpec(
        num_scalar_prefetch=1,
        grid=(D // tc, n_tok // tm),
        in_specs=[pl.BlockSpec((n_rows, tc), lambda j, i, d: (0, j), pipeline_mode=pl.Buffered(1)),
                  pl.BlockSpec((tm, V7X_LANES), lambda j, i, d: (i, 0))],
        out_specs=pl.BlockSpec((tm, tc), lambda j, i, d: (i, j)),
        scratch_shapes=[pltpu.VMEM((tm, tc), F32), pltpu.VMEM((tm, tc), F32)],
    )
    return pl.pallas_call(
        functools.partial(_combine_kernel, tm=tm),
        grid_spec=grid_spec,
        out_shape=jax.ShapeDtypeStruct((n_tok, D), F32),
        compiler_params=_cparams(2),
        name="moe_combine",
    )(dest, ys, route)


def _residual_kernel(x_ref, y_ref, g_ref, o_ref):
    o_ref[...] = x_ref[...] + g_ref[...] * y_ref[...]


def _residual_norm_kernel(x_ref, y_ref, g_ref, nf_ref, o_ref):
    x = x_ref[...] + g_ref[...] * y_ref[...]
    ms = jnp.mean(x * x, axis=-1, keepdims=True)
    o_ref[...] = x * lax.rsqrt(ms + EPS) * nf_ref[...]


def _norm_kernel(x_ref, nf_ref, o_ref):
    x = x_ref[...]
    ms = jnp.mean(x * x, axis=-1, keepdims=True)
    o_ref[...] = x * lax.rsqrt(ms + EPS) * nf_ref[...]


def _residual(x, y_tok, row0, gate, norm_g):
    B, L, D = x.shape
    R = gate.shape[1]
    tm = L if R == L else _tile(L, 512, V7X_SUBLANES)
    nb = L // tm
    assert row0 % tm == 0
    blk0 = row0 // tm
    blk = pl.BlockSpec((None, tm, D), lambda b, i: (b, i, 0))
    in_specs = [blk, pl.BlockSpec((tm, D), lambda b, i: (blk0 + b * nb + i, 0)),
                pl.BlockSpec((None, R, D), lambda b, i: (b, 0, 0))]
    args = [x, y_tok, gate]
    kern = _residual_kernel
    if norm_g is not None:
        in_specs.append(pl.BlockSpec((1, D), lambda b, i: (0, 0)))
        args.append(norm_g)
        kern = _residual_norm_kernel
    return pl.pallas_call(
        kern, grid=(B, L // tm), in_specs=in_specs, out_specs=blk,
        out_shape=jax.ShapeDtypeStruct((B, L, D), F32),
        compiler_params=_cparams(2), name="residual",
    )(*args)


def _final_norm(x, norm_g):
    B, L, D = x.shape
    tm = _tile(L, 512, V7X_SUBLANES)
    blk = pl.BlockSpec((None, tm, D), lambda b, i: (b, i, 0))
    return pl.pallas_call(
        _norm_kernel, grid=(B, L // tm),
        in_specs=[blk, pl.BlockSpec((1, D), lambda b, i: (0, 0))], out_specs=blk,
        out_shape=jax.ShapeDtypeStruct((B, L, D), F32),
        compiler_params=_cparams(2), name="final_norm",
    )(x, norm_g)


def _split_mods(m, n_p, n_s):
    D = m.shape[-1] // 6
    mp = [m[:n_p, k * D:(k + 1) * D][:, None, :] for k in range(6)]
    ms = [m[n_p:n_p + n_s, k * D:(k + 1) * D][None, :, :] for k in range(6)]
    return mp, ms


def _stack_cond(c_prompt, c_sample):
    c_all = jnp.concatenate([c_prompt, c_sample], axis=0)
    pad = (-c_all.shape[0]) % V7X_SUBLANES
    if pad:
        c_all = jnp.concatenate([c_all, jnp.zeros((pad, c_all.shape[1]), c_all.dtype)], axis=0)
    return c_all


def _moe(table, route, w1, w3, w2, li, n_experts):
    T = table.shape[0]
    E = n_experts
    tm = MOE_ROW_BLOCK
    e_flat = route[:, :TOP_K].astype(jnp.int32).reshape(-1)
    n_slots = T * TOP_K
    onehot = (e_flat[:, None] == jnp.arange(E, dtype=jnp.int32)[None, :]).astype(jnp.int32)
    csum = jnp.cumsum(onehot, axis=0)
    counts = csum[-1]
    rank = jnp.sum((csum - onehot) * onehot, axis=-1)
    padded = (counts + tm - 1) // tm * tm
    pends = jnp.cumsum(padded)
    pstarts = pends - padded
    rem = counts % tm
    rem_s = jnp.sum(onehot * rem[None, :], axis=-1)
    pos = jnp.where(rank < rem_s, rank, rank + (tm - rem_s) % tm)
    dest = (jnp.sum(onehot * pstarts[None, :], axis=-1) + pos).astype(jnp.int32)
    n_blocks = -(-n_slots // tm) + E
    n_rows = n_blocks * tm
    tok_flat = jnp.arange(n_slots, dtype=jnp.int32) // TOP_K
    src = jnp.zeros((n_rows,), jnp.int32).at[dest].set(tok_flat)
    blk_start = jnp.arange(n_blocks, dtype=jnp.int32) * tm
    blk_e = jnp.minimum(jnp.sum((blk_start[:, None] >= pends[None, :]).astype(jnp.int32), axis=-1), E - 1)
    blk_first = jnp.concatenate([jnp.ones((1,), jnp.int32), (blk_e[1:] != blk_e[:-1]).astype(jnp.int32)])
    rem_b = jnp.sum((blk_e[:, None] == jnp.arange(E, dtype=jnp.int32)[None, :]) * rem[None, :], axis=-1)
    blk_valid = jnp.where(blk_start >= pends[-1], 0,
                          jnp.where(jnp.logical_and(blk_first == 1, rem_b > 0), rem_b, tm)).astype(jnp.int32)
    xs = _dispatch(table, src, blk_valid)
    hh = _moe_up(xs, w1, w3, blk_e, blk_first, blk_valid, li, E)
    ys = _moe_down(hh, w2, blk_e, blk_first, blk_valid, li, E)
    return _combine(ys, dest, route)


def kernel(x_prompt, x_sample, c_prompt, c_sample, cache_k, cache_v, state_ret, state_conv, state_lru, page_table, ada_w_even, ada_b_even, norm_mix_even, norm_ffn_even, w_in_even, lambda_qk, subln_a, gn_b, w_out_even, ffn_w1, ffn_w3, ffn_w2, ada_w_odd, ada_b_odd, norm_mix_odd, norm_ffn_odd, w_in_odd, conv_w, conv_b, w_gate_a, b_gate_a, w_gate_x, b_gate_x, lru_lambda, w_out_odd, router, moe_w1, moe_w3, moe_w2, norm_final):
    Bp, Lp, D = x_prompt.shape
    Bs, Ls, _ = x_sample.shape
    assert Ls == 1, "the sample group carries one new token per sequence"
    n_even, n_odd = w_in_even.shape[0], w_in_odd.shape[0]
    depth = n_even + n_odd
    _, n_phys, page, HA, _, dka = cache_k.shape
    HB, dkb = state_ret.shape[2], state_ret.shape[3]
    past_len = page_table.shape[1] * page
    E = router.shape[-1]
    CW = conv_w.shape[1]
    W = conv_w.shape[-1]

    half = dkb // 2
    freqs = ROPE_BASE ** (-jnp.arange(half, dtype=F32) / half)
    ang_p = jnp.arange(Lp).astype(F32)[:, None] * freqs[None, :]
    ang_s = (past_len + jnp.arange(Ls)).astype(F32)[:, None] * freqs[None, :]
    cos_p, sin_p, cos_s, sin_s = jnp.cos(ang_p), jnp.sin(ang_p), jnp.cos(ang_s), jnp.sin(ang_s)
    log_gamma = jnp.log1p(-jnp.exp2(-5.0 - jnp.arange(HB, dtype=F32)))
    log_gamma = jnp.broadcast_to(log_gamma[:, None, None], (HB, 1, V7X_LANES))

    c_all = _stack_cond(c_prompt, c_sample)
    row3 = lambda a: a.reshape(a.shape[0], 1, a.shape[-1])
    cache_kt = jnp.transpose(cache_k, (0, 1, 3, 4, 5, 2)).reshape(n_even * n_phys, HA * 2 * dka, page)
    cache_v2 = cache_v.reshape(n_even * n_phys, page * HA, 2 * dka)
    subln3 = subln_a.reshape(n_even, HA, 2 * dka)
    state_ret2 = state_ret.reshape(n_even * Bs, HB, dkb, dkb)
    w_ga2 = w_gate_a.reshape(-1, w_gate_a.shape[-2], w_gate_a.shape[-1])
    w_gx2 = w_gate_x.reshape(-1, w_gate_x.shape[-2], w_gate_x.shape[-1])
    moe_w1f = moe_w1.reshape(-1, moe_w1.shape[-2], moe_w1.shape[-1])
    moe_w3f = moe_w3.reshape(-1, moe_w3.shape[-2], moe_w3.shape[-1])
    moe_w2f = moe_w2.reshape(-1, moe_w2.shape[-2], moe_w2.shape[-1])
    router_pad = jnp.concatenate(
        [router, jnp.zeros(router.shape[:-1] + (V7X_LANES - E,), router.dtype)], axis=-1)
    nf = norm_final.reshape(1, D)

    xp = x_prompt
    xs = x_sample.reshape(1, Bs, D)
    kp_l, vp_l, rp_l, cp_l, lp_l = [], [], [], [], []
    ks_l, vs_l, rs_l, cs_l, ls_l = [], [], [], [], []
    normed = False
    for l in range(depth):
        i = l // 2
        last = l == depth - 1
        if l % 2 == 0:
            lam_init = 0.8 - 0.6 * math.exp(-0.3 * l)
            m = _adaln(c_all, ada_w_even, row3(ada_b_even), i)
            (sh1, sc1, g1, sh2, sc2, g2), (sh1s, sc1s, g1s, sh2s, sc2s, g2s) = _split_mods(m, Bp, Bs)
            gmix, gffn = norm_mix_even[i][None, :], norm_ffn_even[i][None, :]
            wa = HA * 2 * dka
            h = _modulate(xp, gmix, sh1, sc1)
            z = _mm_plain(h, w_in_even, i, F32)
            kp_l.append(z[:, :, wa:2 * wa].reshape(Bp, Lp, HA, 2, dka))
            vp_l.append(z[:, :, 2 * wa:3 * wa].reshape(Bp, Lp, HA, 2 * dka))
            oa = _attn_prompt(z, lambda_qk, row3(subln_a), i, HA, lam_init)
            ob, s_fin = _ret_prompt(z, cos_p, sin_p, log_gamma, row3(gn_b), i, HA, HB)
            rp_l.append(s_fin)
            xp = _mm_residual2(oa, ob, w_out_even, i, xp, g1)
            h = _modulate(xp, gffn, sh2, sc2)
            hh = _mm_swiglu(h, ffn_w1, ffn_w3, i)
            xp = _mm_residual(hh, ffn_w2, i, xp, g2)
            h = _modulate(xs, gmix, sh1s, sc1s)
            zs = _mm_plain(h, w_in_even, i, F32)
            zs3 = zs.reshape(Bs, 1, zs.shape[-1])
            ks_l.append(zs3[:, :, wa:2 * wa].reshape(Bs, Ls, HA, 2, dka))
            v_new = zs3[:, :, 2 * wa:3 * wa].reshape(Bs, HA, 2 * dka)
            vs_l.append(v_new.reshape(Bs, Ls, HA, 2 * dka))
            oa = _attn_sample(zs3, v_new, cache_kt, cache_v2, page_table, lambda_qk, subln3, i, HA, lam_init)
            ob, s_new = _ret_sample(zs3, cos_s, sin_s, log_gamma, row3(gn_b), state_ret2, i, HA, HB)
            rs_l.append(s_new)
            xs = _mm_residual2(oa.reshape(1, Bs, wa), ob.reshape(1, Bs, D - wa), w_out_even, i, xs, g1s)
            h = _modulate(xs, gffn, sh2s, sc2s)
            hh = _mm_swiglu(h, ffn_w1, ffn_w3, i)
            xs = _mm_residual(hh, ffn_w2, i, xs, g2s)
        else:
            m = _adaln(c_all, ada_w_odd, row3(ada_b_odd), i)
            (sh1, sc1, g1, sh2, sc2, g2), (sh1s, sc1s, g1s, sh2s, sc2s, g2s) = _split_mods(m, Bp, Bs)
            gmix, gffn = norm_mix_odd[i][None, :], norm_ffn_odd[i][None, :]
            lru_args = (conv_w, row3(conv_b), w_ga2, w_gx2, row3(b_gate_a), row3(b_gate_x), row3(lru_lambda))
            h = _modulate(xp, gmix, sh1, sc1)
            z = _mm_plain(h, w_in_odd, i, F32)
            tail0 = jnp.zeros((Bp, V7X_SUBLANES, W), F32)
            h0 = jnp.zeros((Bp, 1, W), F32)
            y, h_last = _lru_prompt(z, *lru_args, tail0, h0, i)
            cp_l.append(z[:, Lp - (CW - 1):, W:])
            lp_l.append(h_last.reshape(Bp, W))
            xp = _mm_residual(y, w_out_odd, i, xp, g1)
            h = _modulate(xs, gmix, sh1s, sc1s)
            zs = _mm_plain(h, w_in_odd, i, F32).reshape(Bs, 2 * W)
            buf = state_conv[i]
            ys, h_new = _lru_sample(zs, *lru_args, jnp.swapaxes(buf, 0, 1), state_lru[i], i)
            cs_l.append(jnp.concatenate([buf[:, 1:], zs[:, None, W:]], axis=1))
            ls_l.append(h_new)
            xs = _mm_residual(ys.reshape(1, Bs, W), w_out_odd, i, xs, g1s)
            Tp = Bp * Lp
            table, route = _modulate_route(xp, gffn, sh2, sc2, xs[0], sh2s[0], sc2s[0], router_pad[i], E)
            y_tok = _moe(table, route, moe_w1f, moe_w3f, moe_w2f, i, E)
            xp = _residual(xp, y_tok, 0, g2, nf if last else None)
            xs = _residual(xs, y_tok, Tp, g2s, nf if last else None)
            normed = last
    if not normed:
        xp = _final_norm(xp, nf)
        xs = _final_norm(xs, nf)
    y_prompt = xp.astype(x_prompt.dtype)
    y_sample = xs.reshape(Bs, Ls, D).astype(x_sample.dtype)
    return (y_prompt, y_sample,
            jnp.stack(kp_l), jnp.stack(vp_l), jnp.stack(rp_l), jnp.stack(cp_l), jnp.stack(lp_l),
            jnp.stack(ks_l), jnp.stack(vs_l), jnp.stack(rs_l), jnp.stack(cs_l), jnp.stack(ls_l))
```

```python
import functools
import math

import jax
import jax.numpy as jnp
from jax import lax
from jax.experimental import pallas as pl
from jax.experimental.pallas import tpu as pltpu

F32 = jnp.float32
BF16 = jnp.bfloat16

EPS = 1e-6
ROPE_BASE = 10000.0
LRU_C = 8.0
TOP_K = 2
NEG_BIG = -1e30

V7X_LANES = 128
V7X_SUBLANES = 8
V7X_VMEM_LIMIT_BYTES = 56 * 1024 * 1024


def _cparams(n_axes):
    return pltpu.CompilerParams(dimension_semantics=("arbitrary",) * n_axes,
                                vmem_limit_bytes=V7X_VMEM_LIMIT_BYTES)


def _tile(n, pref, align):
    if n <= pref:
        return n
    t = (pref // align) * align
    while t > align and n % t:
        t -= align
    assert n % t == 0, (n, pref, align)
    return t


def _iota(shape, dim):
    return lax.broadcasted_iota(jnp.int32, shape, dim)


def _adaln_kernel(c_ref, w_ref, b_ref, o_ref):
    c = c_ref[...]
    s = (c * jax.nn.sigmoid(c)).astype(BF16)
    o_ref[...] = jnp.dot(s, w_ref[...].astype(BF16), preferred_element_type=F32) + b_ref[...]


def _adaln(c_all, w, b, li):
    R, D = c_all.shape
    N = w.shape[-1]
    tn = _tile(N, 1024, V7X_LANES)
    return pl.pallas_call(
        _adaln_kernel,
        grid=(N // tn,),
        in_specs=[pl.BlockSpec((R, D), lambda j: (0, 0)),
                  pl.BlockSpec((None, D, tn), lambda j: (li, 0, j)),
                  pl.BlockSpec((None, 1, tn), lambda j: (li, 0, j))],
        out_specs=pl.BlockSpec((R, tn), lambda j: (0, j)),
        out_shape=jax.ShapeDtypeStruct((R, N), F32),
        compiler_params=_cparams(1),
        name="adaln",
    )(c_all, w, b)


def _modulated(x_ref, g_ref, sh_ref, sc_ref):
    x = x_ref[...]
    ms = jnp.mean(x * x, axis=-1, keepdims=True)
    return x * lax.rsqrt(ms + EPS) * g_ref[...] * (1.0 + sc_ref[...]) + sh_ref[...]


def _modulate_kernel(x_ref, g_ref, sh_ref, sc_ref, o_ref):
    o_ref[...] = _modulated(x_ref, g_ref, sh_ref, sc_ref).astype(o_ref.dtype)


def _top2_route(h, wr_ref, n_experts):
    w = wr_ref[...]
    h_hi, w_hi = h.astype(BF16), w.astype(BF16)
    h_lo = (h - h_hi.astype(F32)).astype(BF16)
    w_lo = (w - w_hi.astype(F32)).astype(BF16)
    logits = (jnp.dot(h_hi, w_hi, preferred_element_type=F32) + jnp.dot(h_hi, w_lo, preferred_element_type=F32)
              + jnp.dot(h_lo, w_hi, preferred_element_type=F32))
    lane = _iota(logits.shape, 1)
    logits = jnp.where(lane < n_experts, logits, NEG_BIG)
    m1 = jnp.max(logits, axis=-1, keepdims=True)
    i1 = jnp.min(jnp.where(logits == m1, lane, V7X_LANES), axis=-1, keepdims=True)
    rest = jnp.where(lane == i1, NEG_BIG, logits)
    m2 = jnp.max(rest, axis=-1, keepdims=True)
    i2 = jnp.min(jnp.where(rest == m2, lane, V7X_LANES), axis=-1, keepdims=True)
    e2 = jnp.exp(m2 - m1)
    g1 = 1.0 / (1.0 + e2)
    g2 = e2 / (1.0 + e2)
    return jnp.where(lane == 0, i1.astype(F32),
                     jnp.where(lane == 1, i2.astype(F32),
                               jnp.where(lane == 2, g1, jnp.where(lane == 3, g2, 0.0))))


def _modulate_route_kernel(x_ref, g_ref, sh_ref, sc_ref, xs_ref, shs_ref, scs_ref, wr_ref, o_ref, r_ref,
                           *, n_experts, n_prompt_steps):
    t = pl.program_id(0)

    @pl.when(t < n_prompt_steps)
    def _():
        h = _modulated(x_ref, g_ref, sh_ref, sc_ref)
        o_ref[...] = h
        r_ref[...] = _top2_route(h, wr_ref, n_experts)

    @pl.when(t == n_prompt_steps)
    def _():
        h = _modulated(xs_ref, g_ref, shs_ref, scs_ref)
        n = h.shape[0]
        o_ref[:n, :] = h
        r_ref[:n, :] = _top2_route(h, wr_ref, n_experts)


def _mod_specs(B, L, D, R, tm):
    xmap = lambda b, i: (b, i, 0)
    return [pl.BlockSpec((None, tm, D), xmap),
            pl.BlockSpec((1, D), lambda b, i: (0, 0)),
            pl.BlockSpec((None, R, D), lambda b, i: (b, 0, 0)),
            pl.BlockSpec((None, R, D), lambda b, i: (b, 0, 0))]


def _modulate(x, g, shift, scale):
    B, L, D = x.shape
    R = shift.shape[1]
    tm = L if R == L else _tile(L, 512, V7X_SUBLANES)
    return pl.pallas_call(
        _modulate_kernel,
        grid=(B, L // tm),
        in_specs=_mod_specs(B, L, D, R, tm),
        out_specs=pl.BlockSpec((None, tm, D), lambda b, i: (b, i, 0)),
        out_shape=jax.ShapeDtypeStruct((B, L, D), BF16),
        compiler_params=_cparams(2),
        name="modulate",
    )(x, g, shift, scale)


def _modulate_route(x, g, shift, scale, xs, shift_s, scale_s, w_router_padded, n_experts):
    B, L, D = x.shape
    Bs = xs.shape[0]
    tm = _tile(L, 512, V7X_SUBLANES)
    assert Bs <= tm
    nb = L // tm
    n_p = B * nb
    n_tok = B * L + Bs
    bmap = lambda t: jnp.minimum(t // nb, B - 1)
    imap = lambda t: jnp.where(t < n_p, t % nb, nb - 1)
    whole = lambda t: (0, 0)
    return pl.pallas_call(
        functools.partial(_modulate_route_kernel, n_experts=n_experts, n_prompt_steps=n_p),
        grid=(n_p + 1,),
        in_specs=[pl.BlockSpec((None, tm, D), lambda t: (bmap(t), imap(t), 0)),
                  pl.BlockSpec((1, D), whole),
                  pl.BlockSpec((None, 1, D), lambda t: (bmap(t), 0, 0)),
                  pl.BlockSpec((None, 1, D), lambda t: (bmap(t), 0, 0)),
                  pl.BlockSpec((Bs, D), whole), pl.BlockSpec((Bs, D), whole), pl.BlockSpec((Bs, D), whole),
                  pl.BlockSpec((D, V7X_LANES), whole)],
        out_specs=[pl.BlockSpec((tm, D), lambda t: (t, 0)), pl.BlockSpec((tm, V7X_LANES), lambda t: (t, 0))],
        out_shape=[jax.ShapeDtypeStruct((n_tok, D), F32),
                   jax.ShapeDtypeStruct((n_tok, V7X_LANES), F32)],
        compiler_params=_cparams(1),
        name="modulate_route",
    )(x, g, shift, scale, xs, shift_s, scale_s, w_router_padded)


def _cast_weights_once(first, w_refs, wb_refs):
    @pl.when(first)
    def _():
        for w_ref, wb_ref in zip(w_refs, wb_refs):
            wb_ref[...] = w_ref[...].astype(BF16)


def _mm_plain_kernel(a_ref, w_ref, o_ref, wb_ref):
    first = jnp.logical_and(pl.program_id(1) == 0, pl.program_id(2) == 0)
    _cast_weights_once(first, [w_ref], [wb_ref])
    o_ref[...] = jnp.dot(a_ref[...], wb_ref[...], preferred_element_type=F32).astype(o_ref.dtype)


def _mm_swiglu_kernel(a_ref, w1_ref, w3_ref, o_ref, w1b_ref, w3b_ref):
    first = jnp.logical_and(pl.program_id(1) == 0, pl.program_id(2) == 0)
    _cast_weights_once(first, [w1_ref, w3_ref], [w1b_ref, w3b_ref])
    a = a_ref[...]
    u = jnp.dot(a, w1b_ref[...], preferred_element_type=F32)
    v = jnp.dot(a, w3b_ref[...], preferred_element_type=F32)
    o_ref[...] = (u * jax.nn.sigmoid(u) * v).astype(o_ref.dtype)


def _mm_residual_kernel(a_ref, w_ref, x_ref, g_ref, o_ref, wb_ref):
    first = jnp.logical_and(pl.program_id(1) == 0, pl.program_id(2) == 0)
    _cast_weights_once(first, [w_ref], [wb_ref])
    y = jnp.dot(a_ref[...], wb_ref[...], preferred_element_type=F32)
    o_ref[...] = x_ref[...] + g_ref[...] * y


def _mm_residual2_kernel(a1_ref, a2_ref, w_ref, x_ref, g_ref, o_ref, wb_ref):
    first = jnp.logical_and(pl.program_id(1) == 0, pl.program_id(2) == 0)
    _cast_weights_once(first, [w_ref], [wb_ref])
    k1 = a1_ref.shape[-1]
    y = jnp.dot(a1_ref[...], wb_ref[:k1, :], preferred_element_type=F32)
    y = y + jnp.dot(a2_ref[...], wb_ref[k1:, :], preferred_element_type=F32)
    o_ref[...] = x_ref[...] + g_ref[...] * y


def _mm_tiles(L, K, N, n_w):
    budget = V7X_VMEM_LIMIT_BYTES // 2
    tn = _tile(N, 1024, V7X_LANES)
    while tn > V7X_LANES and n_w * K * tn * (2 * 4 + 2) > budget:
        tn = _tile(N, tn // 2, V7X_LANES)
    tm = _tile(L, 1024 if K <= 2048 else 512, V7X_SUBLANES)
    return tm, tn


def _mm_plain(a, w, li, out_dtype):
    B, L, K = a.shape
    N = w.shape[-1]
    tm, tn = _mm_tiles(L, K, N, 1)
    return pl.pallas_call(
        _mm_plain_kernel,
        grid=(N // tn, B, L // tm),
        in_specs=[pl.BlockSpec((None, tm, K), lambda j, b, i: (b, i, 0)),
                  pl.BlockSpec((None, K, tn), lambda j, b, i: (li, 0, j))],
        out_specs=pl.BlockSpec((None, tm, tn), lambda j, b, i: (b, i, j)),
        out_shape=jax.ShapeDtypeStruct((B, L, N), out_dtype),
        scratch_shapes=[pltpu.VMEM((K, tn), BF16)],
        compiler_params=_cparams(3),
        name="mm_plain",
    )(a, w)


def _mm_swiglu(a, w1, w3, li):
    B, L, K = a.shape
    N = w1.shape[-1]
    tm, tn = _mm_tiles(L, K, N, 2)
    wspec = pl.BlockSpec((None, K, tn), lambda j, b, i: (li, 0, j))
    return pl.pallas_call(
        _mm_swiglu_kernel,
        grid=(N // tn, B, L // tm),
        in_specs=[pl.BlockSpec((None, tm, K), lambda j, b, i: (b, i, 0)), wspec, wspec],
        out_specs=pl.BlockSpec((None, tm, tn), lambda j, b, i: (b, i, j)),
        out_shape=jax.ShapeDtypeStruct((B, L, N), BF16),
        scratch_shapes=[pltpu.VMEM((K, tn), BF16), pltpu.VMEM((K, tn), BF16)],
        compiler_params=_cparams(3),
        name="mm_swiglu",
    )(a, w1, w3)


def _mm_residual(a, w, li, x, gate):
    B, L, K = a.shape
    N = w.shape[-1]
    R = gate.shape[1]
    tm, tn = _mm_tiles(L, K, N, 1)
    if R == L:
        tm = L
    return pl.pallas_call(
        _mm_residual_kernel,
        grid=(N // tn, B, L // tm),
        in_specs=[pl.BlockSpec((None, tm, K), lambda j, b, i: (b, i, 0)),
                  pl.BlockSpec((None, K, tn), lambda j, b, i: (li, 0, j)),
                  pl.BlockSpec((None, tm, tn), lambda j, b, i: (b, i, j)),
                  pl.BlockSpec((None, R, tn), lambda j, b, i: (b, 0, j))],
        out_specs=pl.BlockSpec((None, tm, tn), lambda j, b, i: (b, i, j)),
        out_shape=jax.ShapeDtypeStruct((B, L, N), F32),
        scratch_shapes=[pltpu.VMEM((K, tn), BF16)],
        compiler_params=_cparams(3),
        name="mm_residual",
    )(a, w, x, gate)


def _mm_residual2(a1, a2, w, li, x, gate):
    B, L, K1 = a1.shape
    K2 = a2.shape[-1]
    K = K1 + K2
    N = w.shape[-1]
    R = gate.shape[1]
    tm, tn = _mm_tiles(L, K, N, 1)
    if R == L:
        tm = L
    return pl.pallas_call(
        _mm_residual2_kernel,
        grid=(N // tn, B, L // tm),
        in_specs=[pl.BlockSpec((None, tm, K1), lambda j, b, i: (b, i, 0)),
                  pl.BlockSpec((None, tm, K2), lambda j, b, i: (b, i, 0)),
                  pl.BlockSpec((None, K, tn), lambda j, b, i: (li, 0, j)),
                  pl.BlockSpec((None, tm, tn), lambda j, b, i: (b, i, j)),
                  pl.BlockSpec((None, R, tn), lambda j, b, i: (b, 0, j))],
        out_specs=pl.BlockSpec((None, tm, tn), lambda j, b, i: (b, i, j)),
        out_shape=jax.ShapeDtypeStruct((B, L, N), F32),
        scratch_shapes=[pltpu.VMEM((K, tn), BF16)],
        compiler_params=_cparams(3),
        name="mm_residual2",
    )(a1, a2, w, x, gate)


def _diff_lambda(lq_ref, lam_init):
    lq = lq_ref[...]
    a = jnp.sum(lq[0:1] * lq[1:2], axis=-1, keepdims=True)
    b = jnp.sum(lq[2:3] * lq[3:4], axis=-1, keepdims=True)
    return jnp.exp(a) - jnp.exp(b) + lam_init


def _attn_prompt_kernel(q_ref, k_ref, v_ref, lq_ref, sub_ref, o_ref, kt_ref, vo_ref, *, tq, tk, dk, lam_init):
    qi = pl.program_id(2)

    @pl.when(qi == 0)
    def _():
        kt_ref[...] = k_ref[...].T
        vo_ref[...] = v_ref[...]

    hw = 2 * dk
    sc = dk ** -0.5
    q = q_ref[...] * sc
    lane = _iota(q.shape, 1)
    qq = jnp.concatenate([jnp.where(lane < dk, q, 0.0), jnp.where(lane >= dk, q, 0.0)], axis=0).astype(BF16)

    def block(j, carry, masked):
        m, l, acc = carry
        start = pl.multiple_of(j * tk, tk)
        kb = k_ref[pl.ds(start, tk), :].astype(BF16)
        vb = v_ref[pl.ds(start, tk), :].astype(BF16)
        s = lax.dot_general(qq, kb, (((1,), (1,)), ((), ())), preferred_element_type=F32)
        if masked:
            row = _iota(s.shape, 0)
            row = jnp.where(row >= tq, row - tq, row) + qi * tq
            s = jnp.where(_iota(s.shape, 1) + j * tk <= row, s, NEG_BIG)
        m_new = jnp.maximum(m, jnp.max(s, axis=-1, keepdims=True))
        alpha = jnp.exp(m - m_new)
        p = jnp.exp(s - m_new)
        l = alpha * l + jnp.sum(p, axis=-1, keepdims=True)
        acc = alpha * acc + jnp.dot(p.astype(BF16), vb, preferred_element_type=F32)
        return m_new, l, acc

    init = (jnp.full((2 * tq, 1), NEG_BIG, F32), jnp.zeros((2 * tq, 1), F32), jnp.zeros((2 * tq, hw), F32))
    n_full = (qi * tq) // tk
    carry = lax.fori_loop(0, n_full, lambda j, c: block(j, c, False), init)
    m, l, acc = block(n_full, carry, True)
    on = acc / l
    lam = _diff_lambda(lq_ref, lam_init)
    o = on[:tq] - lam * on[tq:]
    ms = jnp.mean(o * o, axis=-1, keepdims=True)
    o_ref[...] = (o * lax.rsqrt(ms + EPS) * sub_ref[...] * (1.0 - lam_init)).astype(o_ref.dtype)


def _attn_prompt(z, lambda_qk, subln, li, n_heads, lam_init):
    B, L, _ = z.shape
    hw = 2 * lambda_qk.shape[-1]
    tq = _tile(L, 512, V7X_SUBLANES)
    tk = _tile(L, 512, V7X_SUBLANES)
    assert tk % tq == 0
    H = n_heads
    return pl.pallas_call(
        functools.partial(_attn_prompt_kernel, tq=tq, tk=tk, dk=hw // 2, lam_init=lam_init),
        grid=(B, H, L // tq),
        in_specs=[pl.BlockSpec((None, tq, hw), lambda b, h, i: (b, i, h)),
                  pl.BlockSpec((None, L, hw), lambda b, h, i: (b, 0, H + h)),
                  pl.BlockSpec((None, L, hw), lambda b, h, i: (b, 0, 2 * H + h)),
                  pl.BlockSpec((None,) + lambda_qk.shape[1:], lambda b, h, i: (li, 0, 0)),
                  pl.BlockSpec((None, 1, hw), lambda b, h, i: (li, 0, h))],
        out_specs=[pl.BlockSpec((None, tq, hw), lambda b, h, i: (b, i, h)),
                   pl.BlockSpec((None, hw, L), lambda b, h, i: (b, h, 0)),
                   pl.BlockSpec((None, L, hw), lambda b, h, i: (b, 0, h))],
        out_shape=[jax.ShapeDtypeStruct((B, L, H * hw), BF16),
                   jax.ShapeDtypeStruct((B, H * hw, L), F32),
                   jax.ShapeDtypeStruct((B, L, H * hw), F32)],
        compiler_params=_cparams(3),
        name="attn_prompt",
    )(z, z, z, lambda_qk, subln)


def _attn_sample_kernel(pt_ref, q_ref, kn_ref, vn_ref, lq_ref, sub_ref, *rest, n_pages, n_heads, dk, lam_init):
    kt_refs = rest[:n_pages]
    v_refs = rest[n_pages:2 * n_pages]
    o_ref, m_ref, l_ref, acc_ref = rest[2 * n_pages:]
    s_id = pl.program_id(1)
    H = n_heads
    hw = 2 * dk
    W = H * hw
    page = kt_refs[0].shape[-1]
    sc = dk ** -0.5

    @pl.when(s_id == 0)
    def _():
        m_ref[...] = jnp.full(m_ref.shape, NEG_BIG, F32)
        l_ref[...] = jnp.zeros(l_ref.shape, F32)
        acc_ref[...] = jnp.zeros(acc_ref.shape, F32)

    row = _iota((2 * H, W), 0)
    lane = _iota((2 * H, W), 1)
    r_head = jnp.where(row >= H, row - H, row)
    r_comp = jnp.where(row >= H, 1, 0)
    qsel = jnp.logical_and(lane // hw == r_head, (lane // dk) % 2 == r_comp)
    qf = jnp.where(qsel, q_ref[...] * sc, 0.0)
    qb = qf.astype(BF16)

    s_parts = [jnp.dot(qb, kt_refs[r][...].astype(BF16), preferred_element_type=F32) for r in range(n_pages)]
    s = jnp.concatenate(s_parts, axis=-1) if n_pages > 1 else s_parts[0]
    m = m_ref[...]
    m_new = jnp.maximum(m, jnp.max(s, axis=-1, keepdims=True))
    alpha = jnp.exp(m - m_new)
    p = jnp.exp(s - m_new)
    l_ref[...] = alpha * l_ref[...] + jnp.sum(p, axis=-1, keepdims=True)
    m_ref[...] = m_new
    p_head = r_head[:, :page]
    pv = jnp.zeros((2 * H, hw), F32)
    for r in range(n_pages):
        pr = p[:, r * page:(r + 1) * page]
        pexp = jnp.concatenate([jnp.where(p_head == h, pr, 0.0) for h in range(H)], axis=1).astype(BF16)
        vh = jnp.concatenate([v_refs[r][pl.ds(h, page, stride=H), :] for h in range(H)], axis=0).astype(BF16)
        pv = pv + jnp.dot(pexp, vh, preferred_element_type=F32)
    acc_ref[...] = alpha * acc_ref[...] + pv

    @pl.when(s_id == pl.num_programs(1) - 1)
    def _():
        s_new = jnp.sum(qf * kn_ref[...], axis=-1, keepdims=True)
        m_old = m_ref[...]
        m_fin = jnp.maximum(m_old, s_new)
        a_fin = jnp.exp(m_old - m_fin)
        p_new = jnp.exp(s_new - m_fin)
        l_fin = a_fin * l_ref[...] + p_new
        vn = vn_ref[...]
        acc = a_fin * acc_ref[...] + p_new * jnp.concatenate([vn, vn], axis=0)
        on = acc / l_fin
        lam = _diff_lambda(lq_ref, lam_init)
        o = on[:H] - lam * on[H:]
        ms = jnp.mean(o * o, axis=-1, keepdims=True)
        o_ref[...] = (o * lax.rsqrt(ms + EPS) * sub_ref[...] * (1.0 - lam_init)).astype(o_ref.dtype)


def _attn_sample(z, v_new, cache_kt, cache_v, page_table, lambda_qk, subln, li, n_heads, lam_init):
    Bs = z.shape[0]
    n_phys_total, W, page = cache_kt.shape
    n_pg = page_table.shape[1]
    P = next(c for c in (8, 4, 2, 1) if n_pg % c == 0)
    n_phys = n_phys_total // lambda_qk.shape[0]
    base = li * n_phys
    H = n_heads
    dk = lambda_qk.shape[-1]
    hw = 2 * dk

    def page_spec(shape, r):
        return pl.BlockSpec((None,) + shape, lambda b, s, pt: (base + pt[b, s * P + r], 0, 0))

    zrow = lambda c: pl.BlockSpec((None, 1, W), lambda b, s, pt: (b, 0, c))
    grid_spec = pltpu.PrefetchScalarGridSpec(
        num_scalar_prefetch=1,
        grid=(Bs, n_pg // P),
        in_specs=[zrow(0), zrow(1),
                  pl.BlockSpec((None, H, hw), lambda b, s, pt: (b, 0, 0)),
                  pl.BlockSpec((None,) + lambda_qk.shape[1:], lambda b, s, pt: (li, 0, 0)),
                  pl.BlockSpec((None, H, hw), lambda b, s, pt: (li, 0, 0))]
                 + [page_spec((W, page), r) for r in range(P)]
                 + [page_spec((page * H, hw), r) for r in range(P)],
        out_specs=pl.BlockSpec((None, H, hw), lambda b, s, pt: (b, 0, 0)),
        scratch_shapes=[pltpu.VMEM((2 * H, 1), F32), pltpu.VMEM((2 * H, 1), F32),
                        pltpu.VMEM((2 * H, hw), F32)],
    )
    return pl.pallas_call(
        functools.partial(_attn_sample_kernel, n_pages=P, n_heads=H, dk=dk, lam_init=lam_init),
        grid_spec=grid_spec,
        out_shape=jax.ShapeDtypeStruct((Bs, H, hw), BF16),
        compiler_params=_cparams(2),
        name="attn_sample",
    )(page_table, z, z, v_new, lambda_qk, subln, *([cache_kt] * P), *([cache_v] * P))


def _rope(x, cos, sin):
    half = x.shape[-1] // 2
    x1, x2 = x[:, :half], x[:, half:]
    return jnp.concatenate([x1 * cos - x2 * sin, x1 * sin + x2 * cos], axis=-1)


def _groupnorm_gate(o, gn, g):
    mu = jnp.mean(o, axis=-1, keepdims=True)
    d = o - mu
    var = jnp.mean(d * d, axis=-1, keepdims=True)
    return d * lax.rsqrt(var + EPS) * gn * (g * jax.nn.sigmoid(g))


def _ret_prompt_kernel(q_ref, k_ref, v_ref, g_ref, cos_ref, sin_ref, lg_ref, gn_ref,
                       o_ref, sfin_ref, s_ref, *, chunk, dk):
    c = pl.program_id(2)

    @pl.when(c == 0)
    def _():
        s_ref[...] = jnp.zeros(s_ref.shape, F32)

    cos, sin = cos_ref[...], sin_ref[...]
    qr = _rope(q_ref[...], cos, sin) * dk ** -0.5
    kr = _rope(k_ref[...], cos, sin)
    vb = v_ref[...].astype(BF16)
    lg = lg_ref[...][:, :1]
    C = chunk
    d = _iota((C, C), 0) - _iota((C, C), 1)
    decay = jnp.where(d >= 0, jnp.exp(jnp.maximum(d, 0).astype(F32) * lg), 0.0)
    qrb = qr.astype(BF16)
    scores = lax.dot_general(qrb, kr.astype(BF16), (((1,), (1,)), ((), ())), preferred_element_type=F32) * decay
    intra = jnp.dot(scores.astype(BF16), vb, preferred_element_type=F32)
    t = _iota((C, 1), 0).astype(F32)
    s_old = s_ref[...]
    inter = jnp.dot(qrb, s_old.astype(BF16), preferred_element_type=F32) * jnp.exp((t + 1.0) * lg)
    kd = kr * jnp.exp((C - 1.0 - t) * lg)
    s_new = jnp.exp(C * lg) * s_old + jnp.dot(kd.T.astype(BF16), vb, preferred_element_type=F32)
    s_ref[...] = s_new
    sfin_ref[...] = s_new
    o_ref[...] = _groupnorm_gate(intra + inter, gn_ref[...], g_ref[...]).astype(o_ref.dtype)


def _ret_prompt(z, cos_tab, sin_tab, log_gamma, gn_b, li, n_heads_a, n_heads_b):
    B, L, _ = z.shape
    dk = 2 * cos_tab.shape[-1]
    HB = n_heads_b
    assert (3 * n_heads_a * V7X_LANES) % dk == 0
    off = 3 * n_heads_a * V7X_LANES // dk
    C = _tile(L, 256, V7X_SUBLANES)
    zb = lambda g: pl.BlockSpec((None, C, dk), lambda b, h, c: (b, c, off + g * HB + h))
    return pl.pallas_call(
        functools.partial(_ret_prompt_kernel, chunk=C, dk=dk),
        grid=(B, HB, L // C),
        in_specs=[zb(0), zb(1), zb(2), zb(3),
                  pl.BlockSpec((C, dk // 2), lambda b, h, c: (c, 0)),
                  pl.BlockSpec((C, dk // 2), lambda b, h, c: (c, 0)),
                  pl.BlockSpec((None, 1, V7X_LANES), lambda b, h, c: (h, 0, 0)),
                  pl.BlockSpec((None, 1, dk), lambda b, h, c: (li, 0, h))],
        out_specs=[pl.BlockSpec((None, C, dk), lambda b, h, c: (b, c, h)),
                   pl.BlockSpec((None, None, dk, dk), lambda b, h, c: (b, h, 0, 0))],
        out_shape=[jax.ShapeDtypeStruct((B, L, HB * dk), BF16),
                   jax.ShapeDtypeStruct((B, HB, dk, dk), F32)],
        scratch_shapes=[pltpu.VMEM((dk, dk), F32)],
        compiler_params=_cparams(3),
        name="ret_prompt",
    )(z, z, z, z, cos_tab, sin_tab, log_gamma, gn_b)


def _ret_sample_kernel(q_ref, k_ref, v_ref, g_ref, cos_ref, sin_ref, lg_ref, gn_ref, s0_ref,
                       o_ref, snew_ref, *, dk):
    cos, sin = cos_ref[...], sin_ref[...]
    qr = _rope(q_ref[...], cos, sin) * dk ** -0.5
    kr = _rope(k_ref[...], cos, sin)
    v = v_ref[...]
    gamma = jnp.exp(lg_ref[...][:, :1])
    eye = _iota((dk, dk), 0) == _iota((dk, dk), 1)
    kcol = jnp.sum(jnp.where(eye, kr, 0.0), axis=-1, keepdims=True)
    qcol = jnp.sum(jnp.where(eye, qr, 0.0), axis=-1, keepdims=True)
    s0 = s0_ref[...]
    snew_ref[...] = gamma * s0 + kcol * v
    inter = gamma * jnp.sum(qcol * s0, axis=0, keepdims=True)
    intra = jnp.sum(qr * kr, axis=-1, keepdims=True) * v
    o_ref[...] = _groupnorm_gate(intra + inter, gn_ref[...], g_ref[...]).astype(o_ref.dtype)


def _ret_sample(z, cos_row, sin_row, log_gamma, gn_b, state, li, n_heads_a, n_heads_b):
    Bs = z.shape[0]
    dk = 2 * cos_row.shape[-1]
    HB = n_heads_b
    off = 3 * n_heads_a * V7X_LANES // dk
    zb = lambda g: pl.BlockSpec((None, 1, dk), lambda b, h: (b, 0, off + g * HB + h))
    return pl.pallas_call(
        functools.partial(_ret_sample_kernel, dk=dk),
        grid=(Bs, HB),
        in_specs=[zb(0), zb(1), zb(2), zb(3),
                  pl.BlockSpec((1, dk // 2), lambda b, h: (0, 0)),
                  pl.BlockSpec((1, dk // 2), lambda b, h: (0, 0)),
                  pl.BlockSpec((None, 1, V7X_LANES), lambda b, h: (h, 0, 0)),
                  pl.BlockSpec((None, 1, dk), lambda b, h: (li, 0, h)),
                  pl.BlockSpec((None, None, dk, dk), lambda b, h: (li * Bs + b, h, 0, 0))],
        out_specs=[pl.BlockSpec((None, 1, dk), lambda b, h: (b, 0, h)),
                   pl.BlockSpec((None, None, dk, dk), lambda b, h: (b, h, 0, 0))],
        out_shape=[jax.ShapeDtypeStruct((Bs, 1, HB * dk), BF16),
                   jax.ShapeDtypeStruct((Bs, HB, dk, dk), F32)],
        compiler_params=_cparams(2),
        name="ret_sample",
    )(z, z, z, z, cos_row, sin_row, log_gamma, gn_b, state)


def _softplus(z):
    return jnp.maximum(z, 0.0) + jnp.log1p(jnp.exp(-jnp.abs(z)))


def _lru_coeffs(xc, wa_ref, wx_ref, ba_ref, bx_ref, lam_ref):
    xcb = xc.astype(BF16)
    r = jax.nn.sigmoid(jnp.dot(xcb, wa_ref[...].astype(BF16), preferred_element_type=F32) + ba_ref[...])
    gi = jax.nn.sigmoid(jnp.dot(xcb, wx_ref[...].astype(BF16), preferred_element_type=F32) + bx_ref[...])
    log_a = -LRU_C * r * _softplus(-lam_ref[...])
    a = jnp.exp(log_a)
    t = jnp.tanh(log_a)
    mult = jnp.sqrt(jnp.maximum(-2.0 * t / (1.0 - t), 0.0))
    return a, mult * gi * xc


def _lru_prompt_kernel(gate_ref, x_ref, cw_ref, cb_ref, wa_ref, wx_ref, ba_ref, bx_ref, lam_ref,
                       tail0_ref, h0_ref, y_ref, hl_ref, tail_ref, hc_ref, *, tt, conv_w):
    t_id = pl.program_id(2)

    @pl.when(t_id == 0)
    def _():
        tail_ref[...] = tail0_ref[...]
        hc_ref[...] = h0_ref[...]

    x = x_ref[...]
    tl = tail_ref[...]
    row8 = _iota(tl.shape, 0)
    xc = cb_ref[...] + cw_ref[conv_w - 1:conv_w, :] * x
    for s in range(1, conv_w):
        xs = pltpu.roll(x, s, 0)
        head = jnp.where(row8 >= s, xs[:V7X_SUBLANES], pltpu.roll(tl, s, 0))
        xs = jnp.concatenate([head, xs[V7X_SUBLANES:]], axis=0) if tt > V7X_SUBLANES else head
        xc = xc + cw_ref[conv_w - 1 - s:conv_w - s, :] * xs
    tail_ref[...] = x[tt - V7X_SUBLANES:, :]

    a, b = _lru_coeffs(xc, wa_ref, wx_ref, ba_ref, bx_ref, lam_ref)
    row = _iota(a.shape, 0)
    s = 1
    while s < tt:
        if s % V7X_SUBLANES == 0:
            a_s = jnp.concatenate([jnp.ones((s, a.shape[1]), F32), a[:tt - s]], axis=0)
            b_s = jnp.concatenate([jnp.zeros((s, a.shape[1]), F32), b[:tt - s]], axis=0)
        else:
            keep = row >= s
            a_s = jnp.where(keep, pltpu.roll(a, s, 0), 1.0)
            b_s = jnp.where(keep, pltpu.roll(b, s, 0), 0.0)
        b = b + a * b_s
        a = a * a_s
        s *= 2
    h = b + a * hc_ref[...]
    hc_ref[...] = h[tt - 1:tt, :]
    hl_ref[...] = h[tt - 1:tt, :]
    y_ref[...] = (jax.nn.gelu(gate_ref[...], approximate=True) * h).astype(y_ref.dtype)


def _lru_prompt(z, conv_w, conv_b, w_ga, w_gx, b_ga, b_gx, lam, tail0, h0, li):
    B, L, W2 = z.shape
    W = W2 // 2
    bw = w_ga.shape[-1]
    NB = W // bw
    CW = conv_w.shape[1]
    tt = _tile(L, 256, V7X_SUBLANES)
    vec = lambda: pl.BlockSpec((None, 1, bw), lambda b, n, t: (li, 0, n))
    gw = lambda: pl.BlockSpec((None, bw, bw), lambda b, n, t: (li * NB + n, 0, 0))
    return pl.pallas_call(
        functools.partial(_lru_prompt_kernel, tt=tt, conv_w=CW),
        grid=(B, NB, L // tt),
        in_specs=[pl.BlockSpec((None, tt, bw), lambda b, n, t: (b, t, n)),
                  pl.BlockSpec((None, tt, bw), lambda b, n, t: (b, t, NB + n)),
                  pl.BlockSpec((None, CW, bw), lambda b, n, t: (li, 0, n)),
                  vec(), gw(), gw(), vec(), vec(), vec(),
                  pl.BlockSpec((None, V7X_SUBLANES, bw), lambda b, n, t: (b, 0, n)),
                  pl.BlockSpec((None, 1, bw), lambda b, n, t: (b, 0, n))],
        out_specs=[pl.BlockSpec((None, tt, bw), lambda b, n, t: (b, t, n)),
                   pl.BlockSpec((None, 1, bw), lambda b, n, t: (b, 0, n))],
        out_shape=[jax.ShapeDtypeStruct((B, L, W), BF16), jax.ShapeDtypeStruct((B, 1, W), F32)],
        scratch_shapes=[pltpu.VMEM((V7X_SUBLANES, bw), F32), pltpu.VMEM((1, bw), F32)],
        compiler_params=_cparams(3),
        name="lru_prompt",
    )(z, z, conv_w, conv_b, w_ga, w_gx, b_ga, b_gx, lam, tail0, h0)


def _lru_sample_kernel(gate_ref, x_ref, cw_ref, cb_ref, wa_ref, wx_ref, ba_ref, bx_ref, lam_ref,
                       buf_ref, h0_ref, y_ref, h_ref, *, conv_w):
    x = x_ref[...]
    xc = cb_ref[...] + cw_ref[conv_w - 1:conv_w, :] * x
    for j in range(conv_w - 1):
        xc = xc + cw_ref[j:j + 1, :] * buf_ref[j]
    a, b = _lru_coeffs(xc, wa_ref, wx_ref, ba_ref, bx_ref, lam_ref)
    h = a * h0_ref[...] + b
    h_ref[...] = h
    y_ref[...] = (jax.nn.gelu(gate_ref[...], approximate=True) * h).astype(y_ref.dtype)


def _lru_sample(z, conv_w, conv_b, w_ga, w_gx, b_ga, b_gx, lam, buf, h0, li):
    Bs, W2 = z.shape
    W = W2 // 2
    bw = w_ga.shape[-1]
    NB = W // bw
    CW = conv_w.shape[1]
    vec = lambda: pl.BlockSpec((None, 1, bw), lambda n: (li, 0, n))
    gw = lambda: pl.BlockSpec((None, bw, bw), lambda n: (li * NB + n, 0, 0))
    return pl.pallas_call(
        functools.partial(_lru_sample_kernel, conv_w=CW),
        grid=(NB,),
        in_specs=[pl.BlockSpec((Bs, bw), lambda n: (0, n)),
                  pl.BlockSpec((Bs, bw), lambda n: (0, NB + n)),
                  pl.BlockSpec((None, CW, bw), lambda n: (li, 0, n)),
                  vec(), gw(), gw(), vec(), vec(), vec(),
                  pl.BlockSpec((CW - 1, Bs, bw), lambda n: (0, 0, n)),
                  pl.BlockSpec((Bs, bw), lambda n: (0, n))],
        out_specs=[pl.BlockSpec((Bs, bw), lambda n: (0, n)),
                   pl.BlockSpec((Bs, bw), lambda n: (0, n))],
        out_shape=[jax.ShapeDtypeStruct((Bs, W), BF16), jax.ShapeDtypeStruct((Bs, W), F32)],
        compiler_params=_cparams(1),
        name="lru_sample",
    )(z, z, conv_w, conv_b, w_ga, w_gx, b_ga, b_gx, lam, buf, h0)


MOE_ROW_BLOCK = 512
MOE_SUB_BLOCK = 128


def _valid_rows_only(n_valid, o_ref, compute):
    tm = o_ref.shape[0]

    @pl.when(n_valid == tm)
    def _():
        o_ref[...] = compute(slice(None))

    for s in range(tm // MOE_SUB_BLOCK):
        rows = slice(s * MOE_SUB_BLOCK, (s + 1) * MOE_SUB_BLOCK)

        @pl.when(jnp.logical_and(n_valid < tm, n_valid > s * MOE_SUB_BLOCK))
        def _(rows=rows):
            o_ref[rows, :] = compute(rows)

        @pl.when(n_valid <= s * MOE_SUB_BLOCK)
        def _(rows=rows):
            o_ref[rows, :] = jnp.zeros((MOE_SUB_BLOCK, o_ref.shape[1]), o_ref.dtype)


def _stream_weight_panels(bf_ref, sg_ref, se_ref, ns_ref, w_hbms, wf_refs, wb_refs, sem, *, e0, tn):
    j = pl.program_id(0)
    rb = pl.program_id(1)
    n_seg = ns_ref[0]

    def copies(k, jcol, slot):
        col = pl.multiple_of(jcol * tn, tn)
        return [pltpu.make_async_copy(w.at[e0 + se_ref[k], :, pl.ds(col, tn)], wf.at[slot], sem.at[i, slot])
                for i, (w, wf) in enumerate(zip(w_hbms, wf_refs))]

    @pl.when(bf_ref[rb] == 1)
    def _():
        k = sg_ref[rb]
        slot = jnp.bitwise_and(j * n_seg + k, 1)

        @pl.when(jnp.logical_and(j == 0, k == 0))
        def _():
            for c in copies(0, 0, 0):
                c.start()

        for c in copies(k, j, slot):
            c.wait()
        for wf, wb in zip(wf_refs, wb_refs):
            wb[...] = wf[slot].astype(BF16)

        last = k + 1 == n_seg
        k2 = jnp.where(last, 0, k + 1)
        j2 = jnp.where(last, j + 1, j)

        @pl.when(j2 < pl.num_programs(0))
        def _():
            for c in copies(k2, j2, 1 - slot):
                c.start()


def _moe_up_kernel(bf_ref, bv_ref, sg_ref, se_ref, ns_ref, x_ref, w1_hbm, w3_hbm, o_ref,
                   w1f_ref, w3f_ref, w1b_ref, w3b_ref, sem, *, e0, tn):
    rb = pl.program_id(1)
    _stream_weight_panels(bf_ref, sg_ref, se_ref, ns_ref, [w1_hbm, w3_hbm], [w1f_ref, w3f_ref],
                          [w1b_ref, w3b_ref], sem, e0=e0, tn=tn)

    def compute(rows):
        x = x_ref[rows, :]
        u = jnp.dot(x, w1b_ref[...], preferred_element_type=F32)
        v = jnp.dot(x, w3b_ref[...], preferred_element_type=F32)
        return (u * jax.nn.sigmoid(u) * v).astype(o_ref.dtype)

    _valid_rows_only(bv_ref[rb], o_ref, compute)


def _moe_down_kernel(bf_ref, bv_ref, sg_ref, se_ref, ns_ref, a_ref, w_hbm, o_ref, wf_ref, wb_ref, sem, *, e0, tn):
    rb = pl.program_id(1)
    _stream_weight_panels(bf_ref, sg_ref, se_ref, ns_ref, [w_hbm], [wf_ref], [wb_ref], sem, e0=e0, tn=tn)

    def compute(rows):
        return jnp.dot(a_ref[rows, :], wb_ref[...], preferred_element_type=F32)

    _valid_rows_only(bv_ref[rb], o_ref, compute)


def _moe_up(xs, w1, w3, seg, li, n_experts):
    n_rows, D = xs.shape
    F = w1.shape[-1]
    tm = MOE_ROW_BLOCK
    tn = _tile(F, 1024, V7X_LANES)
    rows = lambda j, rb, *_: (rb, 0)
    grid_spec = pltpu.PrefetchScalarGridSpec(
        num_scalar_prefetch=5,
        grid=(F // tn, n_rows // tm),
        in_specs=[pl.BlockSpec((tm, D), rows), pl.BlockSpec(memory_space=pl.ANY),
                  pl.BlockSpec(memory_space=pl.ANY)],
        out_specs=pl.BlockSpec((tm, tn), lambda j, rb, *_: (rb, j)),
        scratch_shapes=[pltpu.VMEM((2, D, tn), F32), pltpu.VMEM((2, D, tn), F32),
                        pltpu.VMEM((D, tn), BF16), pltpu.VMEM((D, tn), BF16),
                        pltpu.SemaphoreType.DMA((2, 2))],
    )
    return pl.pallas_call(
        functools.partial(_moe_up_kernel, e0=li * n_experts, tn=tn),
        grid_spec=grid_spec,
        out_shape=jax.ShapeDtypeStruct((n_rows, F), BF16),
        compiler_params=_cparams(2),
        name="moe_up",
    )(*seg, xs, w1, w3)


def _moe_down(hh, w2, seg, li, n_experts):
    n_rows, F = hh.shape
    D = w2.shape[-1]
    tm = MOE_ROW_BLOCK
    tn = _tile(D, 512, V7X_LANES)
    grid_spec = pltpu.PrefetchScalarGridSpec(
        num_scalar_prefetch=5,
        grid=(D // tn, n_rows // tm),
        in_specs=[pl.BlockSpec((tm, F), lambda j, rb, *_: (rb, 0)), pl.BlockSpec(memory_space=pl.ANY)],
        out_specs=pl.BlockSpec((tm, tn), lambda j, rb, *_: (rb, j)),
        scratch_shapes=[pltpu.VMEM((2, F, tn), F32), pltpu.VMEM((F, tn), BF16),
                        pltpu.SemaphoreType.DMA((1, 2))],
    )
    return pl.pallas_call(
        functools.partial(_moe_down_kernel, e0=li * n_experts, tn=tn),
        grid_spec=grid_spec,
        out_shape=jax.ShapeDtypeStruct((n_rows, D), F32),
        compiler_params=_cparams(2),
        name="moe_down",
    )(*seg, hh, w2)


def _dispatch_kernel(src_ref, bv_ref, tab_ref, o_ref, buf_ref, *, tm):
    rb = pl.program_id(1)

    @pl.when(bv_ref[rb] > 0)
    def _():
        base = rb * tm

        def body(r, carry):
            buf_ref[pl.ds(r, 1), :] = tab_ref[pl.ds(src_ref[base + r], 1), :]
            return carry

        lax.fori_loop(0, tm, body, 0, unroll=8)
        o_ref[...] = buf_ref[...].astype(o_ref.dtype)

    @pl.when(bv_ref[rb] == 0)
    def _():
        o_ref[...] = jnp.zeros(o_ref.shape, o_ref.dtype)


def _dispatch(table, src, blk_valid):
    n_tok, D = table.shape
    n_rows = src.shape[0]
    tm = MOE_ROW_BLOCK
    tc = D
    while tc > V7X_LANES and n_tok * tc * 4 > 36 * 1024 * 1024:
        tc = _tile(D, tc // 2, V7X_LANES)
    grid_spec = pltpu.PrefetchScalarGridSpec(
        num_scalar_prefetch=2,
        grid=(D // tc, n_rows // tm),
        in_specs=[pl.BlockSpec((n_tok, tc), lambda j, rb, src, nu: (0, j), pipeline_mode=pl.Buffered(1))],
        out_specs=pl.BlockSpec((tm, tc), lambda j, rb, src, nu: (rb, j)),
        scratch_shapes=[pltpu.VMEM((tm, tc), F32)],
    )
    return pl.pallas_call(
        functools.partial(_dispatch_kernel, tm=tm),
        grid_spec=grid_spec,
        out_shape=jax.ShapeDtypeStruct((n_rows, D), BF16),
        compiler_params=_cparams(2),
        name="moe_dispatch",
    )(src, blk_valid, table)


def _combine_kernel(d_ref, tab_ref, route_ref, o_ref, a_ref, b_ref, *, tm):
    base = pl.program_id(1) * (tm * TOP_K)

    def body(t, carry):
        a_ref[pl.ds(t, 1), :] = tab_ref[pl.ds(d_ref[base + TOP_K * t], 1), :]
        b_ref[pl.ds(t, 1), :] = tab_ref[pl.ds(d_ref[base + TOP_K * t + 1], 1), :]
        return carry

    lax.fori_loop(0, tm, body, 0, unroll=8)
    rt = route_ref[...]
    o_ref[...] = rt[:, TOP_K:TOP_K + 1] * a_ref[...] + rt[:, TOP_K + 1:TOP_K + 2] * b_ref[...]


def _combine(ys, dest, route):
    assert TOP_K == 2
    n_rows, D = ys.shape
    n_tok = route.shape[0]
    tm = _tile(n_tok, 512, V7X_SUBLANES)
    tc = _tile(D, 512, V7X_LANES)
    while tc > V7X_LANES and n_rows * tc * 4 > 44 * 1024 * 1024:
        tc = _tile(D, tc // 2, V7X_LANES)
    grid_spec = pltpu.PrefetchScalarGridSpec(
        num_scalar_prefetch=1,
        grid=(D // tc, n_tok // tm),
        in_specs=[pl.BlockSpec((n_rows, tc), lambda j, i, d: (0, j), pipeline_mode=pl.Buffered(1)),
                  pl.BlockSpec((tm, V7X_LANES), lambda j, i, d: (i, 0))],
        out_specs=pl.BlockSpec((tm, tc), lambda j, i, d: (i, j)),
        scratch_shapes=[pltpu.VMEM((tm, tc), F32), pltpu.VMEM((tm, tc), F32)],
    )
    return pl.pallas_call(
        functools.partial(_combine_kernel, tm=tm),
        grid_spec=grid_spec,
        out_shape=jax.ShapeDtypeStruct((n_tok, D), F32),
        compiler_params=_cparams(2),
        name="moe_combine",
    )(dest, ys, route)


def _residual_kernel(x_ref, y_ref, g_ref, o_ref):
    o_ref[...] = x_ref[...] + g_ref[...] * y_ref[...]


def _residual_norm_kernel(x_ref, y_ref, g_ref, nf_ref, o_ref):
    x = x_ref[...] + g_ref[...] * y_ref[...]
    ms = jnp.mean(x * x, axis=-1, keepdims=True)
    o_ref[...] = x * lax.rsqrt(ms + EPS) * nf_ref[...]


def _norm_kernel(x_ref, nf_ref, o_ref):
    x = x_ref[...]
    ms = jnp.mean(x * x, axis=-1, keepdims=True)
    o_ref[...] = x * lax.rsqrt(ms + EPS) * nf_ref[...]


def _residual(x, y_tok, row0, gate, norm_g):
    B, L, D = x.shape
    R = gate.shape[1]
    tm = L if R == L else _tile(L, 512, V7X_SUBLANES)
    nb = L // tm
    assert row0 % tm == 0
    blk0 = row0 // tm
    blk = pl.BlockSpec((None, tm, D), lambda b, i: (b, i, 0))
    in_specs = [blk, pl.BlockSpec((tm, D), lambda b, i: (blk0 + b * nb + i, 0)),
                pl.BlockSpec((None, R, D), lambda b, i: (b, 0, 0))]
    args = [x, y_tok, gate]
    kern = _residual_kernel
    if norm_g is not None:
        in_specs.append(pl.BlockSpec((1, D), lambda b, i: (0, 0)))
        args.append(norm_g)
        kern = _residual_norm_kernel
    return pl.pallas_call(
        kern, grid=(B, L // tm), in_specs=in_specs, out_specs=blk,
        out_shape=jax.ShapeDtypeStruct((B, L, D), F32),
        compiler_params=_cparams(2), name="residual",
    )(*args)


def _final_norm(x, norm_g):
    B, L, D = x.shape
    tm = _tile(L, 512, V7X_SUBLANES)
    blk = pl.BlockSpec((None, tm, D), lambda b, i: (b, i, 0))
    return pl.pallas_call(
        _norm_kernel, grid=(B, L // tm),
        in_specs=[blk, pl.BlockSpec((1, D), lambda b, i: (0, 0))], out_specs=blk,
        out_shape=jax.ShapeDtypeStruct((B, L, D), F32),
        compiler_params=_cparams(2), name="final_norm",
    )(x, norm_g)


def _split_mods(m, n_p, n_s):
    D = m.shape[-1] // 6
    mp = [m[:n_p, k * D:(k + 1) * D][:, None, :] for k in range(6)]
    ms = [m[n_p:n_p + n_s, k * D:(k + 1) * D][None, :, :] for k in range(6)]
    return mp, ms


def _stack_cond(c_prompt, c_sample):
    c_all = jnp.concatenate([c_prompt, c_sample], axis=0)
    pad = (-c_all.shape[0]) % V7X_SUBLANES
    if pad:
        c_all = jnp.concatenate([c_all, jnp.zeros((pad, c_all.shape[1]), c_all.dtype)], axis=0)
    return c_all


def _moe(table, route, w1, w3, w2, li, n_experts):
    T = table.shape[0]
    E = n_experts
    tm = MOE_ROW_BLOCK
    e_flat = route[:, :TOP_K].astype(jnp.int32).reshape(-1)
    n_slots = T * TOP_K
    onehot = (e_flat[:, None] == jnp.arange(E, dtype=jnp.int32)[None, :]).astype(jnp.int32)
    csum = jnp.cumsum(onehot, axis=0)
    counts = csum[-1]
    rank = jnp.sum((csum - onehot) * onehot, axis=-1)
    padded = (counts + tm - 1) // tm * tm
    pends = jnp.cumsum(padded)
    pstarts = pends - padded
    rem = counts % tm
    rem_s = jnp.sum(onehot * rem[None, :], axis=-1)
    pos = jnp.where(rank < rem_s, rank, rank + (tm - rem_s) % tm)
    dest = (jnp.sum(onehot * pstarts[None, :], axis=-1) + pos).astype(jnp.int32)
    n_blocks = -(-n_slots // tm) + E
    n_rows = n_blocks * tm
    tok_flat = jnp.arange(n_slots, dtype=jnp.int32) // TOP_K
    src = jnp.zeros((n_rows,), jnp.int32).at[dest].set(tok_flat)
    blk_start = jnp.arange(n_blocks, dtype=jnp.int32) * tm
    blk_e = jnp.minimum(jnp.sum((blk_start[:, None] >= pends[None, :]).astype(jnp.int32), axis=-1), E - 1)
    blk_first = jnp.concatenate([jnp.ones((1,), jnp.int32), (blk_e[1:] != blk_e[:-1]).astype(jnp.int32)])
    rem_b = jnp.sum((blk_e[:, None] == jnp.arange(E, dtype=jnp.int32)[None, :]) * rem[None, :], axis=-1)
    blk_valid = jnp.where(blk_start >= pends[-1], 0,
                          jnp.where(jnp.logical_and(blk_first == 1, rem_b > 0), rem_b, tm)).astype(jnp.int32)
    present = (counts > 0).astype(jnp.int32)
    seg_of_e = jnp.cumsum(present) - 1
    eids = jnp.arange(E, dtype=jnp.int32)
    seg_e = jnp.sum(eids[None, :] * present[None, :] * (seg_of_e[None, :] == eids[:, None]), axis=-1).astype(jnp.int32)
    n_seg = jnp.sum(present, keepdims=True).astype(jnp.int32)
    seg_first = (blk_first * (blk_valid > 0)).astype(jnp.int32)
    seg_idx = jnp.sum((blk_e[:, None] == eids[None, :]) * seg_of_e[None, :], axis=-1).astype(jnp.int32)
    seg = (seg_first, blk_valid, seg_idx, seg_e, n_seg)
    xs = _dispatch(table, src, blk_valid)
    hh = _moe_up(xs, w1, w3, seg, li, E)
    ys = _moe_down(hh, w2, seg, li, E)
    return _combine(ys, dest, route)


def kernel(x_prompt, x_sample, c_prompt, c_sample, cache_k, cache_v, state_ret, state_conv, state_lru, page_table, ada_w_even, ada_b_even, norm_mix_even, norm_ffn_even, w_in_even, lambda_qk, subln_a, gn_b, w_out_even, ffn_w1, ffn_w3, ffn_w2, ada_w_odd, ada_b_odd, norm_mix_odd, norm_ffn_odd, w_in_odd, conv_w, conv_b, w_gate_a, b_gate_a, w_gate_x, b_gate_x, lru_lambda, w_out_odd, router, moe_w1, moe_w3, moe_w2, norm_final):
    Bp, Lp, D = x_prompt.shape
    Bs, Ls, _ = x_sample.shape
    assert Ls == 1, "the sample group carries one new token per sequence"
    n_even, n_odd = w_in_even.shape[0], w_in_odd.shape[0]
    depth = n_even + n_odd
    _, n_phys, page, HA, _, dka = cache_k.shape
    HB, dkb = state_ret.shape[2], state_ret.shape[3]
    past_len = page_table.shape[1] * page
    E = router.shape[-1]
    CW = conv_w.shape[1]
    W = conv_w.shape[-1]

    half = dkb // 2
    freqs = ROPE_BASE ** (-jnp.arange(half, dtype=F32) / half)
    ang_p = jnp.arange(Lp).astype(F32)[:, None] * freqs[None, :]
    ang_s = (past_len + jnp.arange(Ls)).astype(F32)[:, None] * freqs[None, :]
    cos_p, sin_p, cos_s, sin_s = jnp.cos(ang_p), jnp.sin(ang_p), jnp.cos(ang_s), jnp.sin(ang_s)
    log_gamma = jnp.log1p(-jnp.exp2(-5.0 - jnp.arange(HB, dtype=F32)))
    log_gamma = jnp.broadcast_to(log_gamma[:, None, None], (HB, 1, V7X_LANES))

    c_all = _stack_cond(c_prompt, c_sample)
    row3 = lambda a: a.reshape(a.shape[0], 1, a.shape[-1])
    cache_kt = jnp.transpose(cache_k, (0, 1, 3, 4, 5, 2)).reshape(n_even * n_phys, HA * 2 * dka, page)
    cache_v2 = cache_v.reshape(n_even * n_phys, page * HA, 2 * dka)
    subln3 = subln_a.reshape(n_even, HA, 2 * dka)
    state_ret2 = state_ret.reshape(n_even * Bs, HB, dkb, dkb)
    w_ga2 = w_gate_a.reshape(-1, w_gate_a.shape[-2], w_gate_a.shape[-1])
    w_gx2 = w_gate_x.reshape(-1, w_gate_x.shape[-2], w_gate_x.shape[-1])
    moe_w1f = moe_w1.reshape(-1, moe_w1.shape[-2], moe_w1.shape[-1])
    moe_w3f = moe_w3.reshape(-1, moe_w3.shape[-2], moe_w3.shape[-1])
    moe_w2f = moe_w2.reshape(-1, moe_w2.shape[-2], moe_w2.shape[-1])
    router_pad = jnp.concatenate(
        [router, jnp.zeros(router.shape[:-1] + (V7X_LANES - E,), router.dtype)], axis=-1)
    nf = norm_final.reshape(1, D)

    xp = x_prompt
    xs = x_sample.reshape(1, Bs, D)
    kp_l, vp_l, rp_l, cp_l, lp_l = [], [], [], [], []
    ks_l, vs_l, rs_l, cs_l, ls_l = [], [], [], [], []
    normed = False
    for l in range(depth):
        i = l // 2
        last = l == depth - 1
        if l % 2 == 0:
            lam_init = 0.8 - 0.6 * math.exp(-0.3 * l)
            m = _adaln(c_all, ada_w_even, row3(ada_b_even), i)
            (sh1, sc1, g1, sh2, sc2, g2), (sh1s, sc1s, g1s, sh2s, sc2s, g2s) = _split_mods(m, Bp, Bs)
            gmix, gffn = norm_mix_even[i][None, :], norm_ffn_even[i][None, :]
            wa = HA * 2 * dka
            h = _modulate(xp, gmix, sh1, sc1)
            z = _mm_plain(h, w_in_even, i, F32)
            oa, k_t, v_new = _attn_prompt(z, lambda_qk, row3(subln_a), i, HA, lam_init)
            kp_l.append(jnp.transpose(k_t.reshape(Bp, HA, 2, dka, Lp), (0, 4, 1, 2, 3)))
            vp_l.append(v_new.reshape(Bp, Lp, HA, 2 * dka))
            ob, s_fin = _ret_prompt(z, cos_p, sin_p, log_gamma, row3(gn_b), i, HA, HB)
            rp_l.append(s_fin)
            xp = _mm_residual2(oa, ob, w_out_even, i, xp, g1)
            h = _modulate(xp, gffn, sh2, sc2)
            hh = _mm_swiglu(h, ffn_w1, ffn_w3, i)
            xp = _mm_residual(hh, ffn_w2, i, xp, g2)
            h = _modulate(xs, gmix, sh1s, sc1s)
            zs = _mm_plain(h, w_in_even, i, F32)
            zs3 = zs.reshape(Bs, 1, zs.shape[-1])
            ks_l.append(zs3[:, :, wa:2 * wa].reshape(Bs, Ls, HA, 2, dka))
            v_new = zs3[:, :, 2 * wa:3 * wa].reshape(Bs, HA, 2 * dka)
            vs_l.append(v_new.reshape(Bs, Ls, HA, 2 * dka))
            oa = _attn_sample(zs3, v_new, cache_kt, cache_v2, page_table, lambda_qk, subln3, i, HA, lam_init)
            ob, s_new = _ret_sample(zs3, cos_s, sin_s, log_gamma, row3(gn_b), state_ret2, i, HA, HB)
            rs_l.append(s_new)
            xs = _mm_residual2(oa.reshape(1, Bs, wa), ob.reshape(1, Bs, D - wa), w_out_even, i, xs, g1s)
            h = _modulate(xs, gffn, sh2s, sc2s)
            hh = _mm_swiglu(h, ffn_w1, ffn_w3, i)
            xs = _mm_residual(hh, ffn_w2, i, xs, g2s)
        else:
            m = _adaln(c_all, ada_w_odd, row3(ada_b_odd), i)
            (sh1, sc1, g1, sh2, sc2, g2), (sh1s, sc1s, g1s, sh2s, sc2s, g2s) = _split_mods(m, Bp, Bs)
            gmix, gffn = norm_mix_odd[i][None, :], norm_ffn_odd[i][None, :]
            lru_args = (conv_w, row3(conv_b), w_ga2, w_gx2, row3(b_gate_a), row3(b_gate_x), row3(lru_lambda))
            h = _modulate(xp, gmix, sh1, sc1)
            z = _mm_plain(h, w_in_odd, i, F32)
            tail0 = jnp.zeros((Bp, V7X_SUBLANES, W), F32)
            h0 = jnp.zeros((Bp, 1, W), F32)
            y, h_last = _lru_prompt(z, *lru_args, tail0, h0, i)
            cp_l.append(z[:, Lp - (CW - 1):, W:])
            lp_l.append(h_last.reshape(Bp, W))
            xp = _mm_residual(y, w_out_odd, i, xp, g1)
            h = _modulate(xs, gmix, sh1s, sc1s)
            zs = _mm_plain(h, w_in_odd, i, F32).reshape(Bs, 2 * W)
            buf = state_conv[i]
            ys, h_new = _lru_sample(zs, *lru_args, jnp.swapaxes(buf, 0, 1), state_lru[i], i)
            cs_l.append(jnp.concatenate([buf[:, 1:], zs[:, None, W:]], axis=1))
            ls_l.append(h_new)
            xs = _mm_residual(ys.reshape(1, Bs, W), w_out_odd, i, xs, g1s)
            Tp = Bp * Lp
            table, route = _modulate_route(xp, gffn, sh2, sc2, xs[0], sh2s[0], sc2s[0], router_pad[i], E)
            y_tok = _moe(table, route, moe_w1f, moe_w3f, moe_w2f, i, E)
            xp = _residual(xp, y_tok, 0, g2, nf if last else None)
            xs = _residual(xs, y_tok, Tp, g2s, nf if last else None)
            normed = last
    if not normed:
        xp = _final_norm(xp, nf)
        xs = _final_norm(xs, nf)
    y_prompt = xp.astype(x_prompt.dtype)
    y_sample = xs.reshape(Bs, Ls, D).astype(x_sample.dtype)
    return (y_prompt, y_sample,
            jnp.stack(kp_l), jnp.stack(vp_l), jnp.stack(rp_l), jnp.stack(cp_l), jnp.stack(lp_l),
            jnp.stack(ks_l), jnp.stack(vs_l), jnp.stack(rs_l), jnp.stack(cs_l), jnp.stack(ls_l))
```

```python
import functools
import math

import jax
import jax.numpy as jnp
from jax import lax
from jax.experimental import pallas as pl
from jax.experimental.pallas import tpu as pltpu

F32 = jnp.float32
BF16 = jnp.bfloat16

EPS = 1e-6
ROPE_BASE = 10000.0
LRU_C = 8.0
TOP_K = 2
NEG_BIG = -1e30

V7X_LANES = 128
V7X_SUBLANES = 8
V7X_VMEM_LIMIT_BYTES = 56 * 1024 * 1024


def _cparams(n_axes):
    return pltpu.CompilerParams(dimension_semantics=("arbitrary",) * n_axes,
                                vmem_limit_bytes=V7X_VMEM_LIMIT_BYTES)


def _tile(n, pref, align):
    if n <= pref:
        return n
    t = (pref // align) * align
    while t > align and n % t:
        t -= align
    assert n % t == 0, (n, pref, align)
    return t


def _iota(shape, dim):
    return lax.broadcasted_iota(jnp.int32, shape, dim)


def _adaln_kernel(c_ref, w_ref, b_ref, o_ref):
    c = c_ref[...]
    s = (c * jax.nn.sigmoid(c)).astype(BF16)
    o_ref[...] = jnp.dot(s, w_ref[...].astype(BF16), preferred_element_type=F32) + b_ref[...]


def _adaln(c_all, w, b, li):
    R, D = c_all.shape
    N = w.shape[-1]
    tn = _tile(N, 1024, V7X_LANES)
    return pl.pallas_call(
        _adaln_kernel,
        grid=(N // tn,),
        in_specs=[pl.BlockSpec((R, D), lambda j: (0, 0)),
                  pl.BlockSpec((None, D, tn), lambda j: (li, 0, j)),
                  pl.BlockSpec((None, 1, tn), lambda j: (li, 0, j))],
        out_specs=pl.BlockSpec((R, tn), lambda j: (0, j)),
        out_shape=jax.ShapeDtypeStruct((R, N), F32),
        compiler_params=_cparams(1),
        name="adaln",
    )(c_all, w, b)


def _modulated(x_ref, g_ref, sh_ref, sc_ref):
    x = x_ref[...]
    ms = jnp.mean(x * x, axis=-1, keepdims=True)
    return x * lax.rsqrt(ms + EPS) * g_ref[...] * (1.0 + sc_ref[...]) + sh_ref[...]


def _modulate_kernel(x_ref, g_ref, sh_ref, sc_ref, o_ref):
    o_ref[...] = _modulated(x_ref, g_ref, sh_ref, sc_ref).astype(o_ref.dtype)


def _top2_route(h, wr_ref, n_experts):
    w = wr_ref[...]
    h_hi, w_hi = h.astype(BF16), w.astype(BF16)
    h_lo = (h - h_hi.astype(F32)).astype(BF16)
    w_lo = (w - w_hi.astype(F32)).astype(BF16)
    logits = (jnp.dot(h_hi, w_hi, preferred_element_type=F32) + jnp.dot(h_hi, w_lo, preferred_element_type=F32)
              + jnp.dot(h_lo, w_hi, preferred_element_type=F32))
    lane = _iota(logits.shape, 1)
    logits = jnp.where(lane < n_experts, logits, NEG_BIG)
    m1 = jnp.max(logits, axis=-1, keepdims=True)
    i1 = jnp.min(jnp.where(logits == m1, lane, V7X_LANES), axis=-1, keepdims=True)
    rest = jnp.where(lane == i1, NEG_BIG, logits)
    m2 = jnp.max(rest, axis=-1, keepdims=True)
    i2 = jnp.min(jnp.where(rest == m2, lane, V7X_LANES), axis=-1, keepdims=True)
    e2 = jnp.exp(m2 - m1)
    g1 = 1.0 / (1.0 + e2)
    g2 = e2 / (1.0 + e2)
    return jnp.where(lane == 0, i1.astype(F32),
                     jnp.where(lane == 1, i2.astype(F32),
                               jnp.where(lane == 2, g1, jnp.where(lane == 3, g2, 0.0))))


def _modulate_route_kernel(x_ref, g_ref, sh_ref, sc_ref, xs_ref, shs_ref, scs_ref, wr_ref, o_ref, r_ref,
                           *, n_experts, n_prompt_steps):
    t = pl.program_id(0)

    @pl.when(t < n_prompt_steps)
    def _():
        h = _modulated(x_ref, g_ref, sh_ref, sc_ref)
        o_ref[...] = h
        r_ref[...] = _top2_route(h, wr_ref, n_experts)

    @pl.when(t == n_prompt_steps)
    def _():
        h = _modulated(xs_ref, g_ref, shs_ref, scs_ref)
        n = h.shape[0]
        o_ref[:n, :] = h
        r_ref[:n, :] = _top2_route(h, wr_ref, n_experts)


def _mod_specs(B, L, D, R, tm):
    xmap = lambda b, i: (b, i, 0)
    return [pl.BlockSpec((None, tm, D), xmap),
            pl.BlockSpec((1, D), lambda b, i: (0, 0)),
            pl.BlockSpec((None, R, D), lambda b, i: (b, 0, 0)),
            pl.BlockSpec((None, R, D), lambda b, i: (b, 0, 0))]


def _modulate(x, g, shift, scale):
    B, L, D = x.shape
    R = shift.shape[1]
    tm = L if R == L else _tile(L, 512, V7X_SUBLANES)
    return pl.pallas_call(
        _modulate_kernel,
        grid=(B, L // tm),
        in_specs=_mod_specs(B, L, D, R, tm),
        out_specs=pl.BlockSpec((None, tm, D), lambda b, i: (b, i, 0)),
        out_shape=jax.ShapeDtypeStruct((B, L, D), BF16),
        compiler_params=_cparams(2),
        name="modulate",
    )(x, g, shift, scale)


def _modulate_route(x, g, shift, scale, xs, shift_s, scale_s, w_router_padded, n_experts):
    B, L, D = x.shape
    Bs = xs.shape[0]
    tm = _tile(L, 512, V7X_SUBLANES)
    assert Bs <= tm
    nb = L // tm
    n_p = B * nb
    n_tok = B * L + Bs
    bmap = lambda t: jnp.minimum(t // nb, B - 1)
    imap = lambda t: jnp.where(t < n_p, t % nb, nb - 1)
    whole = lambda t: (0, 0)
    return pl.pallas_call(
        functools.partial(_modulate_route_kernel, n_experts=n_experts, n_prompt_steps=n_p),
        grid=(n_p + 1,),
        in_specs=[pl.BlockSpec((None, tm, D), lambda t: (bmap(t), imap(t), 0)),
                  pl.BlockSpec((1, D), whole),
                  pl.BlockSpec((None, 1, D), lambda t: (bmap(t), 0, 0)),
                  pl.BlockSpec((None, 1, D), lambda t: (bmap(t), 0, 0)),
                  pl.BlockSpec((Bs, D), whole), pl.BlockSpec((Bs, D), whole), pl.BlockSpec((Bs, D), whole),
                  pl.BlockSpec((D, V7X_LANES), whole)],
        out_specs=[pl.BlockSpec((tm, D), lambda t: (t, 0)), pl.BlockSpec((tm, V7X_LANES), lambda t: (t, 0))],
        out_shape=[jax.ShapeDtypeStruct((n_tok, D), F32),
                   jax.ShapeDtypeStruct((n_tok, V7X_LANES), F32)],
        compiler_params=_cparams(1),
        name="modulate_route",
    )(x, g, shift, scale, xs, shift_s, scale_s, w_router_padded)


def _cast_weights_once(first, w_refs, wb_refs):
    @pl.when(first)
    def _():
        for w_ref, wb_ref in zip(w_refs, wb_refs):
            wb_ref[...] = w_ref[...].astype(BF16)


def _dense_kernel(*refs, swiglu, n_a, n_w, residual, n_prompt_steps):
    it = iter(refs)
    take = lambda n: [next(it) for _ in range(n)]
    a_refs, xg_refs, w_refs = take(n_a), take(2 if residual else 0), take(n_w)
    as_refs, xgs_refs = take(n_a), take(2 if residual else 0)
    o_ref, os_ref = take(2)
    wb_refs = take(n_w)
    t = pl.program_id(1)
    _cast_weights_once(t == 0, w_refs, wb_refs)

    def run(a_refs, xg_refs, o_ref):
        if swiglu:
            a = a_refs[0][...]
            u = jnp.dot(a, wb_refs[0][...], preferred_element_type=F32)
            v = jnp.dot(a, wb_refs[1][...], preferred_element_type=F32)
            y = u * jax.nn.sigmoid(u) * v
        else:
            y, k0 = None, 0
            for a_ref in a_refs:
                k1 = k0 + a_ref.shape[-1]
                part = jnp.dot(a_ref[...], wb_refs[0][k0:k1, :], preferred_element_type=F32)
                y = part if y is None else y + part
                k0 = k1
        if residual:
            y = xg_refs[0][...] + xg_refs[1][...] * y
        o_ref[...] = y.astype(o_ref.dtype)

    pl.when(t < n_prompt_steps)(lambda: run(a_refs, xg_refs, o_ref))
    pl.when(t == n_prompt_steps)(lambda: run(as_refs, xgs_refs, os_ref))


def _mm_tiles(L, K, N, n_w):
    budget = V7X_VMEM_LIMIT_BYTES // 2
    tn = _tile(N, 1024, V7X_LANES)
    while tn > V7X_LANES and n_w * K * tn * (2 * 4 + 2) > budget:
        tn = _tile(N, tn // 2, V7X_LANES)
    tm = _tile(L, 1024 if K <= 2048 else 512, V7X_SUBLANES)
    return tm, tn


def _dense(name, acts, weights, li, out_dtype, sample_acts, resid=None, sample_resid=None):
    B, L = acts[0].shape[:2]
    Bs = sample_acts[0].shape[0]
    K = sum(a.shape[-1] for a in acts)
    N = weights[0].shape[-1]
    tm, tn = _mm_tiles(L, K, N, len(weights))
    nb = L // tm
    n_p = B * nb
    bmap = lambda t: jnp.minimum(t // nb, B - 1)
    imap = lambda t: jnp.where(t < n_p, t % nb, nb - 1)
    row_spec = lambda width: pl.BlockSpec((None, tm, width), lambda j, t: (bmap(t), imap(t), 0))
    col_spec = pl.BlockSpec((None, tm, tn), lambda j, t: (bmap(t), imap(t), j))
    in_specs = [row_spec(a.shape[-1]) for a in acts]
    args = list(acts)
    if resid is not None:
        in_specs += [col_spec, pl.BlockSpec((None, 1, tn), lambda j, t: (bmap(t), 0, j))]
        args += list(resid)
    in_specs += [pl.BlockSpec((None, K, tn), lambda j, t: (li, 0, j)) for _ in weights]
    args += list(weights)
    in_specs += [pl.BlockSpec((Bs, a.shape[-1]), lambda j, t: (0, 0)) for a in sample_acts]
    args += list(sample_acts)
    if resid is not None:
        in_specs += [pl.BlockSpec((Bs, tn), lambda j, t: (0, j)), pl.BlockSpec((Bs, tn), lambda j, t: (0, j))]
        args += list(sample_resid)
    return pl.pallas_call(
        functools.partial(_dense_kernel, swiglu=len(weights) == 2, n_a=len(acts), n_w=len(weights),
                          residual=resid is not None, n_prompt_steps=n_p),
        grid=(N // tn, n_p + 1),
        in_specs=in_specs,
        out_specs=[col_spec, pl.BlockSpec((Bs, tn), lambda j, t: (0, j))],
        out_shape=[jax.ShapeDtypeStruct((B, L, N), out_dtype), jax.ShapeDtypeStruct((Bs, N), out_dtype)],
        scratch_shapes=[pltpu.VMEM((K, tn), BF16) for _ in weights],
        compiler_params=_cparams(2),
        name=name,
    )(*args)


def _diff_lambda(lq_ref, lam_init):
    lq = lq_ref[...]
    a = jnp.sum(lq[0:1] * lq[1:2], axis=-1, keepdims=True)
    b = jnp.sum(lq[2:3] * lq[3:4], axis=-1, keepdims=True)
    return jnp.exp(a) - jnp.exp(b) + lam_init


def _attn_prompt_kernel(q_ref, k_ref, v_ref, lq_ref, sub_ref, o_ref, kt_ref, vo_ref, *, tq, tk, dk, lam_init):
    qi = pl.program_id(2)

    @pl.when(qi == 0)
    def _():
        kt_ref[...] = k_ref[...].T
        vo_ref[...] = v_ref[...]

    hw = 2 * dk
    sc = dk ** -0.5
    q = q_ref[...] * sc
    lane = _iota(q.shape, 1)
    qq = jnp.concatenate([jnp.where(lane < dk, q, 0.0), jnp.where(lane >= dk, q, 0.0)], axis=0).astype(BF16)

    def block(j, carry, masked):
        m, l, acc = carry
        start = pl.multiple_of(j * tk, tk)
        kb = k_ref[pl.ds(start, tk), :].astype(BF16)
        vb = v_ref[pl.ds(start, tk), :].astype(BF16)
        s = lax.dot_general(qq, kb, (((1,), (1,)), ((), ())), preferred_element_type=F32)
        if masked:
            row = _iota(s.shape, 0)
            row = jnp.where(row >= tq, row - tq, row) + qi * tq
            s = jnp.where(_iota(s.shape, 1) + j * tk <= row, s, NEG_BIG)
        m_new = jnp.maximum(m, jnp.max(s, axis=-1, keepdims=True))
        alpha = jnp.exp(m - m_new)
        p = jnp.exp(s - m_new)
        l = alpha * l + jnp.sum(p, axis=-1, keepdims=True)
        acc = alpha * acc + jnp.dot(p.astype(BF16), vb, preferred_element_type=F32)
        return m_new, l, acc

    init = (jnp.full((2 * tq, 1), NEG_BIG, F32), jnp.zeros((2 * tq, 1), F32), jnp.zeros((2 * tq, hw), F32))
    n_full = (qi * tq) // tk
    carry = lax.fori_loop(0, n_full, lambda j, c: block(j, c, False), init)
    m, l, acc = block(n_full, carry, True)
    on = acc / l
    lam = _diff_lambda(lq_ref, lam_init)
    o = on[:tq] - lam * on[tq:]
    ms = jnp.mean(o * o, axis=-1, keepdims=True)
    o_ref[...] = (o * lax.rsqrt(ms + EPS) * sub_ref[...] * (1.0 - lam_init)).astype(o_ref.dtype)


def _attn_prompt(z, lambda_qk, subln, li, n_heads, lam_init):
    B, L, _ = z.shape
    hw = 2 * lambda_qk.shape[-1]
    tq = _tile(L, 512, V7X_SUBLANES)
    tk = _tile(L, 512, V7X_SUBLANES)
    assert tk % tq == 0
    H = n_heads
    return pl.pallas_call(
        functools.partial(_attn_prompt_kernel, tq=tq, tk=tk, dk=hw // 2, lam_init=lam_init),
        grid=(B, H, L // tq),
        in_specs=[pl.BlockSpec((None, tq, hw), lambda b, h, i: (b, i, h)),
                  pl.BlockSpec((None, L, hw), lambda b, h, i: (b, 0, H + h)),
                  pl.BlockSpec((None, L, hw), lambda b, h, i: (b, 0, 2 * H + h)),
                  pl.BlockSpec((None,) + lambda_qk.shape[1:], lambda b, h, i: (li, 0, 0)),
                  pl.BlockSpec((None, 1, hw), lambda b, h, i: (li, 0, h))],
        out_specs=[pl.BlockSpec((None, tq, hw), lambda b, h, i: (b, i, h)),
                   pl.BlockSpec((None, hw, L), lambda b, h, i: (b, h, 0)),
                   pl.BlockSpec((None, L, hw), lambda b, h, i: (b, 0, h))],
        out_shape=[jax.ShapeDtypeStruct((B, L, H * hw), BF16),
                   jax.ShapeDtypeStruct((B, H * hw, L), F32),
                   jax.ShapeDtypeStruct((B, L, H * hw), F32)],
        compiler_params=_cparams(3),
        name="attn_prompt",
    )(z, z, z, lambda_qk, subln)


def _attn_sample_kernel(pt_ref, q_ref, kn_ref, vn_ref, lq_ref, sub_ref, *rest, n_pages, n_heads, dk, lam_init):
    kt_refs = rest[:n_pages]
    v_refs = rest[n_pages:2 * n_pages]
    o_ref, m_ref, l_ref, acc_ref = rest[2 * n_pages:]
    s_id = pl.program_id(1)
    H = n_heads
    hw = 2 * dk
    W = H * hw
    page = kt_refs[0].shape[-1]
    sc = dk ** -0.5

    @pl.when(s_id == 0)
    def _():
        m_ref[...] = jnp.full(m_ref.shape, NEG_BIG, F32)
        l_ref[...] = jnp.zeros(l_ref.shape, F32)
        acc_ref[...] = jnp.zeros(acc_ref.shape, F32)

    row = _iota((2 * H, W), 0)
    lane = _iota((2 * H, W), 1)
    r_head = jnp.where(row >= H, row - H, row)
    r_comp = jnp.where(row >= H, 1, 0)
    qsel = jnp.logical_and(lane // hw == r_head, (lane // dk) % 2 == r_comp)
    qf = jnp.where(qsel, q_ref[...] * sc, 0.0)
    qb = qf.astype(BF16)

    s_parts = [jnp.dot(qb, kt_refs[r][...].astype(BF16), preferred_element_type=F32) for r in range(n_pages)]
    s = jnp.concatenate(s_parts, axis=-1) if n_pages > 1 else s_parts[0]
    m = m_ref[...]
    m_new = jnp.maximum(m, jnp.max(s, axis=-1, keepdims=True))
    alpha = jnp.exp(m - m_new)
    p = jnp.exp(s - m_new)
    l_ref[...] = alpha * l_ref[...] + jnp.sum(p, axis=-1, keepdims=True)
    m_ref[...] = m_new
    p_head = r_head[:, :page]
    pv = jnp.zeros((2 * H, hw), F32)
    for r in range(n_pages):
        pr = p[:, r * page:(r + 1) * page]
        pexp = jnp.concatenate([jnp.where(p_head == h, pr, 0.0) for h in range(H)], axis=1).astype(BF16)
        vh = jnp.concatenate([v_refs[r][pl.ds(h, page, stride=H), :] for h in range(H)], axis=0).astype(BF16)
        pv = pv + jnp.dot(pexp, vh, preferred_element_type=F32)
    acc_ref[...] = alpha * acc_ref[...] + pv

    @pl.when(s_id == pl.num_programs(1) - 1)
    def _():
        s_new = jnp.sum(qf * kn_ref[...], axis=-1, keepdims=True)
        m_old = m_ref[...]
        m_fin = jnp.maximum(m_old, s_new)
        a_fin = jnp.exp(m_old - m_fin)
        p_new = jnp.exp(s_new - m_fin)
        l_fin = a_fin * l_ref[...] + p_new
        vn = vn_ref[...]
        acc = a_fin * acc_ref[...] + p_new * jnp.concatenate([vn, vn], axis=0)
        on = acc / l_fin
        lam = _diff_lambda(lq_ref, lam_init)
        o = on[:H] - lam * on[H:]
        ms = jnp.mean(o * o, axis=-1, keepdims=True)
        o_ref[...] = (o * lax.rsqrt(ms + EPS) * sub_ref[...] * (1.0 - lam_init)).astype(o_ref.dtype)


def _attn_sample(z, v_new, cache_kt, cache_v, page_table, lambda_qk, subln, li, n_heads, lam_init):
    Bs = z.shape[0]
    n_phys_total, W, page = cache_kt.shape
    n_pg = page_table.shape[1]
    P = next(c for c in (8, 4, 2, 1) if n_pg % c == 0)
    n_phys = n_phys_total // lambda_qk.shape[0]
    base = li * n_phys
    H = n_heads
    dk = lambda_qk.shape[-1]
    hw = 2 * dk

    def page_spec(shape, r):
        return pl.BlockSpec((None,) + shape, lambda b, s, pt: (base + pt[b, s * P + r], 0, 0))

    zrow = lambda c: pl.BlockSpec((None, 1, W), lambda b, s, pt: (b, 0, c))
    grid_spec = pltpu.PrefetchScalarGridSpec(
        num_scalar_prefetch=1,
        grid=(Bs, n_pg // P),
        in_specs=[zrow(0), zrow(1),
                  pl.BlockSpec((None, H, hw), lambda b, s, pt: (b, 0, 0)),
                  pl.BlockSpec((None,) + lambda_qk.shape[1:], lambda b, s, pt: (li, 0, 0)),
                  pl.BlockSpec((None, H, hw), lambda b, s, pt: (li, 0, 0))]
                 + [page_spec((W, page), r) for r in range(P)]
                 + [page_spec((page * H, hw), r) for r in range(P)],
        out_specs=pl.BlockSpec((None, H, hw), lambda b, s, pt: (b, 0, 0)),
        scratch_shapes=[pltpu.VMEM((2 * H, 1), F32), pltpu.VMEM((2 * H, 1), F32),
                        pltpu.VMEM((2 * H, hw), F32)],
    )
    return pl.pallas_call(
        functools.partial(_attn_sample_kernel, n_pages=P, n_heads=H, dk=dk, lam_init=lam_init),
        grid_spec=grid_spec,
        out_shape=jax.ShapeDtypeStruct((Bs, H, hw), BF16),
        compiler_params=_cparams(2),
        name="attn_sample",
    )(page_table, z, z, v_new, lambda_qk, subln, *([cache_kt] * P), *([cache_v] * P))


def _rope(x, cos, sin):
    half = x.shape[-1] // 2
    x1, x2 = x[:, :half], x[:, half:]
    return jnp.concatenate([x1 * cos - x2 * sin, x1 * sin + x2 * cos], axis=-1)


def _groupnorm_gate(o, gn, g):
    mu = jnp.mean(o, axis=-1, keepdims=True)
    d = o - mu
    var = jnp.mean(d * d, axis=-1, keepdims=True)
    return d * lax.rsqrt(var + EPS) * gn * (g * jax.nn.sigmoid(g))


def _ret_prompt_kernel(q_ref, k_ref, v_ref, g_ref, cos_ref, sin_ref, lg_ref, gn_ref,
                       o_ref, sfin_ref, s_ref, *, chunk, dk):
    c = pl.program_id(2)

    @pl.when(c == 0)
    def _():
        s_ref[...] = jnp.zeros(s_ref.shape, F32)

    cos, sin = cos_ref[...], sin_ref[...]
    qr = _rope(q_ref[...], cos, sin) * dk ** -0.5
    kr = _rope(k_ref[...], cos, sin)
    vb = v_ref[...].astype(BF16)
    lg = lg_ref[...][:, :1]
    C = chunk
    d = _iota((C, C), 0) - _iota((C, C), 1)
    decay = jnp.where(d >= 0, jnp.exp(jnp.maximum(d, 0).astype(F32) * lg), 0.0)
    qrb = qr.astype(BF16)
    scores = lax.dot_general(qrb, kr.astype(BF16), (((1,), (1,)), ((), ())), preferred_element_type=F32) * decay
    intra = jnp.dot(scores.astype(BF16), vb, preferred_element_type=F32)
    t = _iota((C, 1), 0).astype(F32)
    s_old = s_ref[...]
    inter = jnp.dot(qrb, s_old.astype(BF16), preferred_element_type=F32) * jnp.exp((t + 1.0) * lg)
    kd = kr * jnp.exp((C - 1.0 - t) * lg)
    s_new = jnp.exp(C * lg) * s_old + jnp.dot(kd.T.astype(BF16), vb, preferred_element_type=F32)
    s_ref[...] = s_new
    sfin_ref[...] = s_new
    o_ref[...] = _groupnorm_gate(intra + inter, gn_ref[...], g_ref[...]).astype(o_ref.dtype)


def _ret_prompt(z, cos_tab, sin_tab, log_gamma, gn_b, li, n_heads_a, n_heads_b):
    B, L, _ = z.shape
    dk = 2 * cos_tab.shape[-1]
    HB = n_heads_b
    assert (3 * n_heads_a * V7X_LANES) % dk == 0
    off = 3 * n_heads_a * V7X_LANES // dk
    C = _tile(L, 512, V7X_SUBLANES)
    zb = lambda g: pl.BlockSpec((None, C, dk), lambda b, h, c: (b, c, off + g * HB + h))
    return pl.pallas_call(
        functools.partial(_ret_prompt_kernel, chunk=C, dk=dk),
        grid=(B, HB, L // C),
        in_specs=[zb(0), zb(1), zb(2), zb(3),
                  pl.BlockSpec((C, dk // 2), lambda b, h, c: (c, 0)),
                  pl.BlockSpec((C, dk // 2), lambda b, h, c: (c, 0)),
                  pl.BlockSpec((None, 1, V7X_LANES), lambda b, h, c: (h, 0, 0)),
                  pl.BlockSpec((None, 1, dk), lambda b, h, c: (li, 0, h))],
        out_specs=[pl.BlockSpec((None, C, dk), lambda b, h, c: (b, c, h)),
                   pl.BlockSpec((None, None, dk, dk), lambda b, h, c: (b, h, 0, 0))],
        out_shape=[jax.ShapeDtypeStruct((B, L, HB * dk), BF16),
                   jax.ShapeDtypeStruct((B, HB, dk, dk), F32)],
        scratch_shapes=[pltpu.VMEM((dk, dk), F32)],
        compiler_params=_cparams(3),
        name="ret_prompt",
    )(z, z, z, z, cos_tab, sin_tab, log_gamma, gn_b)


def _ret_sample_kernel(q_ref, k_ref, v_ref, g_ref, cos_ref, sin_ref, lg_ref, gn_ref, s0_ref,
                       o_ref, snew_ref, *, dk):
    cos, sin = cos_ref[...], sin_ref[...]
    qr = _rope(q_ref[...], cos, sin) * dk ** -0.5
    kr = _rope(k_ref[...], cos, sin)
    v = v_ref[...]
    gamma = jnp.exp(lg_ref[...][:, :1])
    eye = _iota((dk, dk), 0) == _iota((dk, dk), 1)
    kcol = jnp.sum(jnp.where(eye, kr, 0.0), axis=-1, keepdims=True)
    qcol = jnp.sum(jnp.where(eye, qr, 0.0), axis=-1, keepdims=True)
    s0 = s0_ref[...]
    snew_ref[...] = gamma * s0 + kcol * v
    inter = gamma * jnp.sum(qcol * s0, axis=0, keepdims=True)
    intra = jnp.sum(qr * kr, axis=-1, keepdims=True) * v
    o_ref[...] = _groupnorm_gate(intra + inter, gn_ref[...], g_ref[...]).astype(o_ref.dtype)


def _ret_sample(z, cos_row, sin_row, log_gamma, gn_b, state, li, n_heads_a, n_heads_b):
    Bs = z.shape[0]
    dk = 2 * cos_row.shape[-1]
    HB = n_heads_b
    off = 3 * n_heads_a * V7X_LANES // dk
    zb = lambda g: pl.BlockSpec((None, 1, dk), lambda b, h: (b, 0, off + g * HB + h))
    return pl.pallas_call(
        functools.partial(_ret_sample_kernel, dk=dk),
        grid=(Bs, HB),
        in_specs=[zb(0), zb(1), zb(2), zb(3),
                  pl.BlockSpec((1, dk // 2), lambda b, h: (0, 0)),
                  pl.BlockSpec((1, dk // 2), lambda b, h: (0, 0)),
                  pl.BlockSpec((None, 1, V7X_LANES), lambda b, h: (h, 0, 0)),
                  pl.BlockSpec((None, 1, dk), lambda b, h: (li, 0, h)),
                  pl.BlockSpec((None, None, dk, dk), lambda b, h: (li * Bs + b, h, 0, 0))],
        out_specs=[pl.BlockSpec((None, 1, dk), lambda b, h: (b, 0, h)),
                   pl.BlockSpec((None, None, dk, dk), lambda b, h: (b, h, 0, 0))],
        out_shape=[jax.ShapeDtypeStruct((Bs, 1, HB * dk), BF16),
                   jax.ShapeDtypeStruct((Bs, HB, dk, dk), F32)],
        compiler_params=_cparams(2),
        name="ret_sample",
    )(z, z, z, z, cos_row, sin_row, log_gamma, gn_b, state)


def _softplus(z):
    return jnp.maximum(z, 0.0) + jnp.log1p(jnp.exp(-jnp.abs(z)))


def _lru_coeffs(xc, wa_ref, wx_ref, ba_ref, bx_ref, lam_ref):
    xcb = xc.astype(BF16)
    r = jax.nn.sigmoid(jnp.dot(xcb, wa_ref[...].astype(BF16), preferred_element_type=F32) + ba_ref[...])
    gi = jax.nn.sigmoid(jnp.dot(xcb, wx_ref[...].astype(BF16), preferred_element_type=F32) + bx_ref[...])
    log_a = -LRU_C * r * _softplus(-lam_ref[...])
    a = jnp.exp(log_a)
    t = jnp.tanh(log_a)
    mult = jnp.sqrt(jnp.maximum(-2.0 * t / (1.0 - t), 0.0))
    return a, mult * gi * xc


def _lru_prompt_kernel(gate_ref, x_ref, cw_ref, cb_ref, wa_ref, wx_ref, ba_ref, bx_ref, lam_ref,
                       tail0_ref, h0_ref, y_ref, hl_ref, tail_ref, hc_ref, *, tt, conv_w):
    t_id = pl.program_id(2)

    @pl.when(t_id == 0)
    def _():
        tail_ref[...] = tail0_ref[...]
        hc_ref[...] = h0_ref[...]

    x = x_ref[...]
    tl = tail_ref[...]
    row8 = _iota(tl.shape, 0)
    xc = cb_ref[...] + cw_ref[conv_w - 1:conv_w, :] * x
    for s in range(1, conv_w):
        xs = pltpu.roll(x, s, 0)
        head = jnp.where(row8 >= s, xs[:V7X_SUBLANES], pltpu.roll(tl, s, 0))
        xs = jnp.concatenate([head, xs[V7X_SUBLANES:]], axis=0) if tt > V7X_SUBLANES else head
        xc = xc + cw_ref[conv_w - 1 - s:conv_w - s, :] * xs
    tail_ref[...] = x[tt - V7X_SUBLANES:, :]

    a, b = _lru_coeffs(xc, wa_ref, wx_ref, ba_ref, bx_ref, lam_ref)
    row = _iota(a.shape, 0)
    s = 1
    while s < tt:
        if s % V7X_SUBLANES == 0:
            a_s = jnp.concatenate([jnp.ones((s, a.shape[1]), F32), a[:tt - s]], axis=0)
            b_s = jnp.concatenate([jnp.zeros((s, a.shape[1]), F32), b[:tt - s]], axis=0)
        else:
            keep = row >= s
            a_s = jnp.where(keep, pltpu.roll(a, s, 0), 1.0)
            b_s = jnp.where(keep, pltpu.roll(b, s, 0), 0.0)
        b = b + a * b_s
        a = a * a_s
        s *= 2
    h = b + a * hc_ref[...]
    hc_ref[...] = h[tt - 1:tt, :]
    hl_ref[...] = h[tt - 1:tt, :]
    y_ref[...] = (jax.nn.gelu(gate_ref[...], approximate=True) * h).astype(y_ref.dtype)


def _lru_prompt(z, conv_w, conv_b, w_ga, w_gx, b_ga, b_gx, lam, tail0, h0, li):
    B, L, W2 = z.shape
    W = W2 // 2
    bw = w_ga.shape[-1]
    NB = W // bw
    CW = conv_w.shape[1]
    tt = _tile(L, 512, V7X_SUBLANES)
    vec = lambda: pl.BlockSpec((None, 1, bw), lambda b, n, t: (li, 0, n))
    gw = lambda: pl.BlockSpec((None, bw, bw), lambda b, n, t: (li * NB + n, 0, 0))
    return pl.pallas_call(
        functools.partial(_lru_prompt_kernel, tt=tt, conv_w=CW),
        grid=(B, NB, L // tt),
        in_specs=[pl.BlockSpec((None, tt, bw), lambda b, n, t: (b, t, n)),
                  pl.BlockSpec((None, tt, bw), lambda b, n, t: (b, t, NB + n)),
                  pl.BlockSpec((None, CW, bw), lambda b, n, t: (li, 0, n)),
                  vec(), gw(), gw(), vec(), vec(), vec(),
                  pl.BlockSpec((None, V7X_SUBLANES, bw), lambda b, n, t: (b, 0, n)),
                  pl.BlockSpec((None, 1, bw), lambda b, n, t: (b, 0, n))],
        out_specs=[pl.BlockSpec((None, tt, bw), lambda b, n, t: (b, t, n)),
                   pl.BlockSpec((None, 1, bw), lambda b, n, t: (b, 0, n))],
        out_shape=[jax.ShapeDtypeStruct((B, L, W), BF16), jax.ShapeDtypeStruct((B, 1, W), F32)],
        scratch_shapes=[pltpu.VMEM((V7X_SUBLANES, bw), F32), pltpu.VMEM((1, bw), F32)],
        compiler_params=_cparams(3),
        name="lru_prompt",
    )(z, z, conv_w, conv_b, w_ga, w_gx, b_ga, b_gx, lam, tail0, h0)


def _lru_sample_kernel(gate_ref, x_ref, cw_ref, cb_ref, wa_ref, wx_ref, ba_ref, bx_ref, lam_ref,
                       buf_ref, h0_ref, y_ref, h_ref, *, conv_w):
    x = x_ref[...]
    xc = cb_ref[...] + cw_ref[conv_w - 1:conv_w, :] * x
    for j in range(conv_w - 1):
        xc = xc + cw_ref[j:j + 1, :] * buf_ref[j]
    a, b = _lru_coeffs(xc, wa_ref, wx_ref, ba_ref, bx_ref, lam_ref)
    h = a * h0_ref[...] + b
    h_ref[...] = h
    y_ref[...] = (jax.nn.gelu(gate_ref[...], approximate=True) * h).astype(y_ref.dtype)


def _lru_sample(z, conv_w, conv_b, w_ga, w_gx, b_ga, b_gx, lam, buf, h0, li):
    Bs, W2 = z.shape
    W = W2 // 2
    bw = w_ga.shape[-1]
    NB = W // bw
    CW = conv_w.shape[1]
    vec = lambda: pl.BlockSpec((None, 1, bw), lambda n: (li, 0, n))
    gw = lambda: pl.BlockSpec((None, bw, bw), lambda n: (li * NB + n, 0, 0))
    return pl.pallas_call(
        functools.partial(_lru_sample_kernel, conv_w=CW),
        grid=(NB,),
        in_specs=[pl.BlockSpec((Bs, bw), lambda n: (0, n)),
                  pl.BlockSpec((Bs, bw), lambda n: (0, NB + n)),
                  pl.BlockSpec((None, CW, bw), lambda n: (li, 0, n)),
                  vec(), gw(), gw(), vec(), vec(), vec(),
                  pl.BlockSpec((CW - 1, Bs, bw), lambda n: (0, 0, n)),
                  pl.BlockSpec((Bs, bw), lambda n: (0, n))],
        out_specs=[pl.BlockSpec((Bs, bw), lambda n: (0, n)),
                   pl.BlockSpec((Bs, bw), lambda n: (0, n))],
        out_shape=[jax.ShapeDtypeStruct((Bs, W), BF16), jax.ShapeDtypeStruct((Bs, W), F32)],
        compiler_params=_cparams(1),
        name="lru_sample",
    )(z, z, conv_w, conv_b, w_ga, w_gx, b_ga, b_gx, lam, buf, h0)


MOE_ROW_BLOCK = 512
MOE_SUB_BLOCK = 128


def _valid_rows_only(n_valid, o_ref, compute):
    tm = o_ref.shape[0]

    @pl.when(n_valid == tm)
    def _():
        o_ref[...] = compute(slice(None))

    for s in range(tm // MOE_SUB_BLOCK):
        rows = slice(s * MOE_SUB_BLOCK, (s + 1) * MOE_SUB_BLOCK)

        @pl.when(jnp.logical_and(n_valid < tm, n_valid > s * MOE_SUB_BLOCK))
        def _(rows=rows):
            o_ref[rows, :] = compute(rows)

        @pl.when(n_valid <= s * MOE_SUB_BLOCK)
        def _(rows=rows):
            o_ref[rows, :] = jnp.zeros((MOE_SUB_BLOCK, o_ref.shape[1]), o_ref.dtype)


def _stream_weight_panels(bf_ref, sg_ref, se_ref, ns_ref, w_hbms, wf_refs, wb_refs, sem, *, e0, tn):
    j = pl.program_id(0)
    rb = pl.program_id(1)
    n_seg = ns_ref[0]

    def copies(k, jcol, slot):
        col = pl.multiple_of(jcol * tn, tn)
        return [pltpu.make_async_copy(w.at[e0 + se_ref[k], :, pl.ds(col, tn)], wf.at[slot], sem.at[i, slot])
                for i, (w, wf) in enumerate(zip(w_hbms, wf_refs))]

    @pl.when(bf_ref[rb] == 1)
    def _():
        k = sg_ref[rb]
        slot = jnp.bitwise_and(j * n_seg + k, 1)

        @pl.when(jnp.logical_and(j == 0, k == 0))
        def _():
            for c in copies(0, 0, 0):
                c.start()

        for c in copies(k, j, slot):
            c.wait()
        for wf, wb in zip(wf_refs, wb_refs):
            wb[...] = wf[slot].astype(BF16)

        last = k + 1 == n_seg
        k2 = jnp.where(last, 0, k + 1)
        j2 = jnp.where(last, j + 1, j)

        @pl.when(j2 < pl.num_programs(0))
        def _():
            for c in copies(k2, j2, 1 - slot):
                c.start()


def _moe_up_kernel(bf_ref, bv_ref, sg_ref, se_ref, ns_ref, x_ref, w1_hbm, w3_hbm, o_ref,
                   w1f_ref, w3f_ref, w1b_ref, w3b_ref, sem, *, e0, tn):
    rb = pl.program_id(1)
    _stream_weight_panels(bf_ref, sg_ref, se_ref, ns_ref, [w1_hbm, w3_hbm], [w1f_ref, w3f_ref],
                          [w1b_ref, w3b_ref], sem, e0=e0, tn=tn)

    def compute(rows):
        x = x_ref[rows, :]
        u = jnp.dot(x, w1b_ref[...], preferred_element_type=F32)
        v = jnp.dot(x, w3b_ref[...], preferred_element_type=F32)
        return (u * jax.nn.sigmoid(u) * v).astype(o_ref.dtype)

    _valid_rows_only(bv_ref[rb], o_ref, compute)


def _moe_down_kernel(bf_ref, bv_ref, sg_ref, se_ref, ns_ref, a_ref, w_hbm, o_ref, wf_ref, wb_ref, sem, *, e0, tn):
    rb = pl.program_id(1)
    _stream_weight_panels(bf_ref, sg_ref, se_ref, ns_ref, [w_hbm], [wf_ref], [wb_ref], sem, e0=e0, tn=tn)

    def compute(rows):
        return jnp.dot(a_ref[rows, :], wb_ref[...], preferred_element_type=F32)

    _valid_rows_only(bv_ref[rb], o_ref, compute)


def _moe_up(xs, w1, w3, seg, li, n_experts):
    n_rows, D = xs.shape
    F = w1.shape[-1]
    tm = MOE_ROW_BLOCK
    tn = _tile(F, 1024, V7X_LANES)
    rows = lambda j, rb, *_: (rb, 0)
    grid_spec = pltpu.PrefetchScalarGridSpec(
        num_scalar_prefetch=5,
        grid=(F // tn, n_rows // tm),
        in_specs=[pl.BlockSpec((tm, D), rows), pl.BlockSpec(memory_space=pl.ANY),
                  pl.BlockSpec(memory_space=pl.ANY)],
        out_specs=pl.BlockSpec((tm, tn), lambda j, rb, *_: (rb, j)),
        scratch_shapes=[pltpu.VMEM((2, D, tn), F32), pltpu.VMEM((2, D, tn), F32),
                        pltpu.VMEM((D, tn), BF16), pltpu.VMEM((D, tn), BF16),
                        pltpu.SemaphoreType.DMA((2, 2))],
    )
    return pl.pallas_call(
        functools.partial(_moe_up_kernel, e0=li * n_experts, tn=tn),
        grid_spec=grid_spec,
        out_shape=jax.ShapeDtypeStruct((n_rows, F), BF16),
        compiler_params=_cparams(2),
        name="moe_up",
    )(*seg, xs, w1, w3)


def _moe_down(hh, w2, seg, li, n_experts):
    n_rows, F = hh.shape
    D = w2.shape[-1]
    tm = MOE_ROW_BLOCK
    tn = _tile(D, 512, V7X_LANES)
    grid_spec = pltpu.PrefetchScalarGridSpec(
        num_scalar_prefetch=5,
        grid=(D // tn, n_rows // tm),
        in_specs=[pl.BlockSpec((tm, F), lambda j, rb, *_: (rb, 0)), pl.BlockSpec(memory_space=pl.ANY)],
        out_specs=pl.BlockSpec((tm, tn), lambda j, rb, *_: (rb, j)),
        scratch_shapes=[pltpu.VMEM((2, F, tn), F32), pltpu.VMEM((F, tn), BF16),
                        pltpu.SemaphoreType.DMA((1, 2))],
    )
    return pl.pallas_call(
        functools.partial(_moe_down_kernel, e0=li * n_experts, tn=tn),
        grid_spec=grid_spec,
        out_shape=jax.ShapeDtypeStruct((n_rows, D), F32),
        compiler_params=_cparams(2),
        name="moe_down",
    )(*seg, hh, w2)


def _dispatch_kernel(src_ref, bv_ref, tab_ref, o_ref, buf_ref, *, tm):
    rb = pl.program_id(1)

    @pl.when(bv_ref[rb] > 0)
    def _():
        base = rb * tm

        def body(r, carry):
            buf_ref[pl.ds(r, 1), :] = tab_ref[pl.ds(src_ref[base + r], 1), :]
            return carry

        lax.fori_loop(0, tm, body, 0, unroll=8)
        o_ref[...] = buf_ref[...].astype(o_ref.dtype)

    @pl.when(bv_ref[rb] == 0)
    def _():
        o_ref[...] = jnp.zeros(o_ref.shape, o_ref.dtype)


def _dispatch(table, src, blk_valid):
    n_tok, D = table.shape
    n_rows = src.shape[0]
    tm = MOE_ROW_BLOCK
    tc = D
    while tc > V7X_LANES and n_tok * tc * 4 > 36 * 1024 * 1024:
        tc = _tile(D, tc // 2, V7X_LANES)
    grid_spec = pltpu.PrefetchScalarGridSpec(
        num_scalar_prefetch=2,
        grid=(D // tc, n_rows // tm),
        in_specs=[pl.BlockSpec((n_tok, tc), lambda j, rb, src, nu: (0, j), pipeline_mode=pl.Buffered(1))],
        out_specs=pl.BlockSpec((tm, tc), lambda j, rb, src, nu: (rb, j)),
        scratch_shapes=[pltpu.VMEM((tm, tc), F32)],
    )
    return pl.pallas_call(
        functools.partial(_dispatch_kernel, tm=tm),
        grid_spec=grid_spec,
        out_shape=jax.ShapeDtypeStruct((n_rows, D), BF16),
        compiler_params=_cparams(2),
        name="moe_dispatch",
    )(src, blk_valid, table)


def _combine_kernel(d_ref, tab_ref, route_ref, o_ref, a_ref, b_ref, *, tm):
    base = pl.program_id(1) * (tm * TOP_K)

    def body(t, carry):
        a_ref[pl.ds(t, 1), :] = tab_ref[pl.ds(d_ref[base + TOP_K * t], 1), :]
        b_ref[pl.ds(t, 1), :] = tab_ref[pl.ds(d_ref[base + TOP_K * t + 1], 1), :]
        return carry

    lax.fori_loop(0, tm, body, 0, unroll=8)
    rt = route_ref[...]
    o_ref[...] = rt[:, TOP_K:TOP_K + 1] * a_ref[...] + rt[:, TOP_K + 1:TOP_K + 2] * b_ref[...]


def _combine(ys, dest, route):
    assert TOP_K == 2
    n_rows, D = ys.shape
    n_tok = route.shape[0]
    tm = _tile(n_tok, 512, V7X_SUBLANES)
    tc = _tile(D, 512, V7X_LANES)
    while tc > V7X_LANES and n_rows * tc * 4 > 44 * 1024 * 1024:
        tc = _tile(D, tc // 2, V7X_LANES)
    grid_spec = pltpu.PrefetchScalarGridSpec(
        num_scalar_prefetch=1,
        grid=(D // tc, n_tok // tm),
        in_specs=[pl.BlockSpec((n_rows, tc), lambda j, i, d: (0, j), pipeline_mode=pl.Buffered(1)),
                  pl.BlockSpec((tm, V7X_LANES), lambda j, i, d: (i, 0))],
        out_specs=pl.BlockSpec((tm, tc), lambda j, i, d: (i, j)),
        scratch_shapes=[pltpu.VMEM((tm, tc), F32), pltpu.VMEM((tm, tc), F32)],
    )
    return pl.pallas_call(
        functools.partial(_combine_kernel, tm=tm),
        grid_spec=grid_spec,
        out_shape=jax.ShapeDtypeStruct((n_tok, D), F32),
        compiler_params=_cparams(2),
        name="moe_combine",
    )(dest, ys, route)


def _residual_kernel(x_ref, y_ref, g_ref, o_ref):
    o_ref[...] = x_ref[...] + g_ref[...] * y_ref[...]


def _residual_norm_kernel(x_ref, y_ref, g_ref, nf_ref, o_ref):
    x = x_ref[...] + g_ref[...] * y_ref[...]
    ms = jnp.mean(x * x, axis=-1, keepdims=True)
    o_ref[...] = x * lax.rsqrt(ms + EPS) * nf_ref[...]


def _norm_kernel(x_ref, nf_ref, o_ref):
    x = x_ref[...]
    ms = jnp.mean(x * x, axis=-1, keepdims=True)
    o_ref[...] = x * lax.rsqrt(ms + EPS) * nf_ref[...]


def _residual(x, y_tok, row0, gate, norm_g):
    B, L, D = x.shape
    R = gate.shape[1]
    tm = L if R == L else _tile(L, 512, V7X_SUBLANES)
    nb = L // tm
    assert row0 % tm == 0
    blk0 = row0 // tm
    blk = pl.BlockSpec((None, tm, D), lambda b, i: (b, i, 0))
    in_specs = [blk, pl.BlockSpec((tm, D), lambda b, i: (blk0 + b * nb + i, 0)),
                pl.BlockSpec((None, R, D), lambda b, i: (b, 0, 0))]
    args = [x, y_tok, gate]
    kern = _residual_kernel
    if norm_g is not None:
        in_specs.append(pl.BlockSpec((1, D), lambda b, i: (0, 0)))
        args.append(norm_g)
        kern = _residual_norm_kernel
    return pl.pallas_call(
        kern, grid=(B, L // tm), in_specs=in_specs, out_specs=blk,
        out_shape=jax.ShapeDtypeStruct((B, L, D), F32),
        compiler_params=_cparams(2), name="residual",
    )(*args)


def _final_norm(x, norm_g):
    B, L, D = x.shape
    tm = _tile(L, 512, V7X_SUBLANES)
    blk = pl.BlockSpec((None, tm, D), lambda b, i: (b, i, 0))
    return pl.pallas_call(
        _norm_kernel, grid=(B, L // tm),
        in_specs=[blk, pl.BlockSpec((1, D), lambda b, i: (0, 0))], out_specs=blk,
        out_shape=jax.ShapeDtypeStruct((B, L, D), F32),
        compiler_params=_cparams(2), name="final_norm",
    )(x, norm_g)


def _split_mods(m, n_p, n_s):
    D = m.shape[-1] // 6
    mp = [m[:n_p, k * D:(k + 1) * D][:, None, :] for k in range(6)]
    ms = [m[n_p:n_p + n_s, k * D:(k + 1) * D][None, :, :] for k in range(6)]
    return mp, ms


def _stack_cond(c_prompt, c_sample):
    c_all = jnp.concatenate([c_prompt, c_sample], axis=0)
    pad = (-c_all.shape[0]) % V7X_SUBLANES
    if pad:
        c_all = jnp.concatenate([c_all, jnp.zeros((pad, c_all.shape[1]), c_all.dtype)], axis=0)
    return c_all


def _moe(table, route, w1, w3, w2, li, n_experts):
    T = table.shape[0]
    E = n_experts
    tm = MOE_ROW_BLOCK
    e_flat = route[:, :TOP_K].astype(jnp.int32).reshape(-1)
    n_slots = T * TOP_K
    onehot = (e_flat[:, None] == jnp.arange(E, dtype=jnp.int32)[None, :]).astype(jnp.int32)
    csum = jnp.cumsum(onehot, axis=0)
    counts = csum[-1]
    rank = jnp.sum((csum - onehot) * onehot, axis=-1)
    padded = (counts + tm - 1) // tm * tm
    pends = jnp.cumsum(padded)
    pstarts = pends - padded
    rem = counts % tm
    rem_s = jnp.sum(onehot * rem[None, :], axis=-1)
    pos = jnp.where(rank < rem_s, rank, rank + (tm - rem_s) % tm)
    dest = (jnp.sum(onehot * pstarts[None, :], axis=-1) + pos).astype(jnp.int32)
    n_blocks = -(-n_slots // tm) + E
    n_rows = n_blocks * tm
    tok_flat = jnp.arange(n_slots, dtype=jnp.int32) // TOP_K
    src = jnp.zeros((n_rows,), jnp.int32).at[dest].set(tok_flat)
    blk_start = jnp.arange(n_blocks, dtype=jnp.int32) * tm
    blk_e = jnp.minimum(jnp.sum((blk_start[:, None] >= pends[None, :]).astype(jnp.int32), axis=-1), E - 1)
    blk_first = jnp.concatenate([jnp.ones((1,), jnp.int32), (blk_e[1:] != blk_e[:-1]).astype(jnp.int32)])
    rem_b = jnp.sum((blk_e[:, None] == jnp.arange(E, dtype=jnp.int32)[None, :]) * rem[None, :], axis=-1)
    blk_valid = jnp.where(blk_start >= pends[-1], 0,
                          jnp.where(jnp.logical_and(blk_first == 1, rem_b > 0), rem_b, tm)).astype(jnp.int32)
    present = (counts > 0).astype(jnp.int32)
    seg_of_e = jnp.cumsum(present) - 1
    eids = jnp.arange(E, dtype=jnp.int32)
    seg_e = jnp.sum(eids[None, :] * present[None, :] * (seg_of_e[None, :] == eids[:, None]), axis=-1).astype(jnp.int32)
    n_seg = jnp.sum(present, keepdims=True).astype(jnp.int32)
    seg_first = (blk_first * (blk_valid > 0)).astype(jnp.int32)
    seg_idx = jnp.sum((blk_e[:, None] == eids[None, :]) * seg_of_e[None, :], axis=-1).astype(jnp.int32)
    seg = (seg_first, blk_valid, seg_idx, seg_e, n_seg)
    xs = _dispatch(table, src, blk_valid)
    hh = _moe_up(xs, w1, w3, seg, li, E)
    ys = _moe_down(hh, w2, seg, li, E)
    return _combine(ys, dest, route)


def kernel(x_prompt, x_sample, c_prompt, c_sample, cache_k, cache_v, state_ret, state_conv, state_lru, page_table, ada_w_even, ada_b_even, norm_mix_even, norm_ffn_even, w_in_even, lambda_qk, subln_a, gn_b, w_out_even, ffn_w1, ffn_w3, ffn_w2, ada_w_odd, ada_b_odd, norm_mix_odd, norm_ffn_odd, w_in_odd, conv_w, conv_b, w_gate_a, b_gate_a, w_gate_x, b_gate_x, lru_lambda, w_out_odd, router, moe_w1, moe_w3, moe_w2, norm_final):
    Bp, Lp, D = x_prompt.shape
    Bs, Ls, _ = x_sample.shape
    assert Ls == 1, "the sample group carries one new token per sequence"
    n_even, n_odd = w_in_even.shape[0], w_in_odd.shape[0]
    depth = n_even + n_odd
    _, n_phys, page, HA, _, dka = cache_k.shape
    HB, dkb = state_ret.shape[2], state_ret.shape[3]
    past_len = page_table.shape[1] * page
    E = router.shape[-1]
    CW = conv_w.shape[1]
    W = conv_w.shape[-1]

    half = dkb // 2
    freqs = ROPE_BASE ** (-jnp.arange(half, dtype=F32) / half)
    ang_p = jnp.arange(Lp).astype(F32)[:, None] * freqs[None, :]
    ang_s = (past_len + jnp.arange(Ls)).astype(F32)[:, None] * freqs[None, :]
    cos_p, sin_p, cos_s, sin_s = jnp.cos(ang_p), jnp.sin(ang_p), jnp.cos(ang_s), jnp.sin(ang_s)
    log_gamma = jnp.log1p(-jnp.exp2(-5.0 - jnp.arange(HB, dtype=F32)))
    log_gamma = jnp.broadcast_to(log_gamma[:, None, None], (HB, 1, V7X_LANES))

    c_all = _stack_cond(c_prompt, c_sample)
    row3 = lambda a: a.reshape(a.shape[0], 1, a.shape[-1])
    cache_kt = jnp.transpose(cache_k, (0, 1, 3, 4, 5, 2)).reshape(n_even * n_phys, HA * 2 * dka, page)
    cache_v2 = cache_v.reshape(n_even * n_phys, page * HA, 2 * dka)
    subln3 = subln_a.reshape(n_even, HA, 2 * dka)
    state_ret2 = state_ret.reshape(n_even * Bs, HB, dkb, dkb)
    w_ga2 = w_gate_a.reshape(-1, w_gate_a.shape[-2], w_gate_a.shape[-1])
    w_gx2 = w_gate_x.reshape(-1, w_gate_x.shape[-2], w_gate_x.shape[-1])
    moe_w1f = moe_w1.reshape(-1, moe_w1.shape[-2], moe_w1.shape[-1])
    moe_w3f = moe_w3.reshape(-1, moe_w3.shape[-2], moe_w3.shape[-1])
    moe_w2f = moe_w2.reshape(-1, moe_w2.shape[-2], moe_w2.shape[-1])
    router_pad = jnp.concatenate(
        [router, jnp.zeros(router.shape[:-1] + (V7X_LANES - E,), router.dtype)], axis=-1)
    nf = norm_final.reshape(1, D)

    xp = x_prompt
    xs = x_sample.reshape(1, Bs, D)
    kp_l, vp_l, rp_l, cp_l, lp_l = [], [], [], [], []
    ks_l, vs_l, rs_l, cs_l, ls_l = [], [], [], [], []
    normed = False
    for l in range(depth):
        i = l // 2
        last = l == depth - 1
        if l % 2 == 0:
            lam_init = 0.8 - 0.6 * math.exp(-0.3 * l)
            m = _adaln(c_all, ada_w_even, row3(ada_b_even), i)
            (sh1, sc1, g1, sh2, sc2, g2), (sh1s, sc1s, g1s, sh2s, sc2s, g2s) = _split_mods(m, Bp, Bs)
            gmix, gffn = norm_mix_even[i][None, :], norm_ffn_even[i][None, :]
            wa = HA * 2 * dka
            h = _modulate(xp, gmix, sh1, sc1)
            hs = _modulate(xs, gmix, sh1s, sc1s)
            z, zs = _dense("mm_in_even", [h], [w_in_even], i, F32, [hs[0]])
            oa, k_t, v_new = _attn_prompt(z, lambda_qk, row3(subln_a), i, HA, lam_init)
            kp_l.append(jnp.transpose(k_t.reshape(Bp, HA, 2, dka, Lp), (0, 4, 1, 2, 3)))
            vp_l.append(v_new.reshape(Bp, Lp, HA, 2 * dka))
            ob, s_fin = _ret_prompt(z, cos_p, sin_p, log_gamma, row3(gn_b), i, HA, HB)
            rp_l.append(s_fin)
            zs3 = zs.reshape(Bs, 1, zs.shape[-1])
            ks_l.append(zs3[:, :, wa:2 * wa].reshape(Bs, Ls, HA, 2, dka))
            v_new = zs3[:, :, 2 * wa:3 * wa].reshape(Bs, HA, 2 * dka)
            vs_l.append(v_new.reshape(Bs, Ls, HA, 2 * dka))
            oas = _attn_sample(zs3, v_new, cache_kt, cache_v2, page_table, lambda_qk, subln3, i, HA, lam_init)
            obs, s_new = _ret_sample(zs3, cos_s, sin_s, log_gamma, row3(gn_b), state_ret2, i, HA, HB)
            rs_l.append(s_new)
            xp, xs2 = _dense("mm_out_even", [oa, ob], [w_out_even], i, F32,
                             [oas.reshape(Bs, wa), obs.reshape(Bs, D - wa)], (xp, g1), (xs[0], g1s[0]))
            xs = xs2[None]
            h = _modulate(xp, gffn, sh2, sc2)
            hs = _modulate(xs, gffn, sh2s, sc2s)
            hh, hhs = _dense("mm_swiglu", [h], [ffn_w1, ffn_w3], i, BF16, [hs[0]])
            xp, xs2 = _dense("mm_ffn_down", [hh], [ffn_w2], i, F32, [hhs], (xp, g2), (xs[0], g2s[0]))
            xs = xs2[None]
        else:
            m = _adaln(c_all, ada_w_odd, row3(ada_b_odd), i)
            (sh1, sc1, g1, sh2, sc2, g2), (sh1s, sc1s, g1s, sh2s, sc2s, g2s) = _split_mods(m, Bp, Bs)
            gmix, gffn = norm_mix_odd[i][None, :], norm_ffn_odd[i][None, :]
            lru_args = (conv_w, row3(conv_b), w_ga2, w_gx2, row3(b_gate_a), row3(b_gate_x), row3(lru_lambda))
            h = _modulate(xp, gmix, sh1, sc1)
            hs = _modulate(xs, gmix, sh1s, sc1s)
            z, zs = _dense("mm_in_odd", [h], [w_in_odd], i, F32, [hs[0]])
            tail0 = jnp.zeros((Bp, V7X_SUBLANES, W), F32)
            h0 = jnp.zeros((Bp, 1, W), F32)
            y, h_last = _lru_prompt(z, *lru_args, tail0, h0, i)
            cp_l.append(z[:, Lp - (CW - 1):, W:])
            lp_l.append(h_last.reshape(Bp, W))
            buf = state_conv[i]
            ys, h_new = _lru_sample(zs, *lru_args, jnp.swapaxes(buf, 0, 1), state_lru[i], i)
            cs_l.append(jnp.concatenate([buf[:, 1:], zs[:, None, W:]], axis=1))
            ls_l.append(h_new)
            xp, xs2 = _dense("mm_out_odd", [y], [w_out_odd], i, F32, [ys], (xp, g1), (xs[0], g1s[0]))
            xs = xs2[None]
            Tp = Bp * Lp
            table, route = _modulate_route(xp, gffn, sh2, sc2, xs[0], sh2s[0], sc2s[0], router_pad[i], E)
            y_tok = _moe(table, route, moe_w1f, moe_w3f, moe_w2f, i, E)
            xp = _residual(xp, y_tok, 0, g2, nf if last else None)
            xs = _residual(xs, y_tok, Tp, g2s, nf if last else None)
            normed = last
    if not normed:
        xp = _final_norm(xp, nf)
        xs = _final_norm(xs, nf)
    y_prompt = xp.astype(x_prompt.dtype)
    y_sample = xs.reshape(Bs, Ls, D).astype(x_sample.dtype)
    return (y_prompt, y_sample,
            jnp.stack(kp_l), jnp.stack(vp_l), jnp.stack(rp_l), jnp.stack(cp_l), jnp.stack(lp_l),
            jnp.stack(ks_l), jnp.stack(vs_l), jnp.stack(rs_l), jnp.stack(cs_l), jnp.stack(ls_l))
```

```python
import functools
import math

import jax
import jax.numpy as jnp
from jax import lax
from jax.experimental import pallas as pl
from jax.experimental.pallas import tpu as pltpu

F32 = jnp.float32
BF16 = jnp.bfloat16

EPS = 1e-6
ROPE_BASE = 10000.0
LRU_C = 8.0
TOP_K = 2
NEG_BIG = -1e30

V7X_LANES = 128
V7X_SUBLANES = 8
V7X_VMEM_LIMIT_BYTES = 56 * 1024 * 1024


def _cparams(n_axes):
    return pltpu.CompilerParams(dimension_semantics=("arbitrary",) * n_axes,
                                vmem_limit_bytes=V7X_VMEM_LIMIT_BYTES)


def _tile(n, pref, align):
    if n <= pref:
        return n
    t = (pref // align) * align
    while t > align and n % t:
        t -= align
    assert n % t == 0, (n, pref, align)
    return t


def _iota(shape, dim):
    return lax.broadcasted_iota(jnp.int32, shape, dim)


def _adaln_kernel(c_ref, w_ref, b_ref, o_ref):
    c = c_ref[...]
    s = (c * jax.nn.sigmoid(c)).astype(BF16)
    o_ref[...] = jnp.dot(s, w_ref[...].astype(BF16), preferred_element_type=F32) + b_ref[...]


def _adaln(c_all, w, b, li):
    R, D = c_all.shape
    N = w.shape[-1]
    tn = _tile(N, 1024, V7X_LANES)
    return pl.pallas_call(
        _adaln_kernel,
        grid=(N // tn,),
        in_specs=[pl.BlockSpec((R, D), lambda j: (0, 0)),
                  pl.BlockSpec((None, D, tn), lambda j: (li, 0, j)),
                  pl.BlockSpec((None, 1, tn), lambda j: (li, 0, j))],
        out_specs=pl.BlockSpec((R, tn), lambda j: (0, j)),
        out_shape=jax.ShapeDtypeStruct((R, N), F32),
        compiler_params=_cparams(1),
        name="adaln",
    )(c_all, w, b)


def _modulated(x_ref, g_ref, sh_ref, sc_ref):
    x = x_ref[...]
    ms = jnp.mean(x * x, axis=-1, keepdims=True)
    return x * lax.rsqrt(ms + EPS) * g_ref[...] * (1.0 + sc_ref[...]) + sh_ref[...]


def _modulate_kernel(x_ref, g_ref, sh_ref, sc_ref, o_ref):
    o_ref[...] = _modulated(x_ref, g_ref, sh_ref, sc_ref).astype(o_ref.dtype)


def _top2_route(h, wr_ref, n_experts):
    w = wr_ref[...]
    h_hi, w_hi = h.astype(BF16), w.astype(BF16)
    h_lo = (h - h_hi.astype(F32)).astype(BF16)
    w_lo = (w - w_hi.astype(F32)).astype(BF16)
    logits = (jnp.dot(h_hi, w_hi, preferred_element_type=F32) + jnp.dot(h_hi, w_lo, preferred_element_type=F32)
              + jnp.dot(h_lo, w_hi, preferred_element_type=F32))
    lane = _iota(logits.shape, 1)
    logits = jnp.where(lane < n_experts, logits, NEG_BIG)
    m1 = jnp.max(logits, axis=-1, keepdims=True)
    i1 = jnp.min(jnp.where(logits == m1, lane, V7X_LANES), axis=-1, keepdims=True)
    rest = jnp.where(lane == i1, NEG_BIG, logits)
    m2 = jnp.max(rest, axis=-1, keepdims=True)
    i2 = jnp.min(jnp.where(rest == m2, lane, V7X_LANES), axis=-1, keepdims=True)
    e2 = jnp.exp(m2 - m1)
    g1 = 1.0 / (1.0 + e2)
    g2 = e2 / (1.0 + e2)
    return jnp.where(lane == 0, i1.astype(F32),
                     jnp.where(lane == 1, i2.astype(F32),
                               jnp.where(lane == 2, g1, jnp.where(lane == 3, g2, 0.0))))


def _out_proj_kernel(*refs, n_a, n_prompt_steps, n_experts):
    route = n_experts is not None
    it = iter(refs)
    take = lambda n: [next(it) for _ in range(n)]
    prompt = (take(n_a), *take(4))
    sample = (take(n_a), *take(4))
    w_ref, gn_ref = take(2)
    wr_ref = next(it) if route else None
    xo_ref, xso_ref, h_ref, hs_ref = take(4)
    (wb_ref,) = take(1)
    t = pl.program_id(0)
    _cast_weights_once(t == 0, [w_ref], [wb_ref])

    def run(a_refs, x_ref, g_ref, sh_ref, sc_ref):
        y, k0 = None, 0
        for a_ref in a_refs:
            k1 = k0 + a_ref.shape[-1]
            part = jnp.dot(a_ref[...], wb_ref[k0:k1, :], preferred_element_type=F32)
            y = part if y is None else y + part
            k0 = k1
        xn = x_ref[...] + g_ref[...] * y
        ms = jnp.mean(xn * xn, axis=-1, keepdims=True)
        return xn, xn * lax.rsqrt(ms + EPS) * gn_ref[...] * (1.0 + sc_ref[...]) + sh_ref[...]

    @pl.when(t < n_prompt_steps)
    def _():
        xn, h = run(*prompt)
        xo_ref[...] = xn
        if route:
            h_ref[...] = h
            hs_ref[...] = _top2_route(h, wr_ref, n_experts)
        else:
            h_ref[...] = h.astype(h_ref.dtype)

    @pl.when(t == n_prompt_steps)
    def _():
        xn, h = run(*sample)
        xso_ref[...] = xn
        if route:
            n = h.shape[0]
            h_ref[:n, :] = h
            hs_ref[:n, :] = _top2_route(h, wr_ref, n_experts)
        else:
            hs_ref[...] = h.astype(hs_ref.dtype)


def _mod_specs(B, L, D, R, tm):
    xmap = lambda b, i: (b, i, 0)
    return [pl.BlockSpec((None, tm, D), xmap),
            pl.BlockSpec((1, D), lambda b, i: (0, 0)),
            pl.BlockSpec((None, R, D), lambda b, i: (b, 0, 0)),
            pl.BlockSpec((None, R, D), lambda b, i: (b, 0, 0))]


def _modulate(x, g, shift, scale):
    B, L, D = x.shape
    R = shift.shape[1]
    tm = L if R == L else _tile(L, 512, V7X_SUBLANES)
    return pl.pallas_call(
        _modulate_kernel,
        grid=(B, L // tm),
        in_specs=_mod_specs(B, L, D, R, tm),
        out_specs=pl.BlockSpec((None, tm, D), lambda b, i: (b, i, 0)),
        out_shape=jax.ShapeDtypeStruct((B, L, D), BF16),
        compiler_params=_cparams(2),
        name="modulate",
    )(x, g, shift, scale)


def _out_proj(name, acts, w, li, prompt_mods, sample_acts, sample_mods, g_next, router=None, n_experts=None):
    x = prompt_mods[0]
    B, L, N = x.shape
    Bs = sample_mods[0].shape[0]
    K = sum(a.shape[-1] for a in acts)
    tm = _tile(L, 256, V7X_SUBLANES)
    assert Bs <= tm
    nb = L // tm
    n_p = B * nb
    bmap = lambda t: jnp.minimum(t // nb, B - 1)
    imap = lambda t: jnp.where(t < n_p, t % nb, nb - 1)
    row = lambda width: pl.BlockSpec((None, tm, width), lambda t: (bmap(t), imap(t), 0))
    per_b = pl.BlockSpec((None, 1, N), lambda t: (bmap(t), 0, 0))
    whole = lambda *shape: pl.BlockSpec(shape, lambda t: (0,) * len(shape))
    in_specs = ([row(a.shape[-1]) for a in acts] + [row(N), per_b, per_b, per_b]
                + [whole(Bs, a.shape[-1]) for a in sample_acts] + [whole(Bs, N)] * 4
                + [pl.BlockSpec((None, K, N), lambda t: (li, 0, 0), pipeline_mode=pl.Buffered(1)), whole(1, N)])
    args = [*acts, *prompt_mods, *sample_acts, *sample_mods, w, g_next]
    out_specs = [row(N), whole(Bs, N)]
    out_shape = [jax.ShapeDtypeStruct((B, L, N), F32), jax.ShapeDtypeStruct((Bs, N), F32)]
    if router is not None:
        in_specs.append(whole(N, V7X_LANES))
        args.append(router)
        n_tok = B * L + Bs
        out_specs += [pl.BlockSpec((tm, N), lambda t: (t, 0)), pl.BlockSpec((tm, V7X_LANES), lambda t: (t, 0))]
        out_shape += [jax.ShapeDtypeStruct((n_tok, N), F32), jax.ShapeDtypeStruct((n_tok, V7X_LANES), F32)]
    else:
        out_specs += [row(N), whole(Bs, N)]
        out_shape += [jax.ShapeDtypeStruct((B, L, N), BF16), jax.ShapeDtypeStruct((Bs, N), BF16)]
    return pl.pallas_call(
        functools.partial(_out_proj_kernel, n_a=len(acts), n_prompt_steps=n_p, n_experts=n_experts),
        grid=(n_p + 1,),
        in_specs=in_specs,
        out_specs=out_specs,
        out_shape=out_shape,
        scratch_shapes=[pltpu.VMEM((K, N), BF16)],
        compiler_params=_cparams(1),
        name=name,
    )(*args)


def _cast_weights_once(first, w_refs, wb_refs):
    @pl.when(first)
    def _():
        for w_ref, wb_ref in zip(w_refs, wb_refs):
            wb_ref[...] = w_ref[...].astype(BF16)


def _dense_kernel(*refs, swiglu, n_a, n_w, residual, n_prompt_steps):
    it = iter(refs)
    take = lambda n: [next(it) for _ in range(n)]
    a_refs, xg_refs, w_refs = take(n_a), take(2 if residual else 0), take(n_w)
    as_refs, xgs_refs = take(n_a), take(2 if residual else 0)
    o_ref, os_ref = take(2)
    wb_refs = take(n_w)
    t = pl.program_id(1)
    _cast_weights_once(t == 0, w_refs, wb_refs)

    def run(a_refs, xg_refs, o_ref):
        if swiglu:
            a = a_refs[0][...]
            u = jnp.dot(a, wb_refs[0][...], preferred_element_type=F32)
            v = jnp.dot(a, wb_refs[1][...], preferred_element_type=F32)
            y = u * jax.nn.sigmoid(u) * v
        else:
            y, k0 = None, 0
            for a_ref in a_refs:
                k1 = k0 + a_ref.shape[-1]
                part = jnp.dot(a_ref[...], wb_refs[0][k0:k1, :], preferred_element_type=F32)
                y = part if y is None else y + part
                k0 = k1
        if residual:
            y = xg_refs[0][...] + xg_refs[1][...] * y
        o_ref[...] = y.astype(o_ref.dtype)

    pl.when(t < n_prompt_steps)(lambda: run(a_refs, xg_refs, o_ref))
    pl.when(t == n_prompt_steps)(lambda: run(as_refs, xgs_refs, os_ref))


def _mm_tiles(L, K, N, n_w):
    budget = V7X_VMEM_LIMIT_BYTES // 2
    tn = _tile(N, 1024, V7X_LANES)
    while tn > V7X_LANES and n_w * K * tn * (2 * 4 + 2) > budget:
        tn = _tile(N, tn // 2, V7X_LANES)
    tm = _tile(L, 1024 if K <= 2048 else 512, V7X_SUBLANES)
    return tm, tn


def _dense(name, acts, weights, li, out_dtype, sample_acts, resid=None, sample_resid=None):
    B, L = acts[0].shape[:2]
    Bs = sample_acts[0].shape[0]
    K = sum(a.shape[-1] for a in acts)
    N = weights[0].shape[-1]
    tm, tn = _mm_tiles(L, K, N, len(weights))
    nb = L // tm
    n_p = B * nb
    bmap = lambda t: jnp.minimum(t // nb, B - 1)
    imap = lambda t: jnp.where(t < n_p, t % nb, nb - 1)
    row_spec = lambda width: pl.BlockSpec((None, tm, width), lambda j, t: (bmap(t), imap(t), 0))
    col_spec = pl.BlockSpec((None, tm, tn), lambda j, t: (bmap(t), imap(t), j))
    in_specs = [row_spec(a.shape[-1]) for a in acts]
    args = list(acts)
    if resid is not None:
        in_specs += [col_spec, pl.BlockSpec((None, 1, tn), lambda j, t: (bmap(t), 0, j))]
        args += list(resid)
    in_specs += [pl.BlockSpec((None, K, tn), lambda j, t: (li, 0, j)) for _ in weights]
    args += list(weights)
    in_specs += [pl.BlockSpec((Bs, a.shape[-1]), lambda j, t: (0, 0)) for a in sample_acts]
    args += list(sample_acts)
    if resid is not None:
        in_specs += [pl.BlockSpec((Bs, tn), lambda j, t: (0, j)), pl.BlockSpec((Bs, tn), lambda j, t: (0, j))]
        args += list(sample_resid)
    return pl.pallas_call(
        functools.partial(_dense_kernel, swiglu=len(weights) == 2, n_a=len(acts), n_w=len(weights),
                          residual=resid is not None, n_prompt_steps=n_p),
        grid=(N // tn, n_p + 1),
        in_specs=in_specs,
        out_specs=[col_spec, pl.BlockSpec((Bs, tn), lambda j, t: (0, j))],
        out_shape=[jax.ShapeDtypeStruct((B, L, N), out_dtype), jax.ShapeDtypeStruct((Bs, N), out_dtype)],
        scratch_shapes=[pltpu.VMEM((K, tn), BF16) for _ in weights],
        compiler_params=_cparams(2),
        name=name,
    )(*args)


def _diff_lambda(lq_ref, lam_init):
    lq = lq_ref[...]
    a = jnp.sum(lq[0:1] * lq[1:2], axis=-1, keepdims=True)
    b = jnp.sum(lq[2:3] * lq[3:4], axis=-1, keepdims=True)
    return jnp.exp(a) - jnp.exp(b) + lam_init


def _attn_prompt_kernel(q_ref, k_ref, v_ref, lq_ref, sub_ref, o_ref, kt_ref, vo_ref, *, tq, tk, dk, lam_init):
    qi = pl.program_id(2)

    @pl.when(qi == 0)
    def _():
        kt_ref[...] = k_ref[...].T
        vo_ref[...] = v_ref[...]

    hw = 2 * dk
    sc = dk ** -0.5
    q = q_ref[...] * sc
    lane = _iota(q.shape, 1)
    qq = jnp.concatenate([jnp.where(lane < dk, q, 0.0), jnp.where(lane >= dk, q, 0.0)], axis=0).astype(BF16)

    def block(j, carry, masked):
        m, l, acc = carry
        start = pl.multiple_of(j * tk, tk)
        kb = k_ref[pl.ds(start, tk), :].astype(BF16)
        vb = v_ref[pl.ds(start, tk), :].astype(BF16)
        s = lax.dot_general(qq, kb, (((1,), (1,)), ((), ())), preferred_element_type=F32)
        if masked:
            row = _iota(s.shape, 0)
            row = jnp.where(row >= tq, row - tq, row) + qi * tq
            s = jnp.where(_iota(s.shape, 1) + j * tk <= row, s, NEG_BIG)
        m_new = jnp.maximum(m, jnp.max(s, axis=-1, keepdims=True))
        alpha = jnp.exp(m - m_new)
        p = jnp.exp(s - m_new)
        l = alpha * l + jnp.sum(p, axis=-1, keepdims=True)
        acc = alpha * acc + jnp.dot(p.astype(BF16), vb, preferred_element_type=F32)
        return m_new, l, acc

    init = (jnp.full((2 * tq, 1), NEG_BIG, F32), jnp.zeros((2 * tq, 1), F32), jnp.zeros((2 * tq, hw), F32))
    n_full = (qi * tq) // tk
    carry = lax.fori_loop(0, n_full, lambda j, c: block(j, c, False), init)
    m, l, acc = block(n_full, carry, True)
    on = acc / l
    lam = _diff_lambda(lq_ref, lam_init)
    o = on[:tq] - lam * on[tq:]
    ms = jnp.mean(o * o, axis=-1, keepdims=True)
    o_ref[...] = (o * lax.rsqrt(ms + EPS) * sub_ref[...] * (1.0 - lam_init)).astype(o_ref.dtype)


def _attn_prompt(z, lambda_qk, subln, li, n_heads, lam_init):
    B, L, _ = z.shape
    hw = 2 * lambda_qk.shape[-1]
    tq = _tile(L, 512, V7X_SUBLANES)
    tk = _tile(L, 512, V7X_SUBLANES)
    assert tk % tq == 0
    H = n_heads
    return pl.pallas_call(
        functools.partial(_attn_prompt_kernel, tq=tq, tk=tk, dk=hw // 2, lam_init=lam_init),
        grid=(B, H, L // tq),
        in_specs=[pl.BlockSpec((None, tq, hw), lambda b, h, i: (b, i, h)),
                  pl.BlockSpec((None, L, hw), lambda b, h, i: (b, 0, H + h)),
                  pl.BlockSpec((None, L, hw), lambda b, h, i: (b, 0, 2 * H + h)),
                  pl.BlockSpec((None,) + lambda_qk.shape[1:], lambda b, h, i: (li, 0, 0)),
                  pl.BlockSpec((None, 1, hw), lambda b, h, i: (li, 0, h))],
        out_specs=[pl.BlockSpec((None, tq, hw), lambda b, h, i: (b, i, h)),
                   pl.BlockSpec((None, hw, L), lambda b, h, i: (b, h, 0)),
                   pl.BlockSpec((None, L, hw), lambda b, h, i: (b, 0, h))],
        out_shape=[jax.ShapeDtypeStruct((B, L, H * hw), BF16),
                   jax.ShapeDtypeStruct((B, H * hw, L), F32),
                   jax.ShapeDtypeStruct((B, L, H * hw), F32)],
        compiler_params=_cparams(3),
        name="attn_prompt",
    )(z, z, z, lambda_qk, subln)


def _attn_sample_kernel(pt_ref, q_ref, kn_ref, vn_ref, lq_ref, sub_ref, *rest, n_pages, n_heads, dk, lam_init):
    kt_refs = rest[:n_pages]
    v_refs = rest[n_pages:2 * n_pages]
    o_ref, m_ref, l_ref, acc_ref = rest[2 * n_pages:]
    s_id = pl.program_id(1)
    H = n_heads
    hw = 2 * dk
    W = H * hw
    page = kt_refs[0].shape[-1]
    sc = dk ** -0.5

    @pl.when(s_id == 0)
    def _():
        m_ref[...] = jnp.full(m_ref.shape, NEG_BIG, F32)
        l_ref[...] = jnp.zeros(l_ref.shape, F32)
        acc_ref[...] = jnp.zeros(acc_ref.shape, F32)

    row = _iota((2 * H, W), 0)
    lane = _iota((2 * H, W), 1)
    r_head = jnp.where(row >= H, row - H, row)
    r_comp = jnp.where(row >= H, 1, 0)
    qsel = jnp.logical_and(lane // hw == r_head, (lane // dk) % 2 == r_comp)
    qf = jnp.where(qsel, q_ref[...] * sc, 0.0)
    qb = qf.astype(BF16)

    s_parts = [jnp.dot(qb, kt_refs[r][...].astype(BF16), preferred_element_type=F32) for r in range(n_pages)]
    s = jnp.concatenate(s_parts, axis=-1) if n_pages > 1 else s_parts[0]
    m = m_ref[...]
    m_new = jnp.maximum(m, jnp.max(s, axis=-1, keepdims=True))
    alpha = jnp.exp(m - m_new)
    p = jnp.exp(s - m_new)
    l_ref[...] = alpha * l_ref[...] + jnp.sum(p, axis=-1, keepdims=True)
    m_ref[...] = m_new
    p_head = r_head[:, :page]
    pv = jnp.zeros((2 * H, hw), F32)
    for r in range(n_pages):
        pr = p[:, r * page:(r + 1) * page]
        pexp = jnp.concatenate([jnp.where(p_head == h, pr, 0.0) for h in range(H)], axis=1).astype(BF16)
        vh = jnp.concatenate([v_refs[r][pl.ds(h, page, stride=H), :] for h in range(H)], axis=0).astype(BF16)
        pv = pv + jnp.dot(pexp, vh, preferred_element_type=F32)
    acc_ref[...] = alpha * acc_ref[...] + pv

    @pl.when(s_id == pl.num_programs(1) - 1)
    def _():
        s_new = jnp.sum(qf * kn_ref[...], axis=-1, keepdims=True)
        m_old = m_ref[...]
        m_fin = jnp.maximum(m_old, s_new)
        a_fin = jnp.exp(m_old - m_fin)
        p_new = jnp.exp(s_new - m_fin)
        l_fin = a_fin * l_ref[...] + p_new
        vn = vn_ref[...]
        acc = a_fin * acc_ref[...] + p_new * jnp.concatenate([vn, vn], axis=0)
        on = acc / l_fin
        lam = _diff_lambda(lq_ref, lam_init)
        o = on[:H] - lam * on[H:]
        ms = jnp.mean(o * o, axis=-1, keepdims=True)
        o_ref[...] = (o * lax.rsqrt(ms + EPS) * sub_ref[...] * (1.0 - lam_init)).astype(o_ref.dtype)


def _attn_sample(z, v_new, cache_kt, cache_v, page_table, lambda_qk, subln, li, n_heads, lam_init):
    Bs = z.shape[0]
    n_phys_total, W, page = cache_kt.shape
    n_pg = page_table.shape[1]
    P = next(c for c in (8, 4, 2, 1) if n_pg % c == 0)
    n_phys = n_phys_total // lambda_qk.shape[0]
    base = li * n_phys
    H = n_heads
    dk = lambda_qk.shape[-1]
    hw = 2 * dk

    def page_spec(shape, r):
        return pl.BlockSpec((None,) + shape, lambda b, s, pt: (base + pt[b, s * P + r], 0, 0))

    zrow = lambda c: pl.BlockSpec((None, 1, W), lambda b, s, pt: (b, 0, c))
    grid_spec = pltpu.PrefetchScalarGridSpec(
        num_scalar_prefetch=1,
        grid=(Bs, n_pg // P),
        in_specs=[zrow(0), zrow(1),
                  pl.BlockSpec((None, H, hw), lambda b, s, pt: (b, 0, 0)),
                  pl.BlockSpec((None,) + lambda_qk.shape[1:], lambda b, s, pt: (li, 0, 0)),
                  pl.BlockSpec((None, H, hw), lambda b, s, pt: (li, 0, 0))]
                 + [page_spec((W, page), r) for r in range(P)]
                 + [page_spec((page * H, hw), r) for r in range(P)],
        out_specs=pl.BlockSpec((None, H, hw), lambda b, s, pt: (b, 0, 0)),
        scratch_shapes=[pltpu.VMEM((2 * H, 1), F32), pltpu.VMEM((2 * H, 1), F32),
                        pltpu.VMEM((2 * H, hw), F32)],
    )
    return pl.pallas_call(
        functools.partial(_attn_sample_kernel, n_pages=P, n_heads=H, dk=dk, lam_init=lam_init),
        grid_spec=grid_spec,
        out_shape=jax.ShapeDtypeStruct((Bs, H, hw), BF16),
        compiler_params=_cparams(2),
        name="attn_sample",
    )(page_table, z, z, v_new, lambda_qk, subln, *([cache_kt] * P), *([cache_v] * P))


def _rope(x, cos, sin):
    half = x.shape[-1] // 2
    x1, x2 = x[:, :half], x[:, half:]
    return jnp.concatenate([x1 * cos - x2 * sin, x1 * sin + x2 * cos], axis=-1)


def _groupnorm_gate(o, gn, g):
    mu = jnp.mean(o, axis=-1, keepdims=True)
    d = o - mu
    var = jnp.mean(d * d, axis=-1, keepdims=True)
    return d * lax.rsqrt(var + EPS) * gn * (g * jax.nn.sigmoid(g))


def _ret_prompt_kernel(q_ref, k_ref, v_ref, g_ref, cos_ref, sin_ref, lg_ref, gn_ref,
                       o_ref, sfin_ref, s_ref, *, chunk, dk):
    c = pl.program_id(2)

    @pl.when(c == 0)
    def _():
        s_ref[...] = jnp.zeros(s_ref.shape, F32)

    cos, sin = cos_ref[...], sin_ref[...]
    qr = _rope(q_ref[...], cos, sin) * dk ** -0.5
    kr = _rope(k_ref[...], cos, sin)
    vb = v_ref[...].astype(BF16)
    lg = lg_ref[...][:, :1]
    C = chunk
    d = _iota((C, C), 0) - _iota((C, C), 1)
    decay = jnp.where(d >= 0, jnp.exp(jnp.maximum(d, 0).astype(F32) * lg), 0.0)
    qrb = qr.astype(BF16)
    scores = lax.dot_general(qrb, kr.astype(BF16), (((1,), (1,)), ((), ())), preferred_element_type=F32) * decay
    intra = jnp.dot(scores.astype(BF16), vb, preferred_element_type=F32)
    t = _iota((C, 1), 0).astype(F32)
    s_old = s_ref[...]
    inter = jnp.dot(qrb, s_old.astype(BF16), preferred_element_type=F32) * jnp.exp((t + 1.0) * lg)
    kd = kr * jnp.exp((C - 1.0 - t) * lg)
    s_new = jnp.exp(C * lg) * s_old + jnp.dot(kd.T.astype(BF16), vb, preferred_element_type=F32)
    s_ref[...] = s_new
    sfin_ref[...] = s_new
    o_ref[...] = _groupnorm_gate(intra + inter, gn_ref[...], g_ref[...]).astype(o_ref.dtype)


def _ret_prompt(z, cos_tab, sin_tab, log_gamma, gn_b, li, n_heads_a, n_heads_b):
    B, L, _ = z.shape
    dk = 2 * cos_tab.shape[-1]
    HB = n_heads_b
    assert (3 * n_heads_a * V7X_LANES) % dk == 0
    off = 3 * n_heads_a * V7X_LANES // dk
    C = _tile(L, 512, V7X_SUBLANES)
    zb = lambda g: pl.BlockSpec((None, C, dk), lambda b, h, c: (b, c, off + g * HB + h))
    return pl.pallas_call(
        functools.partial(_ret_prompt_kernel, chunk=C, dk=dk),
        grid=(B, HB, L // C),
        in_specs=[zb(0), zb(1), zb(2), zb(3),
                  pl.BlockSpec((C, dk // 2), lambda b, h, c: (c, 0)),
                  pl.BlockSpec((C, dk // 2), lambda b, h, c: (c, 0)),
                  pl.BlockSpec((None, 1, V7X_LANES), lambda b, h, c: (h, 0, 0)),
                  pl.BlockSpec((None, 1, dk), lambda b, h, c: (li, 0, h))],
        out_specs=[pl.BlockSpec((None, C, dk), lambda b, h, c: (b, c, h)),
                   pl.BlockSpec((None, None, dk, dk), lambda b, h, c: (b, h, 0, 0))],
        out_shape=[jax.ShapeDtypeStruct((B, L, HB * dk), BF16),
                   jax.ShapeDtypeStruct((B, HB, dk, dk), F32)],
        scratch_shapes=[pltpu.VMEM((dk, dk), F32)],
        compiler_params=_cparams(3),
        name="ret_prompt",
    )(z, z, z, z, cos_tab, sin_tab, log_gamma, gn_b)


def _ret_sample_kernel(q_ref, k_ref, v_ref, g_ref, cos_ref, sin_ref, lg_ref, gn_ref, s0_ref,
                       o_ref, snew_ref, *, dk):
    cos, sin = cos_ref[...], sin_ref[...]
    qr = _rope(q_ref[...], cos, sin) * dk ** -0.5
    kr = _rope(k_ref[...], cos, sin)
    v = v_ref[...]
    gamma = jnp.exp(lg_ref[...][:, :1])
    eye = _iota((dk, dk), 0) == _iota((dk, dk), 1)
    kcol = jnp.sum(jnp.where(eye, kr, 0.0), axis=-1, keepdims=True)
    qcol = jnp.sum(jnp.where(eye, qr, 0.0), axis=-1, keepdims=True)
    s0 = s0_ref[...]
    snew_ref[...] = gamma * s0 + kcol * v
    inter = gamma * jnp.sum(qcol * s0, axis=0, keepdims=True)
    intra = jnp.sum(qr * kr, axis=-1, keepdims=True) * v
    o_ref[...] = _groupnorm_gate(intra + inter, gn_ref[...], g_ref[...]).astype(o_ref.dtype)


def _ret_sample(z, cos_row, sin_row, log_gamma, gn_b, state, li, n_heads_a, n_heads_b):
    Bs = z.shape[0]
    dk = 2 * cos_row.shape[-1]
    HB = n_heads_b
    off = 3 * n_heads_a * V7X_LANES // dk
    zb = lambda g: pl.BlockSpec((None, 1, dk), lambda b, h: (b, 0, off + g * HB + h))
    return pl.pallas_call(
        functools.partial(_ret_sample_kernel, dk=dk),
        grid=(Bs, HB),
        in_specs=[zb(0), zb(1), zb(2), zb(3),
                  pl.BlockSpec((1, dk // 2), lambda b, h: (0, 0)),
                  pl.BlockSpec((1, dk // 2), lambda b, h: (0, 0)),
                  pl.BlockSpec((None, 1, V7X_LANES), lambda b, h: (h, 0, 0)),
                  pl.BlockSpec((None, 1, dk), lambda b, h: (li, 0, h)),
                  pl.BlockSpec((None, None, dk, dk), lambda b, h: (li * Bs + b, h, 0, 0))],
        out_specs=[pl.BlockSpec((None, 1, dk), lambda b, h: (b, 0, h)),
                   pl.BlockSpec((None, None, dk, dk), lambda b, h: (b, h, 0, 0))],
        out_shape=[jax.ShapeDtypeStruct((Bs, 1, HB * dk), BF16),
                   jax.ShapeDtypeStruct((Bs, HB, dk, dk), F32)],
        compiler_params=_cparams(2),
        name="ret_sample",
    )(z, z, z, z, cos_row, sin_row, log_gamma, gn_b, state)


def _softplus(z):
    return jnp.maximum(z, 0.0) + jnp.log1p(jnp.exp(-jnp.abs(z)))


def _lru_coeffs(xc, wa_ref, wx_ref, ba_ref, bx_ref, lam_ref):
    xcb = xc.astype(BF16)
    r = jax.nn.sigmoid(jnp.dot(xcb, wa_ref[...].astype(BF16), preferred_element_type=F32) + ba_ref[...])
    gi = jax.nn.sigmoid(jnp.dot(xcb, wx_ref[...].astype(BF16), preferred_element_type=F32) + bx_ref[...])
    log_a = -LRU_C * r * _softplus(-lam_ref[...])
    a = jnp.exp(log_a)
    t = jnp.tanh(log_a)
    mult = jnp.sqrt(jnp.maximum(-2.0 * t / (1.0 - t), 0.0))
    return a, mult * gi * xc


def _lru_prompt_kernel(gate_ref, x_ref, cw_ref, cb_ref, wa_ref, wx_ref, ba_ref, bx_ref, lam_ref,
                       tail0_ref, h0_ref, y_ref, hl_ref, tail_ref, hc_ref, *, tt, conv_w):
    t_id = pl.program_id(2)

    @pl.when(t_id == 0)
    def _():
        tail_ref[...] = tail0_ref[...]
        hc_ref[...] = h0_ref[...]

    x = x_ref[...]
    tl = tail_ref[...]
    row8 = _iota(tl.shape, 0)
    xc = cb_ref[...] + cw_ref[conv_w - 1:conv_w, :] * x
    for s in range(1, conv_w):
        xs = pltpu.roll(x, s, 0)
        head = jnp.where(row8 >= s, xs[:V7X_SUBLANES], pltpu.roll(tl, s, 0))
        xs = jnp.concatenate([head, xs[V7X_SUBLANES:]], axis=0) if tt > V7X_SUBLANES else head
        xc = xc + cw_ref[conv_w - 1 - s:conv_w - s, :] * xs
    tail_ref[...] = x[tt - V7X_SUBLANES:, :]

    a, b = _lru_coeffs(xc, wa_ref, wx_ref, ba_ref, bx_ref, lam_ref)
    row = _iota(a.shape, 0)
    s = 1
    while s < tt:
        if s % V7X_SUBLANES == 0:
            a_s = jnp.concatenate([jnp.ones((s, a.shape[1]), F32), a[:tt - s]], axis=0)
            b_s = jnp.concatenate([jnp.zeros((s, a.shape[1]), F32), b[:tt - s]], axis=0)
        else:
            keep = row >= s
            a_s = jnp.where(keep, pltpu.roll(a, s, 0), 1.0)
            b_s = jnp.where(keep, pltpu.roll(b, s, 0), 0.0)
        b = b + a * b_s
        a = a * a_s
        s *= 2
    h = b + a * hc_ref[...]
    hc_ref[...] = h[tt - 1:tt, :]
    hl_ref[...] = h[tt - 1:tt, :]
    y_ref[...] = (jax.nn.gelu(gate_ref[...], approximate=True) * h).astype(y_ref.dtype)


def _lru_prompt(z, conv_w, conv_b, w_ga, w_gx, b_ga, b_gx, lam, tail0, h0, li):
    B, L, W2 = z.shape
    W = W2 // 2
    bw = w_ga.shape[-1]
    NB = W // bw
    CW = conv_w.shape[1]
    tt = _tile(L, 512, V7X_SUBLANES)
    vec = lambda: pl.BlockSpec((None, 1, bw), lambda b, n, t: (li, 0, n))
    gw = lambda: pl.BlockSpec((None, bw, bw), lambda b, n, t: (li * NB + n, 0, 0))
    return pl.pallas_call(
        functools.partial(_lru_prompt_kernel, tt=tt, conv_w=CW),
        grid=(B, NB, L // tt),
        in_specs=[pl.BlockSpec((None, tt, bw), lambda b, n, t: (b, t, n)),
                  pl.BlockSpec((None, tt, bw), lambda b, n, t: (b, t, NB + n)),
                  pl.BlockSpec((None, CW, bw), lambda b, n, t: (li, 0, n)),
                  vec(), gw(), gw(), vec(), vec(), vec(),
                  pl.BlockSpec((None, V7X_SUBLANES, bw), lambda b, n, t: (b, 0, n)),
                  pl.BlockSpec((None, 1, bw), lambda b, n, t: (b, 0, n))],
        out_specs=[pl.BlockSpec((None, tt, bw), lambda b, n, t: (b, t, n)),
                   pl.BlockSpec((None, 1, bw), lambda b, n, t: (b, 0, n))],
        out_shape=[jax.ShapeDtypeStruct((B, L, W), BF16), jax.ShapeDtypeStruct((B, 1, W), F32)],
        scratch_shapes=[pltpu.VMEM((V7X_SUBLANES, bw), F32), pltpu.VMEM((1, bw), F32)],
        compiler_params=_cparams(3),
        name="lru_prompt",
    )(z, z, conv_w, conv_b, w_ga, w_gx, b_ga, b_gx, lam, tail0, h0)


def _lru_sample_kernel(gate_ref, x_ref, cw_ref, cb_ref, wa_ref, wx_ref, ba_ref, bx_ref, lam_ref,
                       buf_ref, h0_ref, y_ref, h_ref, *, conv_w):
    x = x_ref[...]
    xc = cb_ref[...] + cw_ref[conv_w - 1:conv_w, :] * x
    for j in range(conv_w - 1):
        xc = xc + cw_ref[j:j + 1, :] * buf_ref[j]
    a, b = _lru_coeffs(xc, wa_ref, wx_ref, ba_ref, bx_ref, lam_ref)
    h = a * h0_ref[...] + b
    h_ref[...] = h
    y_ref[...] = (jax.nn.gelu(gate_ref[...], approximate=True) * h).astype(y_ref.dtype)


def _lru_sample(z, conv_w, conv_b, w_ga, w_gx, b_ga, b_gx, lam, buf, h0, li):
    Bs, W2 = z.shape
    W = W2 // 2
    bw = w_ga.shape[-1]
    NB = W // bw
    CW = conv_w.shape[1]
    vec = lambda: pl.BlockSpec((None, 1, bw), lambda n: (li, 0, n))
    gw = lambda: pl.BlockSpec((None, bw, bw), lambda n: (li * NB + n, 0, 0))
    return pl.pallas_call(
        functools.partial(_lru_sample_kernel, conv_w=CW),
        grid=(NB,),
        in_specs=[pl.BlockSpec((Bs, bw), lambda n: (0, n)),
                  pl.BlockSpec((Bs, bw), lambda n: (0, NB + n)),
                  pl.BlockSpec((None, CW, bw), lambda n: (li, 0, n)),
                  vec(), gw(), gw(), vec(), vec(), vec(),
                  pl.BlockSpec((CW - 1, Bs, bw), lambda n: (0, 0, n)),
                  pl.BlockSpec((Bs, bw), lambda n: (0, n))],
        out_specs=[pl.BlockSpec((Bs, bw), lambda n: (0, n)),
                   pl.BlockSpec((Bs, bw), lambda n: (0, n))],
        out_shape=[jax.ShapeDtypeStruct((Bs, W), BF16), jax.ShapeDtypeStruct((Bs, W), F32)],
        compiler_params=_cparams(1),
        name="lru_sample",
    )(z, z, conv_w, conv_b, w_ga, w_gx, b_ga, b_gx, lam, buf, h0)


MOE_ROW_BLOCK = 512
MOE_SUB_BLOCK = 128


def _valid_rows_only(n_valid, o_ref, compute):
    tm = o_ref.shape[0]

    @pl.when(n_valid == tm)
    def _():
        o_ref[...] = compute(slice(None))

    for s in range(tm // MOE_SUB_BLOCK):
        rows = slice(s * MOE_SUB_BLOCK, (s + 1) * MOE_SUB_BLOCK)

        @pl.when(jnp.logical_and(n_valid < tm, n_valid > s * MOE_SUB_BLOCK))
        def _(rows=rows):
            o_ref[rows, :] = compute(rows)

        @pl.when(n_valid <= s * MOE_SUB_BLOCK)
        def _(rows=rows):
            o_ref[rows, :] = jnp.zeros((MOE_SUB_BLOCK, o_ref.shape[1]), o_ref.dtype)


def _stream_weight_panels(bf_ref, sg_ref, se_ref, ns_ref, w_hbms, wf_refs, wb_refs, sem, *, e0, tn):
    j = pl.program_id(0)
    rb = pl.program_id(1)
    n_seg = ns_ref[0]

    def copies(k, jcol, slot):
        col = pl.multiple_of(jcol * tn, tn)
        return [pltpu.make_async_copy(w.at[e0 + se_ref[k], :, pl.ds(col, tn)], wf.at[slot], sem.at[i, slot])
                for i, (w, wf) in enumerate(zip(w_hbms, wf_refs))]

    @pl.when(bf_ref[rb] == 1)
    def _():
        k = sg_ref[rb]
        slot = jnp.bitwise_and(j * n_seg + k, 1)

        @pl.when(jnp.logical_and(j == 0, k == 0))
        def _():
            for c in copies(0, 0, 0):
                c.start()

        for c in copies(k, j, slot):
            c.wait()
        for wf, wb in zip(wf_refs, wb_refs):
            wb[...] = wf[slot].astype(BF16)

        last = k + 1 == n_seg
        k2 = jnp.where(last, 0, k + 1)
        j2 = jnp.where(last, j + 1, j)

        @pl.when(j2 < pl.num_programs(0))
        def _():
            for c in copies(k2, j2, 1 - slot):
                c.start()


def _moe_up_kernel(bf_ref, bv_ref, sg_ref, se_ref, ns_ref, x_ref, w1_hbm, w3_hbm, o_ref,
                   w1f_ref, w3f_ref, w1b_ref, w3b_ref, sem, *, e0, tn):
    rb = pl.program_id(1)
    _stream_weight_panels(bf_ref, sg_ref, se_ref, ns_ref, [w1_hbm, w3_hbm], [w1f_ref, w3f_ref],
                          [w1b_ref, w3b_ref], sem, e0=e0, tn=tn)

    def compute(rows):
        x = x_ref[rows, :]
        u = jnp.dot(x, w1b_ref[...], preferred_element_type=F32)
        v = jnp.dot(x, w3b_ref[...], preferred_element_type=F32)
        return (u * jax.nn.sigmoid(u) * v).astype(o_ref.dtype)

    _valid_rows_only(bv_ref[rb], o_ref, compute)


def _moe_down_kernel(bf_ref, bv_ref, sg_ref, se_ref, ns_ref, a_ref, w_hbm, o_ref, wf_ref, wb_ref, sem, *, e0, tn):
    rb = pl.program_id(1)
    _stream_weight_panels(bf_ref, sg_ref, se_ref, ns_ref, [w_hbm], [wf_ref], [wb_ref], sem, e0=e0, tn=tn)

    def compute(rows):
        return jnp.dot(a_ref[rows, :], wb_ref[...], preferred_element_type=F32)

    _valid_rows_only(bv_ref[rb], o_ref, compute)


def _moe_up(xs, w1, w3, seg, li, n_experts):
    n_rows, D = xs.shape
    F = w1.shape[-1]
    tm = MOE_ROW_BLOCK
    tn = _tile(F, 1024, V7X_LANES)
    rows = lambda j, rb, *_: (rb, 0)
    grid_spec = pltpu.PrefetchScalarGridSpec(
        num_scalar_prefetch=5,
        grid=(F // tn, n_rows // tm),
        in_specs=[pl.BlockSpec((tm, D), rows), pl.BlockSpec(memory_space=pl.ANY),
                  pl.BlockSpec(memory_space=pl.ANY)],
        out_specs=pl.BlockSpec((tm, tn), lambda j, rb, *_: (rb, j)),
        scratch_shapes=[pltpu.VMEM((2, D, tn), F32), pltpu.VMEM((2, D, tn), F32),
                        pltpu.VMEM((D, tn), BF16), pltpu.VMEM((D, tn), BF16),
                        pltpu.SemaphoreType.DMA((2, 2))],
    )
    return pl.pallas_call(
        functools.partial(_moe_up_kernel, e0=li * n_experts, tn=tn),
        grid_spec=grid_spec,
        out_shape=jax.ShapeDtypeStruct((n_rows, F), BF16),
        compiler_params=_cparams(2),
        name="moe_up",
    )(*seg, xs, w1, w3)


def _moe_down(hh, w2, seg, li, n_experts):
    n_rows, F = hh.shape
    D = w2.shape[-1]
    tm = MOE_ROW_BLOCK
    tn = _tile(D, 512, V7X_LANES)
    grid_spec = pltpu.PrefetchScalarGridSpec(
        num_scalar_prefetch=5,
        grid=(D // tn, n_rows // tm),
        in_specs=[pl.BlockSpec((tm, F), lambda j, rb, *_: (rb, 0)), pl.BlockSpec(memory_space=pl.ANY)],
        out_specs=pl.BlockSpec((tm, tn), lambda j, rb, *_: (rb, j)),
        scratch_shapes=[pltpu.VMEM((2, F, tn), F32), pltpu.VMEM((F, tn), BF16),
                        pltpu.SemaphoreType.DMA((1, 2))],
    )
    return pl.pallas_call(
        functools.partial(_moe_down_kernel, e0=li * n_experts, tn=tn),
        grid_spec=grid_spec,
        out_shape=jax.ShapeDtypeStruct((n_rows, D), F32),
        compiler_params=_cparams(2),
        name="moe_down",
    )(*seg, hh, w2)


def _dispatch_kernel(src_ref, bv_ref, tab_ref, o_ref, buf_ref, *, tm):
    rb = pl.program_id(1)

    @pl.when(bv_ref[rb] > 0)
    def _():
        base = rb * tm

        def body(r, carry):
            buf_ref[pl.ds(r, 1), :] = tab_ref[pl.ds(src_ref[base + r], 1), :]
            return carry

        lax.fori_loop(0, tm, body, 0, unroll=8)
        o_ref[...] = buf_ref[...].astype(o_ref.dtype)

    @pl.when(bv_ref[rb] == 0)
    def _():
        o_ref[...] = jnp.zeros(o_ref.shape, o_ref.dtype)


def _dispatch(table, src, blk_valid):
    n_tok, D = table.shape
    n_rows = src.shape[0]
    tm = MOE_ROW_BLOCK
    tc = D
    while tc > V7X_LANES and n_tok * tc * 4 > 36 * 1024 * 1024:
        tc = _tile(D, tc // 2, V7X_LANES)
    grid_spec = pltpu.PrefetchScalarGridSpec(
        num_scalar_prefetch=2,
        grid=(D // tc, n_rows // tm),
        in_specs=[pl.BlockSpec((n_tok, tc), lambda j, rb, src, nu: (0, j), pipeline_mode=pl.Buffered(1))],
        out_specs=pl.BlockSpec((tm, tc), lambda j, rb, src, nu: (rb, j)),
        scratch_shapes=[pltpu.VMEM((tm, tc), F32)],
    )
    return pl.pallas_call(
        functools.partial(_dispatch_kernel, tm=tm),
        grid_spec=grid_spec,
        out_shape=jax.ShapeDtypeStruct((n_rows, D), BF16),
        compiler_params=_cparams(2),
        name="moe_dispatch",
    )(src, blk_valid, table)


def _combine_kernel(d_ref, tab_ref, route_ref, o_ref, a_ref, b_ref, *, tm):
    base = pl.program_id(1) * (tm * TOP_K)

    def body(t, carry):
        a_ref[pl.ds(t, 1), :] = tab_ref[pl.ds(d_ref[base + TOP_K * t], 1), :]
        b_ref[pl.ds(t, 1), :] = tab_ref[pl.ds(d_ref[base + TOP_K * t + 1], 1), :]
        return carry

    lax.fori_loop(0, tm, body, 0, unroll=8)
    rt = route_ref[...]
    o_ref[...] = rt[:, TOP_K:TOP_K + 1] * a_ref[...] + rt[:, TOP_K + 1:TOP_K + 2] * b_ref[...]


def _combine(ys, dest, route):
    assert TOP_K == 2
    n_rows, D = ys.shape
    n_tok = route.shape[0]
    tm = _tile(n_tok, 512, V7X_SUBLANES)
    tc = _tile(D, 512, V7X_LANES)
    while tc > V7X_LANES and n_rows * tc * 4 > 44 * 1024 * 1024:
        tc = _tile(D, tc // 2, V7X_LANES)
    grid_spec = pltpu.PrefetchScalarGridSpec(
        num_scalar_prefetch=1,
        grid=(D // tc, n_tok // tm),
        in_specs=[pl.BlockSpec((n_rows, tc), lambda j, i, d: (0, j), pipeline_mode=pl.Buffered(1)),
                  pl.BlockSpec((tm, V7X_LANES), lambda j, i, d: (i, 0))],
        out_specs=pl.BlockSpec((tm, tc), lambda j, i, d: (i, j)),
        scratch_shapes=[pltpu.VMEM((tm, tc), F32), pltpu.VMEM((tm, tc), F32)],
    )
    return pl.pallas_call(
        functools.partial(_combine_kernel, tm=tm),
        grid_spec=grid_spec,
        out_shape=jax.ShapeDtypeStruct((n_tok, D), F32),
        compiler_params=_cparams(2),
        name="moe_combine",
    )(dest, ys, route)


def _residual_kernel(x_ref, y_ref, g_ref, o_ref):
    o_ref[...] = x_ref[...] + g_ref[...] * y_ref[...]


def _residual_norm_kernel(x_ref, y_ref, g_ref, nf_ref, o_ref):
    x = x_ref[...] + g_ref[...] * y_ref[...]
    ms = jnp.mean(x * x, axis=-1, keepdims=True)
    o_ref[...] = x * lax.rsqrt(ms + EPS) * nf_ref[...]


def _norm_kernel(x_ref, nf_ref, o_ref):
    x = x_ref[...]
    ms = jnp.mean(x * x, axis=-1, keepdims=True)
    o_ref[...] = x * lax.rsqrt(ms + EPS) * nf_ref[...]


def _residual(x, y_tok, row0, gate, norm_g):
    B, L, D = x.shape
    R = gate.shape[1]
    tm = L if R == L else _tile(L, 512, V7X_SUBLANES)
    nb = L // tm
    assert row0 % tm == 0
    blk0 = row0 // tm
    blk = pl.BlockSpec((None, tm, D), lambda b, i: (b, i, 0))
    in_specs = [blk, pl.BlockSpec((tm, D), lambda b, i: (blk0 + b * nb + i, 0)),
                pl.BlockSpec((None, R, D), lambda b, i: (b, 0, 0))]
    args = [x, y_tok, gate]
    kern = _residual_kernel
    if norm_g is not None:
        in_specs.append(pl.BlockSpec((1, D), lambda b, i: (0, 0)))
        args.append(norm_g)
        kern = _residual_norm_kernel
    return pl.pallas_call(
        kern, grid=(B, L // tm), in_specs=in_specs, out_specs=blk,
        out_shape=jax.ShapeDtypeStruct((B, L, D), F32),
        compiler_params=_cparams(2), name="residual",
    )(*args)


def _final_norm(x, norm_g):
    B, L, D = x.shape
    tm = _tile(L, 512, V7X_SUBLANES)
    blk = pl.BlockSpec((None, tm, D), lambda b, i: (b, i, 0))
    return pl.pallas_call(
        _norm_kernel, grid=(B, L // tm),
        in_specs=[blk, pl.BlockSpec((1, D), lambda b, i: (0, 0))], out_specs=blk,
        out_shape=jax.ShapeDtypeStruct((B, L, D), F32),
        compiler_params=_cparams(2), name="final_norm",
    )(x, norm_g)


def _split_mods(m, n_p, n_s):
    D = m.shape[-1] // 6
    mp = [m[:n_p, k * D:(k + 1) * D][:, None, :] for k in range(6)]
    ms = [m[n_p:n_p + n_s, k * D:(k + 1) * D][None, :, :] for k in range(6)]
    return mp, ms


def _stack_cond(c_prompt, c_sample):
    c_all = jnp.concatenate([c_prompt, c_sample], axis=0)
    pad = (-c_all.shape[0]) % V7X_SUBLANES
    if pad:
        c_all = jnp.concatenate([c_all, jnp.zeros((pad, c_all.shape[1]), c_all.dtype)], axis=0)
    return c_all


def _moe(table, route, w1, w3, w2, li, n_experts):
    T = table.shape[0]
    E = n_experts
    tm = MOE_ROW_BLOCK
    e_flat = route[:, :TOP_K].astype(jnp.int32).reshape(-1)
    n_slots = T * TOP_K
    onehot = (e_flat[:, None] == jnp.arange(E, dtype=jnp.int32)[None, :]).astype(jnp.int32)
    csum = jnp.cumsum(onehot, axis=0)
    counts = csum[-1]
    rank = jnp.sum((csum - onehot) * onehot, axis=-1)
    padded = (counts + tm - 1) // tm * tm
    pends = jnp.cumsum(padded)
    pstarts = pends - padded
    rem = counts % tm
    rem_s = jnp.sum(onehot * rem[None, :], axis=-1)
    pos = jnp.where(rank < rem_s, rank, rank + (tm - rem_s) % tm)
    dest = (jnp.sum(onehot * pstarts[None, :], axis=-1) + pos).astype(jnp.int32)
    n_blocks = -(-n_slots // tm) + E
    n_rows = n_blocks * tm
    tok_flat = jnp.arange(n_slots, dtype=jnp.int32) // TOP_K
    src = jnp.zeros((n_rows,), jnp.int32).at[dest].set(tok_flat)
    blk_start = jnp.arange(n_blocks, dtype=jnp.int32) * tm
    blk_e = jnp.minimum(jnp.sum((blk_start[:, None] >= pends[None, :]).astype(jnp.int32), axis=-1), E - 1)
    blk_first = jnp.concatenate([jnp.ones((1,), jnp.int32), (blk_e[1:] != blk_e[:-1]).astype(jnp.int32)])
    rem_b = jnp.sum((blk_e[:, None] == jnp.arange(E, dtype=jnp.int32)[None, :]) * rem[None, :], axis=-1)
    blk_valid = jnp.where(blk_start >= pends[-1], 0,
                          jnp.where(jnp.logical_and(blk_first == 1, rem_b > 0), rem_b, tm)).astype(jnp.int32)
    present = (counts > 0).astype(jnp.int32)
    seg_of_e = jnp.cumsum(present) - 1
    eids = jnp.arange(E, dtype=jnp.int32)
    seg_e = jnp.sum(eids[None, :] * present[None, :] * (seg_of_e[None, :] == eids[:, None]), axis=-1).astype(jnp.int32)
    n_seg = jnp.sum(present, keepdims=True).astype(jnp.int32)
    seg_first = (blk_first * (blk_valid > 0)).astype(jnp.int32)
    seg_idx = jnp.sum((blk_e[:, None] == eids[None, :]) * seg_of_e[None, :], axis=-1).astype(jnp.int32)
    seg = (seg_first, blk_valid, seg_idx, seg_e, n_seg)
    xs = _dispatch(table, src, blk_valid)
    hh = _moe_up(xs, w1, w3, seg, li, E)
    ys = _moe_down(hh, w2, seg, li, E)
    return _combine(ys, dest, route)


def kernel(x_prompt, x_sample, c_prompt, c_sample, cache_k, cache_v, state_ret, state_conv, state_lru, page_table, ada_w_even, ada_b_even, norm_mix_even, norm_ffn_even, w_in_even, lambda_qk, subln_a, gn_b, w_out_even, ffn_w1, ffn_w3, ffn_w2, ada_w_odd, ada_b_odd, norm_mix_odd, norm_ffn_odd, w_in_odd, conv_w, conv_b, w_gate_a, b_gate_a, w_gate_x, b_gate_x, lru_lambda, w_out_odd, router, moe_w1, moe_w3, moe_w2, norm_final):
    Bp, Lp, D = x_prompt.shape
    Bs, Ls, _ = x_sample.shape
    assert Ls == 1, "the sample group carries one new token per sequence"
    n_even, n_odd = w_in_even.shape[0], w_in_odd.shape[0]
    depth = n_even + n_odd
    _, n_phys, page, HA, _, dka = cache_k.shape
    HB, dkb = state_ret.shape[2], state_ret.shape[3]
    past_len = page_table.shape[1] * page
    E = router.shape[-1]
    CW = conv_w.shape[1]
    W = conv_w.shape[-1]

    half = dkb // 2
    freqs = ROPE_BASE ** (-jnp.arange(half, dtype=F32) / half)
    ang_p = jnp.arange(Lp).astype(F32)[:, None] * freqs[None, :]
    ang_s = (past_len + jnp.arange(Ls)).astype(F32)[:, None] * freqs[None, :]
    cos_p, sin_p, cos_s, sin_s = jnp.cos(ang_p), jnp.sin(ang_p), jnp.cos(ang_s), jnp.sin(ang_s)
    log_gamma = jnp.log1p(-jnp.exp2(-5.0 - jnp.arange(HB, dtype=F32)))
    log_gamma = jnp.broadcast_to(log_gamma[:, None, None], (HB, 1, V7X_LANES))

    c_all = _stack_cond(c_prompt, c_sample)
    row3 = lambda a: a.reshape(a.shape[0], 1, a.shape[-1])
    cache_kt = jnp.transpose(cache_k, (0, 1, 3, 4, 5, 2)).reshape(n_even * n_phys, HA * 2 * dka, page)
    cache_v2 = cache_v.reshape(n_even * n_phys, page * HA, 2 * dka)
    subln3 = subln_a.reshape(n_even, HA, 2 * dka)
    state_ret2 = state_ret.reshape(n_even * Bs, HB, dkb, dkb)
    w_ga2 = w_gate_a.reshape(-1, w_gate_a.shape[-2], w_gate_a.shape[-1])
    w_gx2 = w_gate_x.reshape(-1, w_gate_x.shape[-2], w_gate_x.shape[-1])
    moe_w1f = moe_w1.reshape(-1, moe_w1.shape[-2], moe_w1.shape[-1])
    moe_w3f = moe_w3.reshape(-1, moe_w3.shape[-2], moe_w3.shape[-1])
    moe_w2f = moe_w2.reshape(-1, moe_w2.shape[-2], moe_w2.shape[-1])
    router_pad = jnp.concatenate(
        [router, jnp.zeros(router.shape[:-1] + (V7X_LANES - E,), router.dtype)], axis=-1)
    nf = norm_final.reshape(1, D)

    xp = x_prompt
    xs = x_sample.reshape(1, Bs, D)
    kp_l, vp_l, rp_l, cp_l, lp_l = [], [], [], [], []
    ks_l, vs_l, rs_l, cs_l, ls_l = [], [], [], [], []
    normed = False
    for l in range(depth):
        i = l // 2
        last = l == depth - 1
        if l % 2 == 0:
            lam_init = 0.8 - 0.6 * math.exp(-0.3 * l)
            m = _adaln(c_all, ada_w_even, row3(ada_b_even), i)
            (sh1, sc1, g1, sh2, sc2, g2), (sh1s, sc1s, g1s, sh2s, sc2s, g2s) = _split_mods(m, Bp, Bs)
            gmix, gffn = norm_mix_even[i][None, :], norm_ffn_even[i][None, :]
            wa = HA * 2 * dka
            h = _modulate(xp, gmix, sh1, sc1)
            hs = _modulate(xs, gmix, sh1s, sc1s)
            z, zs = _dense("mm_in_even", [h], [w_in_even], i, F32, [hs[0]])
            oa, k_t, v_new = _attn_prompt(z, lambda_qk, row3(subln_a), i, HA, lam_init)
            kp_l.append(jnp.transpose(k_t.reshape(Bp, HA, 2, dka, Lp), (0, 4, 1, 2, 3)))
            vp_l.append(v_new.reshape(Bp, Lp, HA, 2 * dka))
            ob, s_fin = _ret_prompt(z, cos_p, sin_p, log_gamma, row3(gn_b), i, HA, HB)
            rp_l.append(s_fin)
            zs3 = zs.reshape(Bs, 1, zs.shape[-1])
            ks_l.append(zs3[:, :, wa:2 * wa].reshape(Bs, Ls, HA, 2, dka))
            v_new = zs3[:, :, 2 * wa:3 * wa].reshape(Bs, HA, 2 * dka)
            vs_l.append(v_new.reshape(Bs, Ls, HA, 2 * dka))
            oas = _attn_sample(zs3, v_new, cache_kt, cache_v2, page_table, lambda_qk, subln3, i, HA, lam_init)
            obs, s_new = _ret_sample(zs3, cos_s, sin_s, log_gamma, row3(gn_b), state_ret2, i, HA, HB)
            rs_l.append(s_new)
            xp, xs2, h, hs = _out_proj("out_even", [oa, ob], w_out_even, i, (xp, g1, sh2, sc2),
                                       [oas.reshape(Bs, wa), obs.reshape(Bs, D - wa)],
                                       (xs[0], g1s[0], sh2s[0], sc2s[0]), gffn)
            xs = xs2[None]
            hh, hhs = _dense("mm_swiglu", [h], [ffn_w1, ffn_w3], i, BF16, [hs])
            xp, xs2 = _dense("mm_ffn_down", [hh], [ffn_w2], i, F32, [hhs], (xp, g2), (xs[0], g2s[0]))
            xs = xs2[None]
        else:
            m = _adaln(c_all, ada_w_odd, row3(ada_b_odd), i)
            (sh1, sc1, g1, sh2, sc2, g2), (sh1s, sc1s, g1s, sh2s, sc2s, g2s) = _split_mods(m, Bp, Bs)
            gmix, gffn = norm_mix_odd[i][None, :], norm_ffn_odd[i][None, :]
            lru_args = (conv_w, row3(conv_b), w_ga2, w_gx2, row3(b_gate_a), row3(b_gate_x), row3(lru_lambda))
            h = _modulate(xp, gmix, sh1, sc1)
            hs = _modulate(xs, gmix, sh1s, sc1s)
            z, zs = _dense("mm_in_odd", [h], [w_in_odd], i, F32, [hs[0]])
            tail0 = jnp.zeros((Bp, V7X_SUBLANES, W), F32)
            h0 = jnp.zeros((Bp, 1, W), F32)
            y, h_last = _lru_prompt(z, *lru_args, tail0, h0, i)
            cp_l.append(z[:, Lp - (CW - 1):, W:])
            lp_l.append(h_last.reshape(Bp, W))
            buf = state_conv[i]
            ys, h_new = _lru_sample(zs, *lru_args, jnp.swapaxes(buf, 0, 1), state_lru[i], i)
            cs_l.append(jnp.concatenate([buf[:, 1:], zs[:, None, W:]], axis=1))
            ls_l.append(h_new)
            xp, xs2, table, route = _out_proj("out_odd", [y], w_out_odd, i, (xp, g1, sh2, sc2), [ys],
                                              (xs[0], g1s[0], sh2s[0], sc2s[0]), gffn, router_pad[i], E)
            xs = xs2[None]
            Tp = Bp * Lp
            y_tok = _moe(table, route, moe_w1f, moe_w3f, moe_w2f, i, E)
            xp = _residual(xp, y_tok, 0, g2, nf if last else None)
            xs = _residual(xs, y_tok, Tp, g2s, nf if last else None)
            normed = last
    if not normed:
        xp = _final_norm(xp, nf)
        xs = _final_norm(xs, nf)
    y_prompt = xp.astype(x_prompt.dtype)
    y_sample = xs.reshape(Bs, Ls, D).astype(x_sample.dtype)
    return (y_prompt, y_sample,
            jnp.stack(kp_l), jnp.stack(vp_l), jnp.stack(rp_l), jnp.stack(cp_l), jnp.stack(lp_l),
            jnp.stack(ks_l), jnp.stack(vs_l), jnp.stack(rs_l), jnp.stack(cs_l), jnp.stack(ls_l))
```

```python
import functools
import math

import jax
import jax.numpy as jnp
from jax import lax
from jax.experimental import pallas as pl
from jax.experimental.pallas import tpu as pltpu

F32 = jnp.float32
BF16 = jnp.bfloat16

EPS = 1e-6
ROPE_BASE = 10000.0
LRU_C = 8.0
TOP_K = 2
NEG_BIG = -1e30

V7X_LANES = 128
V7X_SUBLANES = 8
V7X_VMEM_LIMIT_BYTES = 56 * 1024 * 1024
V7X_RESIDENT_PANEL_BYTES = V7X_VMEM_LIMIT_BYTES * 4 // 5


def _cparams(n_axes):
    return pltpu.CompilerParams(dimension_semantics=("arbitrary",) * n_axes,
                                vmem_limit_bytes=V7X_VMEM_LIMIT_BYTES)


def _tile(n, pref, align):
    if n <= pref:
        return n
    t = (pref // align) * align
    while t > align and n % t:
        t -= align
    assert n % t == 0, (n, pref, align)
    return t


def _iota(shape, dim):
    return lax.broadcasted_iota(jnp.int32, shape, dim)


def _adaln_kernel(c_ref, w_ref, b_ref, o_ref):
    c = c_ref[...]
    s = (c * jax.nn.sigmoid(c)).astype(BF16)
    o_ref[...] = jnp.dot(s, w_ref[...].astype(BF16), preferred_element_type=F32) + b_ref[...]


def _adaln(c_all, w, b, li):
    R, D = c_all.shape
    N = w.shape[-1]
    tn = _tile(N, 1024, V7X_LANES)
    return pl.pallas_call(
        _adaln_kernel,
        grid=(N // tn,),
        in_specs=[pl.BlockSpec((R, D), lambda j: (0, 0)),
                  pl.BlockSpec((None, D, tn), lambda j: (li, 0, j)),
                  pl.BlockSpec((None, 1, tn), lambda j: (li, 0, j))],
        out_specs=pl.BlockSpec((R, tn), lambda j: (0, j)),
        out_shape=jax.ShapeDtypeStruct((R, N), F32),
        compiler_params=_cparams(1),
        name="adaln",
    )(c_all, w, b)


def _modulated(x_ref, g_ref, sh_ref, sc_ref):
    x = x_ref[...]
    ms = jnp.mean(x * x, axis=-1, keepdims=True)
    return x * lax.rsqrt(ms + EPS) * g_ref[...] * (1.0 + sc_ref[...]) + sh_ref[...]


def _modulate_kernel(x_ref, g_ref, sh_ref, sc_ref, o_ref):
    o_ref[...] = _modulated(x_ref, g_ref, sh_ref, sc_ref).astype(o_ref.dtype)


def _top2_route(h, wr_ref, n_experts):
    w = wr_ref[...]
    h_hi, w_hi = h.astype(BF16), w.astype(BF16)
    h_lo = (h - h_hi.astype(F32)).astype(BF16)
    w_lo = (w - w_hi.astype(F32)).astype(BF16)
    logits = (jnp.dot(h_hi, w_hi, preferred_element_type=F32) + jnp.dot(h_hi, w_lo, preferred_element_type=F32)
              + jnp.dot(h_lo, w_hi, preferred_element_type=F32))
    lane = _iota(logits.shape, 1)
    logits = jnp.where(lane < n_experts, logits, NEG_BIG)
    m1 = jnp.max(logits, axis=-1, keepdims=True)
    i1 = jnp.min(jnp.where(logits == m1, lane, V7X_LANES), axis=-1, keepdims=True)
    rest = jnp.where(lane == i1, NEG_BIG, logits)
    m2 = jnp.max(rest, axis=-1, keepdims=True)
    i2 = jnp.min(jnp.where(rest == m2, lane, V7X_LANES), axis=-1, keepdims=True)
    e2 = jnp.exp(m2 - m1)
    g1 = 1.0 / (1.0 + e2)
    g2 = e2 / (1.0 + e2)
    return jnp.where(lane == 0, i1.astype(F32),
                     jnp.where(lane == 1, i2.astype(F32),
                               jnp.where(lane == 2, g1, jnp.where(lane == 3, g2, 0.0))))


def _out_proj_kernel(*refs, n_a, n_prompt_steps, n_experts):
    route = n_experts is not None
    it = iter(refs)
    take = lambda n: [next(it) for _ in range(n)]
    prompt = (take(n_a), *take(4))
    sample = (take(n_a), *take(4))
    w_ref, gn_ref = take(2)
    wr_ref = next(it) if route else None
    xo_ref, xso_ref, h_ref, hs_ref = take(4)
    (wb_ref,) = take(1)
    t = pl.program_id(0)
    _cast_weights_once(t == 0, [w_ref], [wb_ref])

    def run(a_refs, x_ref, g_ref, sh_ref, sc_ref):
        y, k0 = None, 0
        for a_ref in a_refs:
            k1 = k0 + a_ref.shape[-1]
            part = jnp.dot(a_ref[...], wb_ref[k0:k1, :], preferred_element_type=F32)
            y = part if y is None else y + part
            k0 = k1
        xn = x_ref[...] + g_ref[...] * y
        ms = jnp.mean(xn * xn, axis=-1, keepdims=True)
        return xn, xn * lax.rsqrt(ms + EPS) * gn_ref[...] * (1.0 + sc_ref[...]) + sh_ref[...]

    @pl.when(t < n_prompt_steps)
    def _():
        xn, h = run(*prompt)
        xo_ref[...] = xn
        if route:
            h_ref[...] = h
            hs_ref[...] = _top2_route(h, wr_ref, n_experts)
        else:
            h_ref[...] = h.astype(h_ref.dtype)

    @pl.when(t == n_prompt_steps)
    def _():
        xn, h = run(*sample)
        xso_ref[...] = xn
        if route:
            n = h.shape[0]
            h_ref[:n, :] = h
            hs_ref[:n, :] = _top2_route(h, wr_ref, n_experts)
        else:
            hs_ref[...] = h.astype(hs_ref.dtype)


def _mod_specs(B, L, D, R, tm):
    xmap = lambda b, i: (b, i, 0)
    return [pl.BlockSpec((None, tm, D), xmap),
            pl.BlockSpec((1, D), lambda b, i: (0, 0)),
            pl.BlockSpec((None, R, D), lambda b, i: (b, 0, 0)),
            pl.BlockSpec((None, R, D), lambda b, i: (b, 0, 0))]


def _modulate(x, g, shift, scale):
    B, L, D = x.shape
    R = shift.shape[1]
    tm = L if R == L else _tile(L, 512, V7X_SUBLANES)
    return pl.pallas_call(
        _modulate_kernel,
        grid=(B, L // tm),
        in_specs=_mod_specs(B, L, D, R, tm),
        out_specs=pl.BlockSpec((None, tm, D), lambda b, i: (b, i, 0)),
        out_shape=jax.ShapeDtypeStruct((B, L, D), BF16),
        compiler_params=_cparams(2),
        name="modulate",
    )(x, g, shift, scale)


def _out_proj(name, acts, w, li, prompt_mods, sample_acts, sample_mods, g_next, router=None, n_experts=None):
    x = prompt_mods[0]
    B, L, N = x.shape
    Bs = sample_mods[0].shape[0]
    K = sum(a.shape[-1] for a in acts)
    tm = _tile(L, 256, V7X_SUBLANES)
    assert Bs <= tm
    nb = L // tm
    n_p = B * nb
    bmap = lambda t: jnp.minimum(t // nb, B - 1)
    imap = lambda t: jnp.where(t < n_p, t % nb, nb - 1)
    row = lambda width: pl.BlockSpec((None, tm, width), lambda t: (bmap(t), imap(t), 0))
    per_b = pl.BlockSpec((None, 1, N), lambda t: (bmap(t), 0, 0))
    whole = lambda *shape: pl.BlockSpec(shape, lambda t: (0,) * len(shape))
    in_specs = ([row(a.shape[-1]) for a in acts] + [row(N), per_b, per_b, per_b]
                + [whole(Bs, a.shape[-1]) for a in sample_acts] + [whole(Bs, N)] * 4
                + [pl.BlockSpec((None, K, N), lambda t: (li, 0, 0), pipeline_mode=pl.Buffered(1)), whole(1, N)])
    args = [*acts, *prompt_mods, *sample_acts, *sample_mods, w, g_next]
    out_specs = [row(N), whole(Bs, N)]
    out_shape = [jax.ShapeDtypeStruct((B, L, N), F32), jax.ShapeDtypeStruct((Bs, N), F32)]
    if router is not None:
        in_specs.append(whole(N, V7X_LANES))
        args.append(router)
        n_tok = B * L + Bs
        out_specs += [pl.BlockSpec((tm, N), lambda t: (t, 0)), pl.BlockSpec((tm, V7X_LANES), lambda t: (t, 0))]
        out_shape += [jax.ShapeDtypeStruct((n_tok, N), F32), jax.ShapeDtypeStruct((n_tok, V7X_LANES), F32)]
    else:
        out_specs += [row(N), whole(Bs, N)]
        out_shape += [jax.ShapeDtypeStruct((B, L, N), BF16), jax.ShapeDtypeStruct((Bs, N), BF16)]
    return pl.pallas_call(
        functools.partial(_out_proj_kernel, n_a=len(acts), n_prompt_steps=n_p, n_experts=n_experts),
        grid=(n_p + 1,),
        in_specs=in_specs,
        out_specs=out_specs,
        out_shape=out_shape,
        scratch_shapes=[pltpu.VMEM((K, N), BF16)],
        compiler_params=_cparams(1),
        name=name,
    )(*args)


def _cast_weights_once(first, w_refs, wb_refs):
    @pl.when(first)
    def _():
        for w_ref, wb_ref in zip(w_refs, wb_refs):
            wb_ref[...] = w_ref[...].astype(BF16)


def _dense_kernel(*refs, swiglu, n_a, n_w, residual, n_prompt_steps):
    it = iter(refs)
    take = lambda n: [next(it) for _ in range(n)]
    a_refs, xg_refs, w_refs = take(n_a), take(2 if residual else 0), take(n_w)
    as_refs, xgs_refs = take(n_a), take(2 if residual else 0)
    o_ref, os_ref = take(2)
    wb_refs = take(n_w)
    t = pl.program_id(1)
    _cast_weights_once(t == 0, w_refs, wb_refs)

    def run(a_refs, xg_refs, o_ref):
        if swiglu:
            a = a_refs[0][...]
            u = jnp.dot(a, wb_refs[0][...], preferred_element_type=F32)
            v = jnp.dot(a, wb_refs[1][...], preferred_element_type=F32)
            y = u * jax.nn.sigmoid(u) * v
        else:
            y, k0 = None, 0
            for a_ref in a_refs:
                k1 = k0 + a_ref.shape[-1]
                part = jnp.dot(a_ref[...], wb_refs[0][k0:k1, :], preferred_element_type=F32)
                y = part if y is None else y + part
                k0 = k1
        if residual:
            y = xg_refs[0][...] + xg_refs[1][...] * y
        o_ref[...] = y.astype(o_ref.dtype)

    pl.when(t < n_prompt_steps)(lambda: run(a_refs, xg_refs, o_ref))
    pl.when(t == n_prompt_steps)(lambda: run(as_refs, xgs_refs, os_ref))


def _mm_tiles(L, K, N, n_w):
    budget = V7X_VMEM_LIMIT_BYTES // 2
    tn = _tile(N, 1024, V7X_LANES)
    while tn > V7X_LANES and n_w * K * tn * (2 * 4 + 2) > budget:
        tn = _tile(N, tn // 2, V7X_LANES)
    tm = _tile(L, 1024 if K <= 2048 else 512, V7X_SUBLANES)
    return tm, tn


def _dense(name, acts, weights, li, out_dtype, sample_acts, resid=None, sample_resid=None):
    B, L = acts[0].shape[:2]
    Bs = sample_acts[0].shape[0]
    K = sum(a.shape[-1] for a in acts)
    N = weights[0].shape[-1]
    tm, tn = _mm_tiles(L, K, N, len(weights))
    nb = L // tm
    n_p = B * nb
    bmap = lambda t: jnp.minimum(t // nb, B - 1)
    imap = lambda t: jnp.where(t < n_p, t % nb, nb - 1)
    row_spec = lambda width: pl.BlockSpec((None, tm, width), lambda j, t: (bmap(t), imap(t), 0))
    col_spec = pl.BlockSpec((None, tm, tn), lambda j, t: (bmap(t), imap(t), j))
    in_specs = [row_spec(a.shape[-1]) for a in acts]
    args = list(acts)
    if resid is not None:
        in_specs += [col_spec, pl.BlockSpec((None, 1, tn), lambda j, t: (bmap(t), 0, j))]
        args += list(resid)
    in_specs += [pl.BlockSpec((None, K, tn), lambda j, t: (li, 0, j)) for _ in weights]
    args += list(weights)
    in_specs += [pl.BlockSpec((Bs, a.shape[-1]), lambda j, t: (0, 0)) for a in sample_acts]
    args += list(sample_acts)
    if resid is not None:
        in_specs += [pl.BlockSpec((Bs, tn), lambda j, t: (0, j)), pl.BlockSpec((Bs, tn), lambda j, t: (0, j))]
        args += list(sample_resid)
    return pl.pallas_call(
        functools.partial(_dense_kernel, swiglu=len(weights) == 2, n_a=len(acts), n_w=len(weights),
                          residual=resid is not None, n_prompt_steps=n_p),
        grid=(N // tn, n_p + 1),
        in_specs=in_specs,
        out_specs=[col_spec, pl.BlockSpec((Bs, tn), lambda j, t: (0, j))],
        out_shape=[jax.ShapeDtypeStruct((B, L, N), out_dtype), jax.ShapeDtypeStruct((Bs, N), out_dtype)],
        scratch_shapes=[pltpu.VMEM((K, tn), BF16) for _ in weights],
        compiler_params=_cparams(2),
        name=name,
    )(*args)


def _diff_lambda(lq_ref, lam_init):
    lq = lq_ref[...]
    a = jnp.sum(lq[0:1] * lq[1:2], axis=-1, keepdims=True)
    b = jnp.sum(lq[2:3] * lq[3:4], axis=-1, keepdims=True)
    return jnp.exp(a) - jnp.exp(b) + lam_init


def _attn_prompt_kernel(q_ref, k_ref, v_ref, lq_ref, sub_ref, o_ref, kt_ref, vo_ref, *, tq, tk, dk, lam_init):
    qi = pl.program_id(2)

    @pl.when(qi == 0)
    def _():
        kt_ref[...] = k_ref[...].T
        vo_ref[...] = v_ref[...]

    hw = 2 * dk
    sc = dk ** -0.5
    q = q_ref[...] * sc
    lane = _iota(q.shape, 1)
    qq = jnp.concatenate([jnp.where(lane < dk, q, 0.0), jnp.where(lane >= dk, q, 0.0)], axis=0).astype(BF16)

    def block(j, carry, masked):
        m, l, acc = carry
        start = pl.multiple_of(j * tk, tk)
        kb = k_ref[pl.ds(start, tk), :].astype(BF16)
        vb = v_ref[pl.ds(start, tk), :].astype(BF16)
        s = lax.dot_general(qq, kb, (((1,), (1,)), ((), ())), preferred_element_type=F32)
        if masked:
            row = _iota(s.shape, 0)
            row = jnp.where(row >= tq, row - tq, row) + qi * tq
            s = jnp.where(_iota(s.shape, 1) + j * tk <= row, s, NEG_BIG)
        m_new = jnp.maximum(m, jnp.max(s, axis=-1, keepdims=True))
        alpha = jnp.exp(m - m_new)
        p = jnp.exp(s - m_new)
        l = alpha * l + jnp.sum(p, axis=-1, keepdims=True)
        acc = alpha * acc + jnp.dot(p.astype(BF16), vb, preferred_element_type=F32)
        return m_new, l, acc

    init = (jnp.full((2 * tq, 1), NEG_BIG, F32), jnp.zeros((2 * tq, 1), F32), jnp.zeros((2 * tq, hw), F32))
    n_full = (qi * tq) // tk
    carry = lax.fori_loop(0, n_full, lambda j, c: block(j, c, False), init)
    m, l, acc = block(n_full, carry, True)
    on = acc / l
    lam = _diff_lambda(lq_ref, lam_init)
    o = on[:tq] - lam * on[tq:]
    ms = jnp.mean(o * o, axis=-1, keepdims=True)
    o_ref[...] = (o * lax.rsqrt(ms + EPS) * sub_ref[...] * (1.0 - lam_init)).astype(o_ref.dtype)


def _attn_prompt(z, lambda_qk, subln, li, n_heads, lam_init):
    B, L, _ = z.shape
    hw = 2 * lambda_qk.shape[-1]
    tq = _tile(L, 512, V7X_SUBLANES)
    tk = _tile(L, 512, V7X_SUBLANES)
    assert tk % tq == 0
    H = n_heads
    return pl.pallas_call(
        functools.partial(_attn_prompt_kernel, tq=tq, tk=tk, dk=hw // 2, lam_init=lam_init),
        grid=(B, H, L // tq),
        in_specs=[pl.BlockSpec((None, tq, hw), lambda b, h, i: (b, i, h)),
                  pl.BlockSpec((None, L, hw), lambda b, h, i: (b, 0, H + h)),
                  pl.BlockSpec((None, L, hw), lambda b, h, i: (b, 0, 2 * H + h)),
                  pl.BlockSpec((None,) + lambda_qk.shape[1:], lambda b, h, i: (li, 0, 0)),
                  pl.BlockSpec((None, 1, hw), lambda b, h, i: (li, 0, h))],
        out_specs=[pl.BlockSpec((None, tq, hw), lambda b, h, i: (b, i, h)),
                   pl.BlockSpec((None, hw, L), lambda b, h, i: (b, h, 0)),
                   pl.BlockSpec((None, L, hw), lambda b, h, i: (b, 0, h))],
        out_shape=[jax.ShapeDtypeStruct((B, L, H * hw), BF16),
                   jax.ShapeDtypeStruct((B, H * hw, L), F32),
                   jax.ShapeDtypeStruct((B, L, H * hw), F32)],
        compiler_params=_cparams(3),
        name="attn_prompt",
    )(z, z, z, lambda_qk, subln)


def _attn_sample_kernel(pt_ref, q_ref, kn_ref, vn_ref, lq_ref, sub_ref, *rest, n_pages, n_heads, dk, lam_init):
    kt_refs = rest[:n_pages]
    v_refs = rest[n_pages:2 * n_pages]
    o_ref, m_ref, l_ref, acc_ref = rest[2 * n_pages:]
    s_id = pl.program_id(1)
    H = n_heads
    hw = 2 * dk
    W = H * hw
    page = kt_refs[0].shape[-1]
    sc = dk ** -0.5

    @pl.when(s_id == 0)
    def _():
        m_ref[...] = jnp.full(m_ref.shape, NEG_BIG, F32)
        l_ref[...] = jnp.zeros(l_ref.shape, F32)
        acc_ref[...] = jnp.zeros(acc_ref.shape, F32)

    row = _iota((2 * H, W), 0)
    lane = _iota((2 * H, W), 1)
    r_head = jnp.where(row >= H, row - H, row)
    r_comp = jnp.where(row >= H, 1, 0)
    qsel = jnp.logical_and(lane // hw == r_head, (lane // dk) % 2 == r_comp)
    qf = jnp.where(qsel, q_ref[...] * sc, 0.0)
    qb = qf.astype(BF16)

    s_parts = [jnp.dot(qb, kt_refs[r][...].astype(BF16), preferred_element_type=F32) for r in range(n_pages)]
    s = jnp.concatenate(s_parts, axis=-1) if n_pages > 1 else s_parts[0]
    m = m_ref[...]
    m_new = jnp.maximum(m, jnp.max(s, axis=-1, keepdims=True))
    alpha = jnp.exp(m - m_new)
    p = jnp.exp(s - m_new)
    l_ref[...] = alpha * l_ref[...] + jnp.sum(p, axis=-1, keepdims=True)
    m_ref[...] = m_new
    p_head = r_head[:, :page]
    pv = jnp.zeros((2 * H, hw), F32)
    for r in range(n_pages):
        pr = p[:, r * page:(r + 1) * page]
        pexp = jnp.concatenate([jnp.where(p_head == h, pr, 0.0) for h in range(H)], axis=1).astype(BF16)
        vh = jnp.concatenate([v_refs[r][pl.ds(h, page, stride=H), :] for h in range(H)], axis=0).astype(BF16)
        pv = pv + jnp.dot(pexp, vh, preferred_element_type=F32)
    acc_ref[...] = alpha * acc_ref[...] + pv

    @pl.when(s_id == pl.num_programs(1) - 1)
    def _():
        s_new = jnp.sum(qf * kn_ref[...], axis=-1, keepdims=True)
        m_old = m_ref[...]
        m_fin = jnp.maximum(m_old, s_new)
        a_fin = jnp.exp(m_old - m_fin)
        p_new = jnp.exp(s_new - m_fin)
        l_fin = a_fin * l_ref[...] + p_new
        vn = vn_ref[...]
        acc = a_fin * acc_ref[...] + p_new * jnp.concatenate([vn, vn], axis=0)
        on = acc / l_fin
        lam = _diff_lambda(lq_ref, lam_init)
        o = on[:H] - lam * on[H:]
        ms = jnp.mean(o * o, axis=-1, keepdims=True)
        o_ref[...] = (o * lax.rsqrt(ms + EPS) * sub_ref[...] * (1.0 - lam_init)).astype(o_ref.dtype)


def _attn_sample(z, v_new, cache_kt, cache_v, page_table, lambda_qk, subln, li, n_heads, lam_init):
    Bs = z.shape[0]
    n_phys_total, W, page = cache_kt.shape
    n_pg = page_table.shape[1]
    P = next(c for c in (8, 4, 2, 1) if n_pg % c == 0)
    n_phys = n_phys_total // lambda_qk.shape[0]
    base = li * n_phys
    H = n_heads
    dk = lambda_qk.shape[-1]
    hw = 2 * dk

    def page_spec(shape, r):
        return pl.BlockSpec((None,) + shape, lambda b, s, pt: (base + pt[b, s * P + r], 0, 0))

    zrow = lambda c: pl.BlockSpec((None, 1, W), lambda b, s, pt: (b, 0, c))
    grid_spec = pltpu.PrefetchScalarGridSpec(
        num_scalar_prefetch=1,
        grid=(Bs, n_pg // P),
        in_specs=[zrow(0), zrow(1),
                  pl.BlockSpec((None, H, hw), lambda b, s, pt: (b, 0, 0)),
                  pl.BlockSpec((None,) + lambda_qk.shape[1:], lambda b, s, pt: (li, 0, 0)),
                  pl.BlockSpec((None, H, hw), lambda b, s, pt: (li, 0, 0))]
                 + [page_spec((W, page), r) for r in range(P)]
                 + [page_spec((page * H, hw), r) for r in range(P)],
        out_specs=pl.BlockSpec((None, H, hw), lambda b, s, pt: (b, 0, 0)),
        scratch_shapes=[pltpu.VMEM((2 * H, 1), F32), pltpu.VMEM((2 * H, 1), F32),
                        pltpu.VMEM((2 * H, hw), F32)],
    )
    return pl.pallas_call(
        functools.partial(_attn_sample_kernel, n_pages=P, n_heads=H, dk=dk, lam_init=lam_init),
        grid_spec=grid_spec,
        out_shape=jax.ShapeDtypeStruct((Bs, H, hw), BF16),
        compiler_params=_cparams(2),
        name="attn_sample",
    )(page_table, z, z, v_new, lambda_qk, subln, *([cache_kt] * P), *([cache_v] * P))


def _rope(x, cos, sin):
    half = x.shape[-1] // 2
    x1, x2 = x[:, :half], x[:, half:]
    return jnp.concatenate([x1 * cos - x2 * sin, x1 * sin + x2 * cos], axis=-1)


def _groupnorm_gate(o, gn, g):
    mu = jnp.mean(o, axis=-1, keepdims=True)
    d = o - mu
    var = jnp.mean(d * d, axis=-1, keepdims=True)
    return d * lax.rsqrt(var + EPS) * gn * (g * jax.nn.sigmoid(g))


def _ret_prompt_kernel(q_ref, k_ref, v_ref, g_ref, cos_ref, sin_ref, lg_ref, gn_ref,
                       o_ref, sfin_ref, s_ref, *, chunk, dk):
    c = pl.program_id(2)

    @pl.when(c == 0)
    def _():
        s_ref[...] = jnp.zeros(s_ref.shape, F32)

    cos, sin = cos_ref[...], sin_ref[...]
    qr = _rope(q_ref[...], cos, sin) * dk ** -0.5
    kr = _rope(k_ref[...], cos, sin)
    vb = v_ref[...].astype(BF16)
    lg = lg_ref[...][:, :1]
    C = chunk
    d = _iota((C, C), 0) - _iota((C, C), 1)
    decay = jnp.where(d >= 0, jnp.exp(jnp.maximum(d, 0).astype(F32) * lg), 0.0)
    qrb = qr.astype(BF16)
    scores = lax.dot_general(qrb, kr.astype(BF16), (((1,), (1,)), ((), ())), preferred_element_type=F32) * decay
    intra = jnp.dot(scores.astype(BF16), vb, preferred_element_type=F32)
    t = _iota((C, 1), 0).astype(F32)
    s_old = s_ref[...]
    inter = jnp.dot(qrb, s_old.astype(BF16), preferred_element_type=F32) * jnp.exp((t + 1.0) * lg)
    kd = kr * jnp.exp((C - 1.0 - t) * lg)
    s_new = jnp.exp(C * lg) * s_old + jnp.dot(kd.T.astype(BF16), vb, preferred_element_type=F32)
    s_ref[...] = s_new
    sfin_ref[...] = s_new
    o_ref[...] = _groupnorm_gate(intra + inter, gn_ref[...], g_ref[...]).astype(o_ref.dtype)


def _ret_prompt(z, cos_tab, sin_tab, log_gamma, gn_b, li, n_heads_a, n_heads_b):
    B, L, _ = z.shape
    dk = 2 * cos_tab.shape[-1]
    HB = n_heads_b
    assert (3 * n_heads_a * V7X_LANES) % dk == 0
    off = 3 * n_heads_a * V7X_LANES // dk
    C = _tile(L, 512, V7X_SUBLANES)
    zb = lambda g: pl.BlockSpec((None, C, dk), lambda b, h, c: (b, c, off + g * HB + h))
    return pl.pallas_call(
        functools.partial(_ret_prompt_kernel, chunk=C, dk=dk),
        grid=(B, HB, L // C),
        in_specs=[zb(0), zb(1), zb(2), zb(3),
                  pl.BlockSpec((C, dk // 2), lambda b, h, c: (c, 0)),
                  pl.BlockSpec((C, dk // 2), lambda b, h, c: (c, 0)),
                  pl.BlockSpec((None, 1, V7X_LANES), lambda b, h, c: (h, 0, 0)),
                  pl.BlockSpec((None, 1, dk), lambda b, h, c: (li, 0, h))],
        out_specs=[pl.BlockSpec((None, C, dk), lambda b, h, c: (b, c, h)),
                   pl.BlockSpec((None, None, dk, dk), lambda b, h, c: (b, h, 0, 0))],
        out_shape=[jax.ShapeDtypeStruct((B, L, HB * dk), BF16),
                   jax.ShapeDtypeStruct((B, HB, dk, dk), F32)],
        scratch_shapes=[pltpu.VMEM((dk, dk), F32)],
        compiler_params=_cparams(3),
        name="ret_prompt",
    )(z, z, z, z, cos_tab, sin_tab, log_gamma, gn_b)


def _ret_sample_kernel(q_ref, k_ref, v_ref, g_ref, cos_ref, sin_ref, lg_ref, gn_ref, s0_ref,
                       o_ref, snew_ref, *, dk):
    cos, sin = cos_ref[...], sin_ref[...]
    qr = _rope(q_ref[...], cos, sin) * dk ** -0.5
    kr = _rope(k_ref[...], cos, sin)
    v = v_ref[...]
    gamma = jnp.exp(lg_ref[...][:, :1])
    eye = _iota((dk, dk), 0) == _iota((dk, dk), 1)
    kcol = jnp.sum(jnp.where(eye, kr, 0.0), axis=-1, keepdims=True)
    qcol = jnp.sum(jnp.where(eye, qr, 0.0), axis=-1, keepdims=True)
    s0 = s0_ref[...]
    snew_ref[...] = gamma * s0 + kcol * v
    inter = gamma * jnp.sum(qcol * s0, axis=0, keepdims=True)
    intra = jnp.sum(qr * kr, axis=-1, keepdims=True) * v
    o_ref[...] = _groupnorm_gate(intra + inter, gn_ref[...], g_ref[...]).astype(o_ref.dtype)


def _ret_sample(z, cos_row, sin_row, log_gamma, gn_b, state, li, n_heads_a, n_heads_b):
    Bs = z.shape[0]
    dk = 2 * cos_row.shape[-1]
    HB = n_heads_b
    off = 3 * n_heads_a * V7X_LANES // dk
    zb = lambda g: pl.BlockSpec((None, 1, dk), lambda b, h: (b, 0, off + g * HB + h))
    return pl.pallas_call(
        functools.partial(_ret_sample_kernel, dk=dk),
        grid=(Bs, HB),
        in_specs=[zb(0), zb(1), zb(2), zb(3),
                  pl.BlockSpec((1, dk // 2), lambda b, h: (0, 0)),
                  pl.BlockSpec((1, dk // 2), lambda b, h: (0, 0)),
                  pl.BlockSpec((None, 1, V7X_LANES), lambda b, h: (h, 0, 0)),
                  pl.BlockSpec((None, 1, dk), lambda b, h: (li, 0, h)),
                  pl.BlockSpec((None, None, dk, dk), lambda b, h: (li * Bs + b, h, 0, 0))],
        out_specs=[pl.BlockSpec((None, 1, dk), lambda b, h: (b, 0, h)),
                   pl.BlockSpec((None, None, dk, dk), lambda b, h: (b, h, 0, 0))],
        out_shape=[jax.ShapeDtypeStruct((Bs, 1, HB * dk), BF16),
                   jax.ShapeDtypeStruct((Bs, HB, dk, dk), F32)],
        compiler_params=_cparams(2),
        name="ret_sample",
    )(z, z, z, z, cos_row, sin_row, log_gamma, gn_b, state)


def _softplus(z):
    return jnp.maximum(z, 0.0) + jnp.log1p(jnp.exp(-jnp.abs(z)))


def _lru_coeffs(xc, wa_ref, wx_ref, ba_ref, bx_ref, lam_ref):
    xcb = xc.astype(BF16)
    r = jax.nn.sigmoid(jnp.dot(xcb, wa_ref[...].astype(BF16), preferred_element_type=F32) + ba_ref[...])
    gi = jax.nn.sigmoid(jnp.dot(xcb, wx_ref[...].astype(BF16), preferred_element_type=F32) + bx_ref[...])
    log_a = -LRU_C * r * _softplus(-lam_ref[...])
    a = jnp.exp(log_a)
    t = jnp.tanh(log_a)
    mult = jnp.sqrt(jnp.maximum(-2.0 * t / (1.0 - t), 0.0))
    return a, mult * gi * xc


def _lru_prompt_kernel(gate_ref, x_ref, cw_ref, cb_ref, wa_ref, wx_ref, ba_ref, bx_ref, lam_ref,
                       tail0_ref, h0_ref, y_ref, hl_ref, tail_ref, hc_ref, *, tt, conv_w):
    t_id = pl.program_id(2)

    @pl.when(t_id == 0)
    def _():
        tail_ref[...] = tail0_ref[...]
        hc_ref[...] = h0_ref[...]

    x = x_ref[...]
    tl = tail_ref[...]
    row8 = _iota(tl.shape, 0)
    xc = cb_ref[...] + cw_ref[conv_w - 1:conv_w, :] * x
    for s in range(1, conv_w):
        xs = pltpu.roll(x, s, 0)
        head = jnp.where(row8 >= s, xs[:V7X_SUBLANES], pltpu.roll(tl, s, 0))
        xs = jnp.concatenate([head, xs[V7X_SUBLANES:]], axis=0) if tt > V7X_SUBLANES else head
        xc = xc + cw_ref[conv_w - 1 - s:conv_w - s, :] * xs
    tail_ref[...] = x[tt - V7X_SUBLANES:, :]

    a, b = _lru_coeffs(xc, wa_ref, wx_ref, ba_ref, bx_ref, lam_ref)
    row = _iota(a.shape, 0)
    s = 1
    while s < tt:
        if s % V7X_SUBLANES == 0:
            a_s = jnp.concatenate([jnp.ones((s, a.shape[1]), F32), a[:tt - s]], axis=0)
            b_s = jnp.concatenate([jnp.zeros((s, a.shape[1]), F32), b[:tt - s]], axis=0)
        else:
            keep = row >= s
            a_s = jnp.where(keep, pltpu.roll(a, s, 0), 1.0)
            b_s = jnp.where(keep, pltpu.roll(b, s, 0), 0.0)
        b = b + a * b_s
        a = a * a_s
        s *= 2
    h = b + a * hc_ref[...]
    hc_ref[...] = h[tt - 1:tt, :]
    hl_ref[...] = h[tt - 1:tt, :]
    y_ref[...] = (jax.nn.gelu(gate_ref[...], approximate=True) * h).astype(y_ref.dtype)


def _lru_prompt(z, conv_w, conv_b, w_ga, w_gx, b_ga, b_gx, lam, tail0, h0, li):
    B, L, W2 = z.shape
    W = W2 // 2
    bw = w_ga.shape[-1]
    NB = W // bw
    CW = conv_w.shape[1]
    tt = _tile(L, 512, V7X_SUBLANES)
    vec = lambda: pl.BlockSpec((None, 1, bw), lambda b, n, t: (li, 0, n))
    gw = lambda: pl.BlockSpec((None, bw, bw), lambda b, n, t: (li * NB + n, 0, 0))
    return pl.pallas_call(
        functools.partial(_lru_prompt_kernel, tt=tt, conv_w=CW),
        grid=(B, NB, L // tt),
        in_specs=[pl.BlockSpec((None, tt, bw), lambda b, n, t: (b, t, n)),
                  pl.BlockSpec((None, tt, bw), lambda b, n, t: (b, t, NB + n)),
                  pl.BlockSpec((None, CW, bw), lambda b, n, t: (li, 0, n)),
                  vec(), gw(), gw(), vec(), vec(), vec(),
                  pl.BlockSpec((None, V7X_SUBLANES, bw), lambda b, n, t: (b, 0, n)),
                  pl.BlockSpec((None, 1, bw), lambda b, n, t: (b, 0, n))],
        out_specs=[pl.BlockSpec((None, tt, bw), lambda b, n, t: (b, t, n)),
                   pl.BlockSpec((None, 1, bw), lambda b, n, t: (b, 0, n))],
        out_shape=[jax.ShapeDtypeStruct((B, L, W), BF16), jax.ShapeDtypeStruct((B, 1, W), F32)],
        scratch_shapes=[pltpu.VMEM((V7X_SUBLANES, bw), F32), pltpu.VMEM((1, bw), F32)],
        compiler_params=_cparams(3),
        name="lru_prompt",
    )(z, z, conv_w, conv_b, w_ga, w_gx, b_ga, b_gx, lam, tail0, h0)


def _lru_sample_kernel(gate_ref, x_ref, cw_ref, cb_ref, wa_ref, wx_ref, ba_ref, bx_ref, lam_ref,
                       buf_ref, h0_ref, y_ref, h_ref, *, conv_w):
    x = x_ref[...]
    xc = cb_ref[...] + cw_ref[conv_w - 1:conv_w, :] * x
    for j in range(conv_w - 1):
        xc = xc + cw_ref[j:j + 1, :] * buf_ref[j]
    a, b = _lru_coeffs(xc, wa_ref, wx_ref, ba_ref, bx_ref, lam_ref)
    h = a * h0_ref[...] + b
    h_ref[...] = h
    y_ref[...] = (jax.nn.gelu(gate_ref[...], approximate=True) * h).astype(y_ref.dtype)


def _lru_sample(z, conv_w, conv_b, w_ga, w_gx, b_ga, b_gx, lam, buf, h0, li):
    Bs, W2 = z.shape
    W = W2 // 2
    bw = w_ga.shape[-1]
    NB = W // bw
    CW = conv_w.shape[1]
    vec = lambda: pl.BlockSpec((None, 1, bw), lambda n: (li, 0, n))
    gw = lambda: pl.BlockSpec((None, bw, bw), lambda n: (li * NB + n, 0, 0))
    return pl.pallas_call(
        functools.partial(_lru_sample_kernel, conv_w=CW),
        grid=(NB,),
        in_specs=[pl.BlockSpec((Bs, bw), lambda n: (0, n)),
                  pl.BlockSpec((Bs, bw), lambda n: (0, NB + n)),
                  pl.BlockSpec((None, CW, bw), lambda n: (li, 0, n)),
                  vec(), gw(), gw(), vec(), vec(), vec(),
                  pl.BlockSpec((CW - 1, Bs, bw), lambda n: (0, 0, n)),
                  pl.BlockSpec((Bs, bw), lambda n: (0, n))],
        out_specs=[pl.BlockSpec((Bs, bw), lambda n: (0, n)),
                   pl.BlockSpec((Bs, bw), lambda n: (0, n))],
        out_shape=[jax.ShapeDtypeStruct((Bs, W), BF16), jax.ShapeDtypeStruct((Bs, W), F32)],
        compiler_params=_cparams(1),
        name="lru_sample",
    )(z, z, conv_w, conv_b, w_ga, w_gx, b_ga, b_gx, lam, buf, h0)


MOE_ROW_BLOCK = 512
MOE_SUB_BLOCK = 128


def _valid_rows_only(n_valid, o_ref, compute):
    tm = o_ref.shape[0]

    @pl.when(n_valid == tm)
    def _():
        o_ref[...] = compute(slice(None))

    for s in range(tm // MOE_SUB_BLOCK):
        rows = slice(s * MOE_SUB_BLOCK, (s + 1) * MOE_SUB_BLOCK)

        @pl.when(jnp.logical_and(n_valid < tm, n_valid > s * MOE_SUB_BLOCK))
        def _(rows=rows):
            o_ref[rows, :] = compute(rows)

        @pl.when(n_valid <= s * MOE_SUB_BLOCK)
        def _(rows=rows):
            o_ref[rows, :] = jnp.zeros((MOE_SUB_BLOCK, o_ref.shape[1]), o_ref.dtype)


def _stream_weight_panels(bf_ref, sg_ref, se_ref, ns_ref, w_hbms, wf_refs, wb_refs, sem, *, e0, tn):
    j = pl.program_id(0)
    rb = pl.program_id(1)
    n_seg = ns_ref[0]

    def copies(k, jcol, slot):
        col = pl.multiple_of(jcol * tn, tn)
        return [pltpu.make_async_copy(w.at[e0 + se_ref[k], :, pl.ds(col, tn)], wf.at[slot], sem.at[i, slot])
                for i, (w, wf) in enumerate(zip(w_hbms, wf_refs))]

    @pl.when(bf_ref[rb] == 1)
    def _():
        k = sg_ref[rb]
        slot = jnp.bitwise_and(j * n_seg + k, 1)

        @pl.when(jnp.logical_and(j == 0, k == 0))
        def _():
            for c in copies(0, 0, 0):
                c.start()

        for c in copies(k, j, slot):
            c.wait()
        for wf, wb in zip(wf_refs, wb_refs):
            wb[...] = wf[slot].astype(BF16)

        last = k + 1 == n_seg
        k2 = jnp.where(last, 0, k + 1)
        j2 = jnp.where(last, j + 1, j)

        @pl.when(j2 < pl.num_programs(0))
        def _():
            for c in copies(k2, j2, 1 - slot):
                c.start()


def _moe_up_kernel(bf_ref, bv_ref, sg_ref, se_ref, ns_ref, x_ref, w1_hbm, w3_hbm, o_ref,
                   w1f_ref, w3f_ref, w1b_ref, w3b_ref, sem, *, e0, tn):
    rb = pl.program_id(1)
    _stream_weight_panels(bf_ref, sg_ref, se_ref, ns_ref, [w1_hbm, w3_hbm], [w1f_ref, w3f_ref],
                          [w1b_ref, w3b_ref], sem, e0=e0, tn=tn)

    def compute(rows):
        x = x_ref[rows, :]
        u = jnp.dot(x, w1b_ref[...], preferred_element_type=F32)
        v = jnp.dot(x, w3b_ref[...], preferred_element_type=F32)
        return (u * jax.nn.sigmoid(u) * v).astype(o_ref.dtype)

    _valid_rows_only(bv_ref[rb], o_ref, compute)


def _moe_down_kernel(bf_ref, bv_ref, sg_ref, se_ref, ns_ref, a_ref, w_hbm, o_ref, wf_ref, wb_ref, sem, *, e0, tn):
    rb = pl.program_id(1)
    _stream_weight_panels(bf_ref, sg_ref, se_ref, ns_ref, [w_hbm], [wf_ref], [wb_ref], sem, e0=e0, tn=tn)

    def compute(rows):
        return jnp.dot(a_ref[rows, :], wb_ref[...], preferred_element_type=F32)

    _valid_rows_only(bv_ref[rb], o_ref, compute)


def _moe_up(xs, w1, w3, seg, li, n_experts):
    n_rows, D = xs.shape
    F = w1.shape[-1]
    tm = MOE_ROW_BLOCK
    tn = _tile(F, 1024, V7X_LANES)
    rows = lambda j, rb, *_: (rb, 0)
    grid_spec = pltpu.PrefetchScalarGridSpec(
        num_scalar_prefetch=5,
        grid=(F // tn, n_rows // tm),
        in_specs=[pl.BlockSpec((tm, D), rows), pl.BlockSpec(memory_space=pl.ANY),
                  pl.BlockSpec(memory_space=pl.ANY)],
        out_specs=pl.BlockSpec((tm, tn), lambda j, rb, *_: (rb, j)),
        scratch_shapes=[pltpu.VMEM((2, D, tn), F32), pltpu.VMEM((2, D, tn), F32),
                        pltpu.VMEM((D, tn), BF16), pltpu.VMEM((D, tn), BF16),
                        pltpu.SemaphoreType.DMA((2, 2))],
    )
    return pl.pallas_call(
        functools.partial(_moe_up_kernel, e0=li * n_experts, tn=tn),
        grid_spec=grid_spec,
        out_shape=jax.ShapeDtypeStruct((n_rows, F), BF16),
        compiler_params=_cparams(2),
        name="moe_up",
    )(*seg, xs, w1, w3)


def _moe_down(hh, w2, seg, li, n_experts):
    n_rows, F = hh.shape
    D = w2.shape[-1]
    tm = MOE_ROW_BLOCK
    tn = _tile(D, 512, V7X_LANES)
    grid_spec = pltpu.PrefetchScalarGridSpec(
        num_scalar_prefetch=5,
        grid=(D // tn, n_rows // tm),
        in_specs=[pl.BlockSpec((tm, F), lambda j, rb, *_: (rb, 0)), pl.BlockSpec(memory_space=pl.ANY)],
        out_specs=pl.BlockSpec((tm, tn), lambda j, rb, *_: (rb, j)),
        scratch_shapes=[pltpu.VMEM((2, F, tn), F32), pltpu.VMEM((F, tn), BF16),
                        pltpu.SemaphoreType.DMA((1, 2))],
    )
    return pl.pallas_call(
        functools.partial(_moe_down_kernel, e0=li * n_experts, tn=tn),
        grid_spec=grid_spec,
        out_shape=jax.ShapeDtypeStruct((n_rows, D), F32),
        compiler_params=_cparams(2),
        name="moe_down",
    )(*seg, hh, w2)


def _rank_kernel(e_ref, rank_ref, cnt_ref):
    n_chunks, width = e_ref.shape
    rows = cnt_ref.shape[0]
    expert = _iota((rows, width), 0)
    tri = jnp.where(_iota((width, width), 0) <= _iota((width, width), 1), 1.0, 0.0).astype(BF16)

    def body(c, carry):
        hit = jnp.where(e_ref[pl.ds(c, 1), :] == expert, 1.0, 0.0)
        upto = jnp.dot(hit.astype(BF16), tri, preferred_element_type=F32)
        rank = jnp.sum(hit * (upto - 1.0 + carry), axis=0, keepdims=True)
        rank_ref[pl.ds(c, 1), :] = rank.astype(jnp.int32)
        return carry + upto[:, width - 1:width]

    total = lax.fori_loop(0, n_chunks, body, jnp.zeros((rows, 1), F32))
    cnt_ref[...] = jnp.broadcast_to(total, cnt_ref.shape).astype(jnp.int32)


def _expert_ranks(e_flat, n_experts):
    n_slots = e_flat.shape[0]
    n_chunks = -(-n_slots // V7X_LANES)
    pad = n_chunks * V7X_LANES - n_slots
    e2 = jnp.concatenate([e_flat, jnp.full((pad,), n_experts, jnp.int32)]).reshape(n_chunks, V7X_LANES)
    rows = -(-n_experts // V7X_SUBLANES) * V7X_SUBLANES
    rank, cnt = pl.pallas_call(
        _rank_kernel,
        out_shape=[jax.ShapeDtypeStruct((n_chunks, V7X_LANES), jnp.int32),
                   jax.ShapeDtypeStruct((rows, V7X_LANES), jnp.int32)],
        name="moe_rank",
    )(e2)
    return rank.reshape(-1)[:n_slots], cnt[:n_experts, 0]


def _dispatch_kernel(src_ref, bv_ref, tab_ref, o_ref, buf_ref, *, tm):
    rb = pl.program_id(1)

    @pl.when(bv_ref[rb] > 0)
    def _():
        base = rb * tm

        def body(r, carry):
            buf_ref[pl.ds(r, 1), :] = tab_ref[pl.ds(src_ref[base + r], 1), :]
            return carry

        lax.fori_loop(0, tm, body, 0, unroll=8)
        o_ref[...] = buf_ref[...].astype(o_ref.dtype)

    @pl.when(bv_ref[rb] == 0)
    def _():
        o_ref[...] = jnp.zeros(o_ref.shape, o_ref.dtype)


def _dispatch(table, src, blk_valid):
    n_tok, D = table.shape
    n_rows = src.shape[0]
    tm = MOE_ROW_BLOCK
    tc = D
    while tc > V7X_LANES and n_tok * tc * 4 > V7X_RESIDENT_PANEL_BYTES:
        tc = _tile(D, tc // 2, V7X_LANES)
    grid_spec = pltpu.PrefetchScalarGridSpec(
        num_scalar_prefetch=2,
        grid=(D // tc, n_rows // tm),
        in_specs=[pl.BlockSpec((n_tok, tc), lambda j, rb, src, nu: (0, j), pipeline_mode=pl.Buffered(1))],
        out_specs=pl.BlockSpec((tm, tc), lambda j, rb, src, nu: (rb, j)),
        scratch_shapes=[pltpu.VMEM((tm, tc), F32)],
    )
    return pl.pallas_call(
        functools.partial(_dispatch_kernel, tm=tm),
        grid_spec=grid_spec,
        out_shape=jax.ShapeDtypeStruct((n_rows, D), BF16),
        compiler_params=_cparams(2),
        name="moe_dispatch",
    )(src, blk_valid, table)


def _combine_kernel(d_ref, tab_ref, route_ref, o_ref, a_ref, b_ref, *, tm):
    base = pl.program_id(1) * (tm * TOP_K)

    def body(t, carry):
        a_ref[pl.ds(t, 1), :] = tab_ref[pl.ds(d_ref[base + TOP_K * t], 1), :]
        b_ref[pl.ds(t, 1), :] = tab_ref[pl.ds(d_ref[base + TOP_K * t + 1], 1), :]
        return carry

    lax.fori_loop(0, tm, body, 0, unroll=8)
    rt = route_ref[...]
    o_ref[...] = rt[:, TOP_K:TOP_K + 1] * a_ref[...] + rt[:, TOP_K + 1:TOP_K + 2] * b_ref[...]


def _combine(ys, dest, route):
    assert TOP_K == 2
    n_rows, D = ys.shape
    n_tok = route.shape[0]
    tm = _tile(n_tok, 512, V7X_SUBLANES)
    tc = _tile(D, 512, V7X_LANES)
    while tc > V7X_LANES and n_rows * tc * 4 > V7X_RESIDENT_PANEL_BYTES:
        tc = _tile(D, tc // 2, V7X_LANES)
    grid_spec = pltpu.PrefetchScalarGridSpec(
        num_scalar_prefetch=1,
        grid=(D // tc, n_tok // tm),
        in_specs=[pl.BlockSpec((n_rows, tc), lambda j, i, d: (0, j), pipeline_mode=pl.Buffered(1)),
                  pl.BlockSpec((tm, V7X_LANES), lambda j, i, d: (i, 0))],
        out_specs=pl.BlockSpec((tm, tc), lambda j, i, d: (i, j)),
        scratch_shapes=[pltpu.VMEM((tm, tc), F32), pltpu.VMEM((tm, tc), F32)],
    )
    return pl.pallas_call(
        functools.partial(_combine_kernel, tm=tm),
        grid_spec=grid_spec,
        out_shape=jax.ShapeDtypeStruct((n_tok, D), F32),
        compiler_params=_cparams(2),
        name="moe_combine",
    )(dest, ys, route)


def _residual_kernel(x_ref, y_ref, g_ref, o_ref):
    o_ref[...] = x_ref[...] + g_ref[...] * y_ref[...]


def _residual_norm_kernel(x_ref, y_ref, g_ref, nf_ref, o_ref):
    x = x_ref[...] + g_ref[...] * y_ref[...]
    ms = jnp.mean(x * x, axis=-1, keepdims=True)
    o_ref[...] = x * lax.rsqrt(ms + EPS) * nf_ref[...]


def _norm_kernel(x_ref, nf_ref, o_ref):
    x = x_ref[...]
    ms = jnp.mean(x * x, axis=-1, keepdims=True)
    o_ref[...] = x * lax.rsqrt(ms + EPS) * nf_ref[...]


def _residual(x, y_tok, row0, gate, norm_g):
    B, L, D = x.shape
    R = gate.shape[1]
    tm = L if R == L else _tile(L, 512, V7X_SUBLANES)
    nb = L // tm
    assert row0 % tm == 0
    blk0 = row0 // tm
    blk = pl.BlockSpec((None, tm, D), lambda b, i: (b, i, 0))
    in_specs = [blk, pl.BlockSpec((tm, D), lambda b, i: (blk0 + b * nb + i, 0)),
                pl.BlockSpec((None, R, D), lambda b, i: (b, 0, 0))]
    args = [x, y_tok, gate]
    kern = _residual_kernel
    if norm_g is not None:
        in_specs.append(pl.BlockSpec((1, D), lambda b, i: (0, 0)))
        args.append(norm_g)
        kern = _residual_norm_kernel
    return pl.pallas_call(
        kern, grid=(B, L // tm), in_specs=in_specs, out_specs=blk,
        out_shape=jax.ShapeDtypeStruct((B, L, D), F32),
        compiler_params=_cparams(2), name="residual",
    )(*args)


def _final_norm(x, norm_g):
    B, L, D = x.shape
    tm = _tile(L, 512, V7X_SUBLANES)
    blk = pl.BlockSpec((None, tm, D), lambda b, i: (b, i, 0))
    return pl.pallas_call(
        _norm_kernel, grid=(B, L // tm),
        in_specs=[blk, pl.BlockSpec((1, D), lambda b, i: (0, 0))], out_specs=blk,
        out_shape=jax.ShapeDtypeStruct((B, L, D), F32),
        compiler_params=_cparams(2), name="final_norm",
    )(x, norm_g)


def _split_mods(m, n_p, n_s):
    D = m.shape[-1] // 6
    mp = [m[:n_p, k * D:(k + 1) * D][:, None, :] for k in range(6)]
    ms = [m[n_p:n_p + n_s, k * D:(k + 1) * D][None, :, :] for k in range(6)]
    return mp, ms


def _stack_cond(c_prompt, c_sample):
    c_all = jnp.concatenate([c_prompt, c_sample], axis=0)
    pad = (-c_all.shape[0]) % V7X_SUBLANES
    if pad:
        c_all = jnp.concatenate([c_all, jnp.zeros((pad, c_all.shape[1]), c_all.dtype)], axis=0)
    return c_all


def _moe(table, route, w1, w3, w2, li, n_experts):
    T = table.shape[0]
    E = n_experts
    tm = MOE_ROW_BLOCK
    e_flat = route[:, :TOP_K].astype(jnp.int32).reshape(-1)
    n_slots = T * TOP_K
    onehot = (e_flat[:, None] == jnp.arange(E, dtype=jnp.int32)[None, :]).astype(jnp.int32)
    rank, counts = _expert_ranks(e_flat, E)
    padded = (counts + tm - 1) // tm * tm
    pends = jnp.cumsum(padded)
    pstarts = pends - padded
    rem = counts % tm
    rem_s = jnp.sum(onehot * rem[None, :], axis=-1)
    pos = jnp.where(rank < rem_s, rank, rank + (tm - rem_s) % tm)
    dest = (jnp.sum(onehot * pstarts[None, :], axis=-1) + pos).astype(jnp.int32)
    n_blocks = -(-n_slots // tm) + E
    n_rows = n_blocks * tm
    tok_flat = jnp.arange(n_slots, dtype=jnp.int32) // TOP_K
    src = jnp.zeros((n_rows,), jnp.int32).at[dest].set(tok_flat)
    blk_start = jnp.arange(n_blocks, dtype=jnp.int32) * tm
    blk_e = jnp.minimum(jnp.sum((blk_start[:, None] >= pends[None, :]).astype(jnp.int32), axis=-1), E - 1)
    blk_first = jnp.concatenate([jnp.ones((1,), jnp.int32), (blk_e[1:] != blk_e[:-1]).astype(jnp.int32)])
    rem_b = jnp.sum((blk_e[:, None] == jnp.arange(E, dtype=jnp.int32)[None, :]) * rem[None, :], axis=-1)
    blk_valid = jnp.where(blk_start >= pends[-1], 0,
                          jnp.where(jnp.logical_and(blk_first == 1, rem_b > 0), rem_b, tm)).astype(jnp.int32)
    present = (counts > 0).astype(jnp.int32)
    seg_of_e = jnp.cumsum(present) - 1
    eids = jnp.arange(E, dtype=jnp.int32)
    seg_e = jnp.sum(eids[None, :] * present[None, :] * (seg_of_e[None, :] == eids[:, None]), axis=-1).astype(jnp.int32)
    n_seg = jnp.sum(present, keepdims=True).astype(jnp.int32)
    seg_first = (blk_first * (blk_valid > 0)).astype(jnp.int32)
    seg_idx = jnp.sum((blk_e[:, None] == eids[None, :]) * seg_of_e[None, :], axis=-1).astype(jnp.int32)
    seg = (seg_first, blk_valid, seg_idx, seg_e, n_seg)
    xs = _dispatch(table, src, blk_valid)
    hh = _moe_up(xs, w1, w3, seg, li, E)
    ys = _moe_down(hh, w2, seg, li, E)
    return _combine(ys, dest, route)


def kernel(x_prompt, x_sample, c_prompt, c_sample, cache_k, cache_v, state_ret, state_conv, state_lru, page_table, ada_w_even, ada_b_even, norm_mix_even, norm_ffn_even, w_in_even, lambda_qk, subln_a, gn_b, w_out_even, ffn_w1, ffn_w3, ffn_w2, ada_w_odd, ada_b_odd, norm_mix_odd, norm_ffn_odd, w_in_odd, conv_w, conv_b, w_gate_a, b_gate_a, w_gate_x, b_gate_x, lru_lambda, w_out_odd, router, moe_w1, moe_w3, moe_w2, norm_final):
    Bp, Lp, D = x_prompt.shape
    Bs, Ls, _ = x_sample.shape
    assert Ls == 1, "the sample group carries one new token per sequence"
    n_even, n_odd = w_in_even.shape[0], w_in_odd.shape[0]
    depth = n_even + n_odd
    _, n_phys, page, HA, _, dka = cache_k.shape
    HB, dkb = state_ret.shape[2], state_ret.shape[3]
    past_len = page_table.shape[1] * page
    E = router.shape[-1]
    CW = conv_w.shape[1]
    W = conv_w.shape[-1]

    half = dkb // 2
    freqs = ROPE_BASE ** (-jnp.arange(half, dtype=F32) / half)
    ang_p = jnp.arange(Lp).astype(F32)[:, None] * freqs[None, :]
    ang_s = (past_len + jnp.arange(Ls)).astype(F32)[:, None] * freqs[None, :]
    cos_p, sin_p, cos_s, sin_s = jnp.cos(ang_p), jnp.sin(ang_p), jnp.cos(ang_s), jnp.sin(ang_s)
    log_gamma = jnp.log1p(-jnp.exp2(-5.0 - jnp.arange(HB, dtype=F32)))
    log_gamma = jnp.broadcast_to(log_gamma[:, None, None], (HB, 1, V7X_LANES))

    c_all = _stack_cond(c_prompt, c_sample)
    row3 = lambda a: a.reshape(a.shape[0], 1, a.shape[-1])
    cache_kt = jnp.transpose(cache_k, (0, 1, 3, 4, 5, 2)).reshape(n_even * n_phys, HA * 2 * dka, page)
    cache_v2 = cache_v.reshape(n_even * n_phys, page * HA, 2 * dka)
    subln3 = subln_a.reshape(n_even, HA, 2 * dka)
    state_ret2 = state_ret.reshape(n_even * Bs, HB, dkb, dkb)
    w_ga2 = w_gate_a.reshape(-1, w_gate_a.shape[-2], w_gate_a.shape[-1])
    w_gx2 = w_gate_x.reshape(-1, w_gate_x.shape[-2], w_gate_x.shape[-1])
    moe_w1f = moe_w1.reshape(-1, moe_w1.shape[-2], moe_w1.shape[-1])
    moe_w3f = moe_w3.reshape(-1, moe_w3.shape[-2], moe_w3.shape[-1])
    moe_w2f = moe_w2.reshape(-1, moe_w2.shape[-2], moe_w2.shape[-1])
    router_pad = jnp.concatenate(
        [router, jnp.zeros(router.shape[:-1] + (V7X_LANES - E,), router.dtype)], axis=-1)
    nf = norm_final.reshape(1, D)

    xp = x_prompt
    xs = x_sample.reshape(1, Bs, D)
    kp_l, vp_l, rp_l, cp_l, lp_l = [], [], [], [], []
    ks_l, vs_l, rs_l, cs_l, ls_l = [], [], [], [], []
    normed = False
    for l in range(depth):
        i = l // 2
        last = l == depth - 1
        if l % 2 == 0:
            lam_init = 0.8 - 0.6 * math.exp(-0.3 * l)
            m = _adaln(c_all, ada_w_even, row3(ada_b_even), i)
            (sh1, sc1, g1, sh2, sc2, g2), (sh1s, sc1s, g1s, sh2s, sc2s, g2s) = _split_mods(m, Bp, Bs)
            gmix, gffn = norm_mix_even[i][None, :], norm_ffn_even[i][None, :]
            wa = HA * 2 * dka
            h = _modulate(xp, gmix, sh1, sc1)
            hs = _modulate(xs, gmix, sh1s, sc1s)
            z, zs = _dense("mm_in_even", [h], [w_in_even], i, F32, [hs[0]])
            oa, k_t, v_new = _attn_prompt(z, lambda_qk, row3(subln_a), i, HA, lam_init)
            kp_l.append(jnp.transpose(k_t.reshape(Bp, HA, 2, dka, Lp), (0, 4, 1, 2, 3)))
            vp_l.append(v_new.reshape(Bp, Lp, HA, 2 * dka))
            ob, s_fin = _ret_prompt(z, cos_p, sin_p, log_gamma, row3(gn_b), i, HA, HB)
            rp_l.append(s_fin)
            zs3 = zs.reshape(Bs, 1, zs.shape[-1])
            ks_l.append(zs3[:, :, wa:2 * wa].reshape(Bs, Ls, HA, 2, dka))
            v_new = zs3[:, :, 2 * wa:3 * wa].reshape(Bs, HA, 2 * dka)
            vs_l.append(v_new.reshape(Bs, Ls, HA, 2 * dka))
            oas = _attn_sample(zs3, v_new, cache_kt, cache_v2, page_table, lambda_qk, subln3, i, HA, lam_init)
            obs, s_new = _ret_sample(zs3, cos_s, sin_s, log_gamma, row3(gn_b), state_ret2, i, HA, HB)
            rs_l.append(s_new)
            xp, xs2, h, hs = _out_proj("out_even", [oa, ob], w_out_even, i, (xp, g1, sh2, sc2),
                                       [oas.reshape(Bs, wa), obs.reshape(Bs, D - wa)],
                                       (xs[0], g1s[0], sh2s[0], sc2s[0]), gffn)
            xs = xs2[None]
            hh, hhs = _dense("mm_swiglu", [h], [ffn_w1, ffn_w3], i, BF16, [hs])
            xp, xs2 = _dense("mm_ffn_down", [hh], [ffn_w2], i, F32, [hhs], (xp, g2), (xs[0], g2s[0]))
            xs = xs2[None]
        else:
            m = _adaln(c_all, ada_w_odd, row3(ada_b_odd), i)
            (sh1, sc1, g1, sh2, sc2, g2), (sh1s, sc1s, g1s, sh2s, sc2s, g2s) = _split_mods(m, Bp, Bs)
            gmix, gffn = norm_mix_odd[i][None, :], norm_ffn_odd[i][None, :]
            lru_args = (conv_w, row3(conv_b), w_ga2, w_gx2, row3(b_gate_a), row3(b_gate_x), row3(lru_lambda))
            h = _modulate(xp, gmix, sh1, sc1)
            hs = _modulate(xs, gmix, sh1s, sc1s)
            z, zs = _dense("mm_in_odd", [h], [w_in_odd], i, F32, [hs[0]])
            tail0 = jnp.zeros((Bp, V7X_SUBLANES, W), F32)
            h0 = jnp.zeros((Bp, 1, W), F32)
            y, h_last = _lru_prompt(z, *lru_args, tail0, h0, i)
            cp_l.append(z[:, Lp - (CW - 1):, W:])
            lp_l.append(h_last.reshape(Bp, W))
            buf = state_conv[i]
            ys, h_new = _lru_sample(zs, *lru_args, jnp.swapaxes(buf, 0, 1), state_lru[i], i)
            cs_l.append(jnp.concatenate([buf[:, 1:], zs[:, None, W:]], axis=1))
            ls_l.append(h_new)
            xp, xs2, table, route = _out_proj("out_odd", [y], w_out_odd, i, (xp, g1, sh2, sc2), [ys],
                                              (xs[0], g1s[0], sh2s[0], sc2s[0]), gffn, router_pad[i], E)
            xs = xs2[None]
            Tp = Bp * Lp
            y_tok = _moe(table, route, moe_w1f, moe_w3f, moe_w2f, i, E)
            xp = _residual(xp, y_tok, 0, g2, nf if last else None)
            xs = _residual(xs, y_tok, Tp, g2s, nf if last else None)
            normed = last
    if not normed:
        xp = _final_norm(xp, nf)
        xs = _final_norm(xs, nf)
    y_prompt = xp.astype(x_prompt.dtype)
    y_sample = xs.reshape(Bs, Ls, D).astype(x_sample.dtype)
    return (y_prompt, y_sample,
            jnp.stack(kp_l), jnp.stack(vp_l), jnp.stack(rp_l), jnp.stack(cp_l), jnp.stack(lp_l),
            jnp.stack(ks_l), jnp.stack(vs_l), jnp.stack(rs_l), jnp.stack(cs_l), jnp.stack(ls_l))
```
